```python
import math
import jax, jax.numpy as jnp
from jax import lax
import numpy as np

D_MODEL = 1024
BATCH = 2
SEQ = 8192
DEPTH = 2
DEC_BATCH = 128
DEC_SEQ = 4
PAST_LEN = 16384
PAGE_SIZE = 128

N_HEADS = 8
N_KV_HEADS = 2
HEAD_DIM = 64
Q_PER_KV = N_HEADS // N_KV_HEADS
WINDOW = 128
ATTN_BLOCK = 128
D_ATTN = N_HEADS * HEAD_DIM
D_KV = N_KV_HEADS * HEAD_DIM
D_CONV = D_MODEL // 4
CONV_WIDTH = 31
D_SSM = D_MODEL // 4
SSM_GROUP_CH = 16
SSM_GROUPS = D_SSM // SSM_GROUP_CH
SSM_STATE = 64
DT_MIN = 0.001
DT_MAX = 0.1
D_MIX = D_ATTN + D_CONV + D_SSM
D_IN = D_ATTN + 2 * D_KV + 2 * D_CONV + D_SSM
IN_SPLITS = [D_ATTN, D_ATTN + D_KV, D_ATTN + 2 * D_KV, D_ATTN + 2 * D_KV + D_CONV, D_ATTN + 2 * D_KV + 2 * D_CONV]
N_EXPERT_GROUPS = 4
EXPERTS_PER_GROUP = 8
N_EXPERTS = N_EXPERT_GROUPS * EXPERTS_PER_GROUP
TOP_K_INNER = 2
D_EXPERT = 512
EPS = 1e-6
NEG_INF = -1e30

kernel_name = 'hymba_swa_conformer_s5_hmoe_step'


def rmsnorm(x, g):
    xf = x.astype(jnp.float32)
    y = xf * lax.rsqrt(jnp.mean(xf * xf, axis=-1, keepdims=True) + EPS)
    return (y * g.astype(jnp.float32)).astype(x.dtype)


def layernorm(x, g, b):
    xf = x.astype(jnp.float32)
    mu = jnp.mean(xf, axis=-1, keepdims=True)
    var = jnp.mean(jnp.square(xf - mu), axis=-1, keepdims=True)
    y = (xf - mu) * lax.rsqrt(var + EPS)
    return (y * g.astype(jnp.float32) + b.astype(jnp.float32)).astype(x.dtype)


def sink_attention(q, k, v, mask, sinks):
    s = jnp.einsum('...qkgd,...skd->...kgqs', q, k).astype(jnp.float32) * (HEAD_DIM ** -0.5)
    s = jnp.where(mask, s, NEG_INF)
    sink = jnp.broadcast_to(sinks.astype(jnp.float32)[:, :, None, None], s.shape[:-1] + (1,))
    p = jax.nn.softmax(jnp.concatenate([s, sink], axis=-1), axis=-1)[..., :-1]
    return jnp.einsum('...kgqs,...skd->...qkgd', p.astype(v.dtype), v)


def swa_banded(q, k, v, sinks):
    n, t = q.shape[:2]
    nb = t // ATTN_BLOCK
    qb = q.reshape(n, nb, ATTN_BLOCK, N_KV_HEADS, Q_PER_KV, HEAD_DIM)

    def band(z):
        zb = z.reshape(n, nb, ATTN_BLOCK, N_KV_HEADS, HEAD_DIM)
        prev = jnp.concatenate([jnp.zeros_like(zb[:, :1]), zb[:, :-1]], axis=1)
        return jnp.concatenate([prev, zb], axis=2)

    a = jnp.arange(ATTN_BLOCK)[:, None]
    c = jnp.arange(2 * ATTN_BLOCK)[None, :]
    diff = a + ATTN_BLOCK - c
    kpos = jnp.arange(nb)[:, None, None] * ATTN_BLOCK - ATTN_BLOCK + c[None]
    mask = (diff >= 0) & (diff < WINDOW) & (kpos >= 0)
    o = sink_attention(qb, band(k), band(v), mask[:, None, None], sinks)
    return o.reshape(n, t, D_ATTN)


def swa_step(q, k, v, k_buf, v_buf, sinks):
    n, s = q.shape[:2]
    w = k_buf.shape[1]
    kk = jnp.concatenate([k_buf.astype(k.dtype), k], axis=1)
    vv = jnp.concatenate([v_buf.astype(v.dtype), v], axis=1)
    diff = (w + jnp.arange(s))[:, None] - jnp.arange(w + s)[None, :]
    mask = (diff >= 0) & (diff < WINDOW)
    o = sink_attention(q, kk, vv, mask, sinks)
    return o.reshape(n, s, D_ATTN), kk[:, -w:], vv[:, -w:]


def conv_module(a, g, buf, w, b, ln_g, ln_b):
    u = a * jax.nn.sigmoid(g)
    u_ext = jnp.concatenate([buf.astype(u.dtype), u], axis=1)
    y = lax.conv_general_dilated(u_ext, w[:, None, :].astype(u.dtype), window_strides=(1,), padding='VALID',
                                 dimension_numbers=('NWC', 'WIO', 'NWC'), feature_group_count=D_CONV)
    y = jax.nn.silu(layernorm(y + b.astype(y.dtype), ln_g, ln_b))
    return y, u_ext[:, -(CONV_WIDTH - 1):]


def s5_module(u, h0_re, h0_im, a_re, a_im, log_dt, b_re, b_im, c_re, c_im, d, glu_w, glu_b):
    f32 = jnp.float32
    n, t = u.shape[:2]
    lam = lax.complex(a_re.astype(f32), a_im.astype(f32))
    dt = jnp.exp(log_dt.astype(f32))[:, None]
    lam_bar = jnp.exp(lam * dt)
    b_bar = ((lam_bar - 1.0) / lam)[:, :, None] * lax.complex(b_re.astype(f32), b_im.astype(f32))
    ug = u.astype(f32).reshape(n, t, SSM_GROUPS, SSM_GROUP_CH)
    bu = jnp.einsum('gnc,btgc->btgn', b_bar, ug.astype(jnp.complex64))
    h0 = lax.complex(h0_re.astype(f32), h0_im.astype(f32))
    bu = bu.at[:, 0].add(lam_bar * h0)
    a = jnp.broadcast_to(lam_bar, bu.shape)

    def combine(left, right):
        a_l, b_l = left
        a_r, b_r = right
        return a_l * a_r, a_r * b_l + b_r

    _, h = lax.associative_scan(combine, (a, bu), axis=1)
    c = lax.complex(c_re.astype(f32), c_im.astype(f32))
    y = jnp.real(jnp.einsum('gcn,btgn->btgc', c, h)) + d.astype(f32).reshape(SSM_GROUPS, SSM_GROUP_CH) * ug
    z = jax.nn.gelu(y.reshape(n, t, D_SSM))
    out = z * jax.nn.sigmoid(z @ glu_w.astype(f32) + glu_b.astype(f32))
    h_last = h[:, -1]
    return out.astype(u.dtype), jnp.real(h_last), jnp.imag(h_last)


def group_rmsnorm_merge(parts, g):
    outs = []
    off = 0
    for p in parts:
        wdt = p.shape[-1]
        outs.append(rmsnorm(p, g[off:off + wdt]))
        off += wdt
    return jnp.concatenate(outs, axis=-1)


def hier_moe(x, w_gr, b_gr, w_er, b_er, w_gate, w_up, w_down):
    f32 = jnp.float32
    shp = x.shape
    xt = x.reshape(-1, D_MODEL)
    xf = xt.astype(f32)
    g_logits = xf @ w_gr.astype(f32) + b_gr.astype(f32)
    g_idx = jnp.argmax(g_logits, axis=-1)
    g_w = jnp.take_along_axis(jax.nn.softmax(g_logits, axis=-1), g_idx[:, None], axis=-1)
    e_logits = (xf @ w_er.astype(f32) + b_er.astype(f32)).reshape(-1, N_EXPERT_GROUPS, EXPERTS_PER_GROUP)
    e_logits = jnp.take_along_axis(e_logits, g_idx[:, None, None], axis=1)[:, 0]
    top_v, top_i = lax.top_k(e_logits, TOP_K_INNER)
    gates = jax.nn.softmax(top_v, axis=-1) * g_w
    expert = g_idx[:, None] * EXPERTS_PER_GROUP + top_i
    comb = jnp.sum(jax.nn.one_hot(expert, N_EXPERTS, dtype=f32) * gates[..., None], axis=1)
    y = jnp.zeros(xt.shape, f32)
    for e in range(N_EXPERTS):
        h = jax.nn.silu(xt @ w_gate[e]) * (xt @ w_up[e])
        y = y + comb[:, e:e + 1] * (h @ w_down[e]).astype(f32)
    return y.astype(x.dtype).reshape(shp)


def run_trunk(x, win, k_cache, v_cache, conv_state, ssm_re0, ssm_im0, prm):
    is_sample = k_cache is not None
    n, t = x.shape[:2]
    new_k, new_v, new_conv, new_re, new_im = [], [], [], [], []
    for l in range(DEPTH):
        hn = rmsnorm(x, prm['attn_norm_g'][l])
        z = hn @ prm['w_in'][l]
        q, k, v, c_a, c_g, s_u = jnp.split(z, IN_SPLITS, axis=-1)
        q = q.reshape(n, t, N_KV_HEADS, Q_PER_KV, HEAD_DIM)
        k = k.reshape(n, t, N_KV_HEADS, HEAD_DIM)
        v = v.reshape(n, t, N_KV_HEADS, HEAD_DIM)
        sinks = prm['attn_sinks'][l].reshape(N_KV_HEADS, Q_PER_KV)
        if is_sample:
            o_attn, kb, vb = swa_step(q, k, v, k_cache[l], v_cache[l], sinks)
            cbuf = conv_state[l]
            h0r, h0i = ssm_re0[l], ssm_im0[l]
        else:
            o_attn = swa_banded(q, k, v, sinks)
            kb, vb = k[:, -win:], v[:, -win:]
            cbuf = jnp.zeros((n, CONV_WIDTH - 1, D_CONV), x.dtype)
            h0r = jnp.zeros((n, SSM_GROUPS, SSM_STATE), jnp.float32)
            h0i = jnp.zeros((n, SSM_GROUPS, SSM_STATE), jnp.float32)
        o_conv, cb = conv_module(c_a, c_g, cbuf, prm['conv_w'][l], prm['conv_b'][l],
                                 prm['conv_ln_g'][l], prm['conv_ln_b'][l])
        o_ssm, hr, hi = s5_module(s_u, h0r, h0i, prm['ssm_a_re'][l], prm['ssm_a_im'][l], prm['ssm_log_dt'][l],
                                  prm['ssm_b_re'][l], prm['ssm_b_im'][l], prm['ssm_c_re'][l], prm['ssm_c_im'][l],
                                  prm['ssm_d'][l], prm['ssm_glu_w'][l], prm['ssm_glu_b'][l])
        mix = group_rmsnorm_merge([o_attn, o_conv, o_ssm], prm['grp_norm_g'][l])
        x = x + mix @ prm['w_out'][l]
        x = x + hier_moe(rmsnorm(x, prm['ffn_norm_g'][l]), prm['w_group_router'][l], prm['b_group_router'][l],
                         prm['w_expert_router'][l], prm['b_expert_router'][l],
                         prm['w_gate'][l], prm['w_up'][l], prm['w_down'][l])
        new_k.append(kb)
        new_v.append(vb)
        new_conv.append(cb)
        new_re.append(hr)
        new_im.append(hi)
    y = rmsnorm(x, prm['final_norm_g'])
    return y, jnp.stack(new_k), jnp.stack(new_v), jnp.stack(new_conv), jnp.stack(new_re), jnp.stack(new_im)


def setup_inputs(seed: int = 0) -> dict:
    key = jax.random.key(seed)
    ks = jax.random.split(key, 40)
    f32 = jnp.float32
    win = min(WINDOW, PAST_LEN)

    def nrm(k, shape, scale):
        return jax.random.normal(k, shape, f32) * scale

    n_idx = jnp.arange(SSM_STATE, dtype=f32)
    return {
        'x_prompt': nrm(ks[0], (BATCH, SEQ, D_MODEL), 1.0),
        'x_sample': nrm(ks[1], (DEC_BATCH, DEC_SEQ, D_MODEL), 1.0),
        'cache_k': nrm(ks[2], (DEPTH, DEC_BATCH, win, N_KV_HEADS, HEAD_DIM), 1.0),
        'cache_v': nrm(ks[3], (DEPTH, DEC_BATCH, win, N_KV_HEADS, HEAD_DIM), 1.0),
        'state_conv': nrm(ks[4], (DEPTH, DEC_BATCH, CONV_WIDTH - 1, D_CONV), 0.5),
        'state_ssm_re': nrm(ks[5], (DEPTH, DEC_BATCH, SSM_GROUPS, SSM_STATE), 1.0),
        'state_ssm_im': nrm(ks[6], (DEPTH, DEC_BATCH, SSM_GROUPS, SSM_STATE), 1.0),
        'attn_norm_g': 1.0 + nrm(ks[7], (DEPTH, D_MODEL), 0.02),
        'w_in': nrm(ks[8], (DEPTH, D_MODEL, D_IN), D_MODEL ** -0.5),
        'attn_sinks': nrm(ks[9], (DEPTH, N_HEADS), 0.5),
        'conv_w': nrm(ks[10], (DEPTH, CONV_WIDTH, D_CONV), CONV_WIDTH ** -0.5),
        'conv_b': nrm(ks[11], (DEPTH, D_CONV), 0.02),
        'conv_ln_g': 1.0 + nrm(ks[12], (DEPTH, D_CONV), 0.02),
        'conv_ln_b': nrm(ks[13], (DEPTH, D_CONV), 0.02),
        'ssm_a_re': -0.5 * jnp.exp(nrm(ks[14], (DEPTH, SSM_GROUPS, SSM_STATE), 0.01)),
        'ssm_a_im': math.pi * n_idx + nrm(ks[15], (DEPTH, SSM_GROUPS, SSM_STATE), 0.01),
        'ssm_log_dt': jax.random.uniform(ks[16], (DEPTH, SSM_GROUPS), f32, math.log(DT_MIN), math.log(DT_MAX)),
        'ssm_b_re': nrm(ks[17], (DEPTH, SSM_GROUPS, SSM_STATE, SSM_GROUP_CH), (2 * SSM_GROUP_CH) ** -0.5),
        'ssm_b_im': nrm(ks[18], (DEPTH, SSM_GROUPS, SSM_STATE, SSM_GROUP_CH), (2 * SSM_GROUP_CH) ** -0.5),
        'ssm_c_re': nrm(ks[19], (DEPTH, SSM_GROUPS, SSM_GROUP_CH, SSM_STATE), (2 * SSM_STATE) ** -0.5),
        'ssm_c_im': nrm(ks[20], (DEPTH, SSM_GROUPS, SSM_GROUP_CH, SSM_STATE), (2 * SSM_STATE) ** -0.5),
        'ssm_d': nrm(ks[21], (DEPTH, D_SSM), 0.5),
        'ssm_glu_w': nrm(ks[22], (DEPTH, D_SSM, D_SSM), D_SSM ** -0.5),
        'ssm_glu_b': nrm(ks[23], (DEPTH, D_SSM), 0.02),
        'grp_norm_g': 1.0 + nrm(ks[24], (DEPTH, D_MIX), 0.02),
        'w_out': nrm(ks[25], (DEPTH, D_MIX, D_MODEL), D_MIX ** -0.5),
        'ffn_norm_g': 1.0 + nrm(ks[26], (DEPTH, D_MODEL), 0.02),
        'w_group_router': nrm(ks[27], (DEPTH, D_MODEL, N_EXPERT_GROUPS), D_MODEL ** -0.5),
        'b_group_router': nrm(ks[28], (DEPTH, N_EXPERT_GROUPS), 0.01),
        'w_expert_router': nrm(ks[29], (DEPTH, D_MODEL, N_EXPERTS), D_MODEL ** -0.5),
        'b_expert_router': nrm(ks[30], (DEPTH, N_EXPERTS), 0.01),
        'w_gate': nrm(ks[31], (DEPTH, N_EXPERTS, D_MODEL, D_EXPERT), D_MODEL ** -0.5),
        'w_up': nrm(ks[32], (DEPTH, N_EXPERTS, D_MODEL, D_EXPERT), D_MODEL ** -0.5),
        'w_down': nrm(ks[33], (DEPTH, N_EXPERTS, D_EXPERT, D_MODEL), D_EXPERT ** -0.5),
        'final_norm_g': 1.0 + nrm(ks[34], (D_MODEL,), 0.02),
    }


def reference(x_prompt, x_sample, cache_k, cache_v, state_conv, state_ssm_re, state_ssm_im,
              attn_norm_g, w_in, attn_sinks, conv_w, conv_b, conv_ln_g, conv_ln_b,
              ssm_a_re, ssm_a_im, ssm_log_dt, ssm_b_re, ssm_b_im, ssm_c_re, ssm_c_im, ssm_d,
              ssm_glu_w, ssm_glu_b, grp_norm_g, w_out, ffn_norm_g, w_group_router, b_group_router,
              w_expert_router, b_expert_router, w_gate, w_up, w_down, final_norm_g):
    prm = {
        'attn_norm_g': attn_norm_g, 'w_in': w_in, 'attn_sinks': attn_sinks,
        'conv_w': conv_w, 'conv_b': conv_b, 'conv_ln_g': conv_ln_g, 'conv_ln_b': conv_ln_b,
        'ssm_a_re': ssm_a_re, 'ssm_a_im': ssm_a_im, 'ssm_log_dt': ssm_log_dt,
        'ssm_b_re': ssm_b_re, 'ssm_b_im': ssm_b_im, 'ssm_c_re': ssm_c_re, 'ssm_c_im': ssm_c_im,
        'ssm_d': ssm_d, 'ssm_glu_w': ssm_glu_w, 'ssm_glu_b': ssm_glu_b,
        'grp_norm_g': grp_norm_g, 'w_out': w_out, 'ffn_norm_g': ffn_norm_g,
        'w_group_router': w_group_router, 'b_group_router': b_group_router,
        'w_expert_router': w_expert_router, 'b_expert_router': b_expert_router,
        'w_gate': w_gate, 'w_up': w_up, 'w_down': w_down, 'final_norm_g': final_norm_g,
    }
    win = cache_k.shape[2]
    y_prompt, k_p, v_p, conv_p, re_p, im_p = run_trunk(x_prompt, win, None, None, None, None, None, prm)
    y_sample, k_s, v_s, conv_s, re_s, im_s = run_trunk(x_sample, win, cache_k, cache_v, state_conv,
                                                       state_ssm_re, state_ssm_im, prm)
    return (y_prompt, y_sample, k_p, v_p, conv_p, re_p, im_p, k_s, v_s, conv_s, re_s, im_s)
```

```python
import functools

import jax
import jax.numpy as jnp
from jax import lax
from jax.experimental import pallas as pl
from jax.experimental.pallas import tpu as pltpu

F32 = jnp.float32
BF16 = jnp.bfloat16
U32 = jnp.uint32
I32 = jnp.int32

D_MODEL = 1024
N_HEADS = 8
N_KV_HEADS = 2
HEAD_DIM = 64
WINDOW = 128
D_ATTN = N_HEADS * HEAD_DIM
D_KV = N_KV_HEADS * HEAD_DIM
D_CONV = 256
CONV_WIDTH = 31
D_SSM = 256
SSM_GROUPS = 16
SSM_GROUP_CH = 16
SSM_STATE = 64
N_STATE = SSM_GROUPS * SSM_STATE
D_IN = D_ATTN + 2 * D_KV + 2 * D_CONV + D_SSM
N_EXPERT_GROUPS = 4
EXPERTS_PER_GROUP = 8
N_EXPERTS = N_EXPERT_GROUPS * EXPERTS_PER_GROUP
D_EXPERT = 512
EPS = 1e-6
NEG_INF = -1e30
SCALE = HEAD_DIM ** -0.5

LANES = 128
SUBLANES = 8
HALF = LANES // 2
D_PACK = D_MODEL // 2

TM_IN = 512
TM_OUT = 256
TM_MOE = 256
CONV_T = 256
CONV_CHUNK = 64
CONV_HALO = 32
SCAN_T = 256
SCAN_PITCH = SCAN_T + SUBLANES
SAMPLE_BT = 16
VMEM_LIMIT = 48 * 1024 * 1024

ROUTE_E1, ROUTE_E2, ROUTE_W1, ROUTE_W2, ROUTE_R1, ROUTE_R2 = range(6)
ROUTER_LANE0 = N_EXPERT_GROUPS


def _params(sem, vmem=VMEM_LIMIT):
    return pltpu.CompilerParams(dimension_semantics=sem, vmem_limit_bytes=vmem)


def _full(shape):
    zeros = (0,) * len(shape)
    return pl.BlockSpec(shape, lambda *_: zeros)


def _swap_halves(x):
    return jnp.concatenate([x[:, HALF:], x[:, :HALF]], axis=1)


def _rms(x, g):
    return x * lax.rsqrt(jnp.mean(x * x, axis=-1, keepdims=True) + EPS) * g


def _pack_pair(a, b):
    ha = lax.bitcast_convert_type(a.astype(BF16).astype(F32), U32)
    hb = lax.bitcast_convert_type(b.astype(BF16).astype(F32), U32)
    return ha | (hb >> 16)


def _unpack_pair(p):
    a = lax.bitcast_convert_type(p & jnp.uint32(0xFFFF0000), F32)
    b = lax.bitcast_convert_type(p << 16, F32)
    return a, b


def _in_proj_kernel(x_ref, g_ref, w_ref, q_ref, kv_ref, u_ref, su_ref):
    hn = _rms(x_ref[...], g_ref[...]).astype(BF16)
    z = jnp.dot(hn, w_ref[...], preferred_element_type=F32)
    q_ref[...] = z[:, :D_ATTN].astype(BF16)
    kv_ref[...] = z[:, D_ATTN:D_ATTN + 2 * D_KV].astype(BF16)
    c0 = D_ATTN + 2 * D_KV
    u_ref[...] = z[:, c0:c0 + D_CONV] * jax.nn.sigmoid(z[:, c0 + D_CONV:c0 + 2 * D_CONV])
    su_ref[...] = z[:, c0 + 2 * D_CONV:].astype(BF16)


def _in_proj(x, g, w_bf):
    n = x.shape[0]
    assert n % TM_IN == 0
    row = lambda width: pl.BlockSpec((TM_IN, width), lambda i: (i, 0))
    return pl.pallas_call(
        _in_proj_kernel,
        grid=(n // TM_IN,),
        in_specs=[row(D_MODEL), _full((1, D_MODEL)), _full((D_MODEL, D_IN))],
        out_specs=[row(D_ATTN), row(2 * D_KV), row(D_CONV), row(D_SSM)],
        out_shape=[jax.ShapeDtypeStruct((n, D_ATTN), BF16), jax.ShapeDtypeStruct((n, 2 * D_KV), BF16),
                   jax.ShapeDtypeStruct((n, D_CONV), F32), jax.ShapeDtypeStruct((n, D_SSM), BF16)],
        compiler_params=_params(("arbitrary",)),
        name="in_proj",
    )(x, g.reshape(1, D_MODEL), w_bf)


def _softmax_pv(s, mask, sink, vmat):
    s = jnp.where(mask, s * SCALE, NEG_INF)
    m = jnp.maximum(jnp.max(s, axis=-1, keepdims=True), sink)
    p = jnp.exp(s - m)
    denom = jnp.sum(p, axis=-1, keepdims=True) + jnp.exp(sink - m)
    return jnp.dot(p.astype(BF16), vmat, preferred_element_type=F32) / denom


def _attn_prompt_kernel(sink_ref, q_ref, kvc_ref, kvp_ref, o_ref):
    i = pl.program_id(1)
    q = q_ref[...]
    kvc = kvc_ref[...]
    kvp = kvp_ref[...]
    kk = jnp.concatenate([kvp[:, :LANES], kvc[:, :LANES]], axis=0)
    vv = jnp.concatenate([kvp[:, LANES:], kvc[:, LANES:]], axis=0)
    kk_sw = _swap_halves(kk)
    vv_sw = _swap_halves(vv)
    lo = lax.broadcasted_iota(I32, (1, LANES), 1) < HALF
    a = lax.broadcasted_iota(I32, (WINDOW, 2 * WINDOW), 0)
    c = lax.broadcasted_iota(I32, (WINDOW, 2 * WINDOW), 1)
    diff = a + WINDOW - c
    mask = (diff >= 0) & (diff < WINDOW) & ((c >= WINDOW) | (i > 0))
    zero = jnp.zeros((WINDOW, LANES), BF16)
    for j in range(D_ATTN // LANES):
        kvh = (2 * j) // (N_HEADS // N_KV_HEADS)
        qt = q[:, LANES * j:LANES * (j + 1)]
        mats = ((kk, vv), (kk_sw, vv_sw)) if kvh == 0 else ((kk_sw, vv_sw), (kk, vv))
        outs = []
        for par in range(2):
            kmat, vmat = mats[par]
            qm = jnp.where(lo if par == 0 else jnp.logical_not(lo), qt, zero)
            s = lax.dot_general(qm, kmat, (((1,), (1,)), ((), ())), preferred_element_type=F32)
            outs.append(_softmax_pv(s, mask, sink_ref[2 * j + par], vmat))
        o_ref[:, LANES * j:LANES * (j + 1)] = jnp.where(lo, outs[0], outs[1]).astype(BF16)


def _attn_prompt(q, kv, sinks, n_seq, t):
    nb = t // WINDOW
    blk = lambda width, f: pl.BlockSpec((WINDOW, width), f)
    return pl.pallas_call(
        _attn_prompt_kernel,
        grid=(n_seq, nb),
        in_specs=[pl.BlockSpec(memory_space=pltpu.SMEM),
                  blk(D_ATTN, lambda b, i: (b * nb + i, 0)),
                  blk(2 * D_KV, lambda b, i: (b * nb + i, 0)),
                  blk(2 * D_KV, lambda b, i: (b * nb + jnp.maximum(i - 1, 0), 0))],
        out_specs=blk(D_ATTN, lambda b, i: (b * nb + i, 0)),
        out_shape=jax.ShapeDtypeStruct((n_seq * t, D_ATTN), BF16),
        compiler_params=_params(("arbitrary", "arbitrary")),
        name="attn_prompt",
    )(sinks, q, kv, kv)


def _attn_sample_kernel(sink_ref, q_ref, kv_ref, ck_ref, cv_ref, o_ref, qf_ref, kvf_ref, *, s_new):
    rows = SUBLANES
    n_pair = q_ref.shape[0] // rows
    per = rows // s_new
    qf_ref[...] = q_ref[...].astype(F32)
    kvf_ref[...] = kv_ref[...].astype(F32)
    lane = lax.broadcasted_iota(I32, (1, LANES), 1)
    lo = lane < HALF
    rid = lax.broadcasted_iota(I32, (N_HEADS * rows, 1), 0)
    head = rid // rows
    seq = (rid % rows) // s_new
    tok = rid % s_new
    sink = jnp.zeros((N_HEADS * rows, 1), F32)
    for h in range(N_HEADS):
        sink = jnp.where(head == h, sink_ref[h], sink)
    mask_c = lane > tok

    def pair(p, carry):
        r0 = pl.multiple_of(p * rows, rows)
        q8 = qf_ref[pl.ds(r0, rows), :]
        kv8 = kvf_ref[pl.ds(r0, rows), :]
        knew = kv8[:, :LANES]
        vnew = kv8[:, LANES:]
        pieces = []
        for h in range(N_HEADS):
            qt = q8[:, LANES * (h // 2):LANES * (h // 2 + 1)]
            tgt = h // (N_HEADS // N_KV_HEADS)
            if h % 2 != tgt:
                qt = _swap_halves(qt)
            pieces.append(jnp.where(lo if tgt == 0 else jnp.logical_not(lo), qt, 0.0))
        qm = jnp.concatenate(pieces, axis=0)
        qb = qm.astype(BF16)
        s_c = jnp.zeros((N_HEADS * rows, LANES), F32)
        for bb in range(per):
            kc = ck_ref[p * per + bb].astype(BF16)
            s_bb = lax.dot_general(qb, kc, (((1,), (1,)), ((), ())), preferred_element_type=F32)
            s_c = jnp.where(seq == bb, s_bb, s_c)
        s_c = jnp.where(mask_c, s_c * SCALE, NEG_INF)
        m = jnp.maximum(jnp.max(s_c, axis=-1, keepdims=True), sink)
        s_n = []
        for k in range(rows):
            valid = (seq == k // s_new) & (tok >= k % s_new)
            sk = jnp.sum(qm * knew[k:k + 1, :], axis=-1, keepdims=True) * SCALE
            sk = jnp.where(valid, sk, NEG_INF)
            s_n.append(sk)
            m = jnp.maximum(m, sk)
        p_c = jnp.exp(s_c - m)
        denom = jnp.sum(p_c, axis=-1, keepdims=True) + jnp.exp(sink - m)
        pb = p_c.astype(BF16)
        o = jnp.zeros((N_HEADS * rows, LANES), F32)
        for bb in range(per):
            vc = cv_ref[p * per + bb].astype(BF16)
            o = jnp.where(seq == bb, jnp.dot(pb, vc, preferred_element_type=F32), o)
        for k in range(rows):
            pk = jnp.exp(s_n[k] - m)
            denom = denom + pk
            o = o + pk.astype(BF16).astype(F32) * vnew[k:k + 1, :]
        o = o / denom
        for j in range(D_ATTN // LANES):
            kvh = (2 * j) // (N_HEADS // N_KV_HEADS)
            pe = o[rows * 2 * j:rows * (2 * j + 1), :]
            po = o[rows * (2 * j + 1):rows * (2 * j + 2), :]
            if kvh == 0:
                po = _swap_halves(po)
            else:
                pe = _swap_halves(pe)
            o_ref[pl.ds(r0, rows), LANES * j:LANES * (j + 1)] = jnp.where(lo, pe, po)
        return carry

    lax.fori_loop(0, n_pair, pair, 0)


def _attn_sample(q, kv, cache_k, cache_v, sinks, s_new, row0):
    n_seq, win, _ = cache_k.shape
    n_rows = n_seq * s_new
    rows_blk = SAMPLE_BT * s_new
    assert win == WINDOW and SUBLANES % s_new == 0 and n_seq % SAMPLE_BT == 0 and row0 % rows_blk == 0
    blk0 = row0 // rows_blk
    return pl.pallas_call(
        functools.partial(_attn_sample_kernel, s_new=s_new),
        grid=(n_seq // SAMPLE_BT,),
        in_specs=[pl.BlockSpec(memory_space=pltpu.SMEM),
                  pl.BlockSpec((rows_blk, D_ATTN), lambda i: (blk0 + i, 0)),
                  pl.BlockSpec((rows_blk, 2 * D_KV), lambda i: (blk0 + i, 0)),
                  pl.BlockSpec((SAMPLE_BT, win, D_KV), lambda i: (i, 0, 0)),
                  pl.BlockSpec((SAMPLE_BT, win, D_KV), lambda i: (i, 0, 0))],
        out_specs=pl.BlockSpec((rows_blk, D_ATTN), lambda i: (i, 0)),
        out_shape=jax.ShapeDtypeStruct((n_rows, D_ATTN), F32),
        scratch_shapes=[pltpu.VMEM((rows_blk, D_ATTN), F32), pltpu.VMEM((rows_blk, 2 * D_KV), F32)],
        compiler_params=_params(("arbitrary",)),
        name="attn_sample",
    )(sinks, q, kv, cache_k, cache_v)


def _ln_silu(y, lg, lb):
    mu = jnp.mean(y, axis=-1, keepdims=True)
    var = jnp.mean(jnp.square(y - mu), axis=-1, keepdims=True)
    yn = (y - mu) * lax.rsqrt(var + EPS) * lg + lb
    return yn * jax.nn.sigmoid(yn)


def _conv_prompt_kernel(u_ref, w_ref, b_ref, lg_ref, lb_ref, o_ref, ext_ref):
    i = pl.program_id(1)

    @pl.when(i == 0)
    def _():
        ext_ref[0:CONV_HALO, :] = jnp.zeros((CONV_HALO, D_CONV), F32)

    @pl.when(i > 0)
    def _():
        ext_ref[0:CONV_HALO, :] = ext_ref[CONV_T:CONV_T + CONV_HALO, :]

    ext_ref[CONV_HALO:CONV_HALO + CONV_T, :] = u_ref[...]
    shift = CONV_HALO - (CONV_WIDTH - 1)
    for cidx in range(CONV_T // CONV_CHUNK):
        r0 = cidx * CONV_CHUNK
        acc = jnp.zeros((CONV_CHUNK, D_CONV), F32)
        for j in range(CONV_WIDTH):
            acc = acc + w_ref[j:j + 1, :] * ext_ref[r0 + j + shift:r0 + j + shift + CONV_CHUNK, :]
        o_ref[r0:r0 + CONV_CHUNK, :] = _ln_silu(acc + b_ref[...], lg_ref[...], lb_ref[...]).astype(BF16)


def _conv_prompt(u, w, b, lg, lb, n_seq, t):
    nt = t // CONV_T
    vec = _full((1, D_CONV))
    return pl.pallas_call(
        _conv_prompt_kernel,
        grid=(n_seq, nt),
        in_specs=[pl.BlockSpec((CONV_T, D_CONV), lambda s, i: (s * nt + i, 0)),
                  _full((CONV_WIDTH, D_CONV)), vec, vec, vec],
        out_specs=pl.BlockSpec((CONV_T, D_CONV), lambda s, i: (s * nt + i, 0)),
        out_shape=jax.ShapeDtypeStruct((n_seq * t, D_CONV), BF16),
        scratch_shapes=[pltpu.VMEM((CONV_T + CONV_HALO, D_CONV), F32)],
        compiler_params=_params(("arbitrary", "arbitrary")),
        name="conv_prompt",
    )(u, w, b.reshape(1, D_CONV), lg.reshape(1, D_CONV), lb.reshape(1, D_CONV))


def _conv_sample_kernel(st_ref, u_ref, w_ref, b_ref, lg_ref, lb_ref, o_ref, *, s_new):
    hist = CONV_WIDTH - 1
    for t in range(s_new):
        acc = jnp.zeros((st_ref.shape[0], D_CONV), F32)
        for j in range(CONV_WIDTH):
            idx = t + j
            if idx < hist:
                piece = st_ref[:, idx * D_CONV:(idx + 1) * D_CONV]
            else:
                piece = u_ref[:, (idx - hist) * D_CONV:(idx - hist + 1) * D_CONV]
            acc = acc + w_ref[j:j + 1, :] * piece
        o_ref[:, t * D_CONV:(t + 1) * D_CONV] = _ln_silu(acc + b_ref[...], lg_ref[...], lb_ref[...])


def _conv_sample(state2d, u2d, w, b, lg, lb, s_new):
    n_seq = u2d.shape[0]
    return pl.pallas_call(
        functools.partial(_conv_sample_kernel, s_new=s_new),
        out_shape=jax.ShapeDtypeStruct((n_seq, s_new * D_CONV), F32),
        compiler_params=pltpu.CompilerParams(vmem_limit_bytes=VMEM_LIMIT),
        name="conv_sample",
    )(state2d, u2d, w, b.reshape(1, D_CONV), lg.reshape(1, D_CONV), lb.reshape(1, D_CONV))


def _s5_discretize(a_re, a_im, log_dt):
    dt = jnp.exp(log_dt)
    mag = jnp.exp(a_re * dt)
    ang = a_im * dt
    lr = mag * jnp.cos(ang)
    li = mag * jnp.sin(ang)
    den = a_re * a_re + a_im * a_im
    cr = ((lr - 1.0) * a_re + li * a_im) / den
    ci = (li * a_re - (lr - 1.0) * a_im) / den
    return lr, li, cr, ci


def _s5_bbar(arow_ref, bre_ref, bim_ref):
    _, _, cr, ci = _s5_discretize(arow_ref[0:1, :], arow_ref[1:2, :], arow_ref[2:3, :])
    bre = bre_ref[...]
    bim = bim_ref[...]
    return (cr * bre - ci * bim).astype(BF16), (cr * bim + ci * bre).astype(BF16)


def _s5_readout(h_re, h_im, u, cre_ref, cim_ref, d_ref, gw_ref, gb_ref):
    y = (jnp.dot(h_re.astype(BF16), cre_ref[...], preferred_element_type=F32)
         - jnp.dot(h_im.astype(BF16), cim_ref[...], preferred_element_type=F32)
         + d_ref[...] * u.astype(F32))
    z = jax.nn.gelu(y)
    gate = jnp.dot(z.astype(BF16), gw_ref[...], preferred_element_type=F32) + gb_ref[...]
    return (z * jax.nn.sigmoid(gate)).astype(BF16)


def _s5_prompt_kernel(su0_ref, su1_ref, arow_ref, atile_ref, bre_ref, bim_ref, cre_ref, cim_ref, d_ref,
                      gw_ref, gb_ref, o0_ref, o1_ref, hre_ref, him_ref,
                      bbr_ref, bbi_ref, lam_ref, car_ref, bur_ref, bui_ref, hbr_ref, hbi_ref):
    i = pl.program_id(0)
    n_slab = N_STATE // LANES
    su_refs = (su0_ref, su1_ref)
    o_refs = (o0_ref, o1_ref)

    @pl.when(i == 0)
    def _():
        bbr, bbi = _s5_bbar(arow_ref, bre_ref, bim_ref)
        bbr_ref[...] = bbr
        bbi_ref[...] = bbi
        lr, li, _, _ = _s5_discretize(atile_ref[0], atile_ref[1], atile_ref[2])
        lam_ref[0] = lr
        lam_ref[1] = li
        car_ref[...] = jnp.zeros(car_ref.shape, F32)

    for s in range(2):
        u = su_refs[s][...]
        br = jnp.dot(u, bbr_ref[...], preferred_element_type=F32)
        bi = jnp.dot(u, bbi_ref[...], preferred_element_type=F32)
        for j in range(n_slab):
            bur_ref[s, j * SCAN_PITCH:j * SCAN_PITCH + SCAN_T, :] = br[:, LANES * j:LANES * (j + 1)]
            bui_ref[s, j * SCAN_PITCH:j * SCAN_PITCH + SCAN_T, :] = bi[:, LANES * j:LANES * (j + 1)]

    lr = lam_ref[0]
    li = lam_ref[1]

    def step(t, carry):
        new = []
        for s in range(2):
            hr, hi = carry[2 * s], carry[2 * s + 1]
            rows = pl.ds(t, n_slab, stride=SCAN_PITCH)
            nr = lr * hr - li * hi + bur_ref.at[s][rows, :]
            ni = lr * hi + li * hr + bui_ref.at[s][rows, :]
            hbr_ref.at[s][rows, :] = nr
            hbi_ref.at[s][rows, :] = ni
            new += [nr, ni]
        return tuple(new)

    carry = lax.fori_loop(0, SCAN_T, step, tuple(car_ref[k] for k in range(4)), unroll=8)
    for k in range(4):
        car_ref[k] = carry[k]

    @pl.when(i == pl.num_programs(0) - 1)
    def _():
        for s in range(2):
            hre_ref[s] = carry[2 * s]
            him_ref[s] = carry[2 * s + 1]

    for s in range(2):
        h_re = jnp.concatenate([hbr_ref[s, j * SCAN_PITCH:j * SCAN_PITCH + SCAN_T, :] for j in range(n_slab)], axis=1)
        h_im = jnp.concatenate([hbi_ref[s, j * SCAN_PITCH:j * SCAN_PITCH + SCAN_T, :] for j in range(n_slab)], axis=1)
        o_refs[s][...] = _s5_readout(h_re, h_im, su_refs[s][...], cre_ref, cim_ref, d_ref, gw_ref, gb_ref)


def _s5_prompt(su, prm, t):
    nt = t // SCAN_T
    n_slab = N_STATE // LANES
    blk0 = pl.BlockSpec((SCAN_T, D_SSM), lambda i: (i, 0))
    blk1 = pl.BlockSpec((SCAN_T, D_SSM), lambda i: (nt + i, 0))
    oblk = pl.BlockSpec((SCAN_T, D_SSM), lambda i: (i, 0))
    state = pl.BlockSpec((2, SUBLANES, LANES), lambda i: (0, 0, 0))
    slabs = pltpu.VMEM((2, n_slab * SCAN_PITCH, LANES), F32)
    return pl.pallas_call(
        _s5_prompt_kernel,
        grid=(nt,),
        in_specs=[blk0, blk1, _full((3, N_STATE)), _full((3, SUBLANES, LANES)),
                  _full((D_SSM, N_STATE)), _full((D_SSM, N_STATE)), _full((N_STATE, D_SSM)), _full((N_STATE, D_SSM)),
                  _full((1, D_SSM)), _full((D_SSM, D_SSM)), _full((1, D_SSM))],
        out_specs=[oblk, oblk, state, state],
        out_shape=[jax.ShapeDtypeStruct((t, D_SSM), BF16), jax.ShapeDtypeStruct((t, D_SSM), BF16),
                   jax.ShapeDtypeStruct((2, SUBLANES, LANES), F32), jax.ShapeDtypeStruct((2, SUBLANES, LANES), F32)],
        scratch_shapes=[pltpu.VMEM((D_SSM, N_STATE), BF16), pltpu.VMEM((D_SSM, N_STATE), BF16),
                        pltpu.VMEM((2, SUBLANES, LANES), F32), pltpu.VMEM((4, SUBLANES, LANES), F32),
                        slabs, slabs, slabs, slabs],
        compiler_params=_params(("arbitrary",)),
        name="s5_prompt",
    )(su, su, prm["arow"], prm["atile"], prm["bre"], prm["bim"], prm["cre"], prm["cim"],
      prm["d"], prm["gw"], prm["gb"])


def _s5_sample_kernel(su_ref, h0r_ref, h0i_ref, arow_ref, bre_ref, bim_ref, cre_ref, cim_ref, d_ref, gw_ref, gb_ref,
                      o_ref, hr_ref, hi_ref, hbr_ref, hbi_ref, *, s_new):
    n_seq = h0r_ref.shape[0]
    lr, li, _, _ = _s5_discretize(arow_ref[0:1, :], arow_ref[1:2, :], arow_ref[2:3, :])
    bbr, bbi = _s5_bbar(arow_ref, bre_ref, bim_ref)
    u = su_ref[...]
    hbr_ref[...] = jnp.dot(u, bbr, preferred_element_type=F32)
    hbi_ref[...] = jnp.dot(u, bbi, preferred_element_type=F32)
    hr_ref[...] = h0r_ref[...]
    hi_ref[...] = h0i_ref[...]
    for t in range(s_new):
        rows = slice(t * n_seq, (t + 1) * n_seq)
        hr = hr_ref[...]
        hi = hi_ref[...]
        nr = lr * hr - li * hi + hbr_ref[rows, :]
        ni = lr * hi + li * hr + hbi_ref[rows, :]
        hbr_ref[rows, :] = nr
        hbi_ref[rows, :] = ni
        hr_ref[...] = nr
        hi_ref[...] = ni
    o_ref[...] = _s5_readout(hbr_ref[...], hbi_ref[...], u, cre_ref, cim_ref, d_ref, gw_ref, gb_ref)


def _s5_sample(su_tb, h0r, h0i, prm, s_new):
    n_rows = su_tb.shape[0]
    n_seq = n_rows // s_new
    return pl.pallas_call(
        functools.partial(_s5_sample_kernel, s_new=s_new),
        out_shape=[jax.ShapeDtypeStruct((n_rows, D_SSM), BF16),
                   jax.ShapeDtypeStruct((n_seq, N_STATE), F32), jax.ShapeDtypeStruct((n_seq, N_STATE), F32)],
        scratch_shapes=[pltpu.VMEM((n_rows, N_STATE), F32), pltpu.VMEM((n_rows, N_STATE), F32)],
        compiler_params=pltpu.CompilerParams(vmem_limit_bytes=VMEM_LIMIT),
        name="s5_sample",
    )(su_tb, h0r, h0i, prm["arow"], prm["bre"], prm["bim"], prm["cre"], prm["cim"], prm["d"], prm["gw"], prm["gb"])


def _s5_params(p, l):
    eye = jnp.eye(SSM_GROUPS, dtype=F32)

    def b_diag(b):
        return jnp.einsum("gnc,gh->gchn", b, eye).reshape(D_SSM, N_STATE)

    def c_diag(c):
        return jnp.einsum("gcn,gh->gnhc", c, eye).reshape(N_STATE, D_SSM)

    ldt = jnp.broadcast_to(p["ssm_log_dt"][l][:, None], (SSM_GROUPS, SSM_STATE))
    a3 = jnp.stack([p["ssm_a_re"][l], p["ssm_a_im"][l], ldt])
    return {
        "arow": a3.reshape(3, N_STATE),
        "atile": a3.reshape(3, SUBLANES, LANES),
        "bre": b_diag(p["ssm_b_re"][l]), "bim": b_diag(p["ssm_b_im"][l]),
        "cre": c_diag(p["ssm_c_re"][l]).astype(BF16), "cim": c_diag(p["ssm_c_im"][l]).astype(BF16),
        "d": p["ssm_d"][l].reshape(1, D_SSM),
        "gw": p["ssm_glu_w"][l].astype(BF16), "gb": p["ssm_glu_b"][l].reshape(1, D_SSM),
    }


def _merge_out_kernel(oa_ref, oc_ref, os_ref, x_ref, gn_ref, wo_ref, fg_ref, wrh_ref, wrl_ref, br_ref,
                      x1_ref, xn_ref, route_ref, cnt_ref, carry_ref):
    i = pl.program_id(0)
    tm = x_ref.shape[0]

    @pl.when(i == 0)
    def _():
        carry_ref[...] = jnp.zeros(carry_ref.shape, F32)

    gn = gn_ref[...]
    mix = jnp.concatenate([
        _rms(oa_ref[...].astype(F32), gn[:, :D_ATTN]),
        _rms(oc_ref[...].astype(F32), gn[:, D_ATTN:D_ATTN + D_CONV]),
        _rms(os_ref[...].astype(F32), gn[:, D_ATTN + D_CONV:]),
    ], axis=1).astype(BF16)
    x1 = x_ref[...] + jnp.dot(mix, wo_ref[...], preferred_element_type=F32)
    x1_ref[...] = x1
    xn = _rms(x1, fg_ref[...])
    xn_ref[...] = _pack_pair(xn[:, :D_PACK], xn[:, D_PACK:])

    xh = xn.astype(BF16)
    xl = (xn - xh.astype(F32)).astype(BF16)
    wh = wrh_ref[...]
    logits = (jnp.dot(xh, wh, preferred_element_type=F32) + jnp.dot(xl, wh, preferred_element_type=F32)
              + jnp.dot(xh, wrl_ref[...], preferred_element_type=F32) + br_ref[...])

    lane = lax.broadcasted_iota(I32, (1, LANES), 1).astype(F32)
    far = float(LANES)

    def first_max(v):
        top = jnp.max(v, axis=-1, keepdims=True)
        return top, jnp.min(jnp.where(v == top, lane, far), axis=-1, keepdims=True)

    is_group = lane < N_EXPERT_GROUPS
    g_top, g_idx = first_max(jnp.where(is_group, logits, -jnp.inf))
    g_w = 1.0 / jnp.sum(jnp.where(is_group, jnp.exp(logits - g_top), 0.0), axis=-1, keepdims=True)
    e_lo = ROUTER_LANE0 + EXPERTS_PER_GROUP * g_idx
    el = jnp.where((lane >= e_lo) & (lane < e_lo + EXPERTS_PER_GROUP), logits, -jnp.inf)
    v1, i1 = first_max(el)
    v2, i2 = first_max(jnp.where(lane == i1, -jnp.inf, el))
    t2 = jnp.exp(v2 - v1)
    w1 = g_w / (1.0 + t2)
    w2 = g_w * t2 / (1.0 + t2)

    sel1 = lane == i1
    sel2 = lane == i2
    onehot = jnp.where(sel1 | sel2, 1.0, 0.0)
    r_i = lax.broadcasted_iota(I32, (tm, tm), 0)
    c_i = lax.broadcasted_iota(I32, (tm, tm), 1)
    lower = jnp.where(c_i < r_i, 1.0, 0.0).astype(BF16)
    rank = jnp.dot(lower, onehot.astype(BF16), preferred_element_type=F32) + carry_ref[...]
    rank1 = jnp.sum(jnp.where(sel1, rank, 0.0), axis=-1, keepdims=True)
    rank2 = jnp.sum(jnp.where(sel2, rank, 0.0), axis=-1, keepdims=True)
    carry_ref[...] = carry_ref[...] + jnp.sum(onehot, axis=0, keepdims=True)
    cnt_ref[...] = carry_ref[...]

    route = jnp.zeros((tm, LANES), F32)
    for pos, val in ((ROUTE_E1, i1 - ROUTER_LANE0), (ROUTE_E2, i2 - ROUTER_LANE0), (ROUTE_W1, w1), (ROUTE_W2, w2),
                     (ROUTE_R1, rank1), (ROUTE_R2, rank2)):
        route = jnp.where(lane == pos, val, route)
    route_ref[...] = route


def _merge_out(oa, oc, os_, x, gn, wo_bf, fg, wr_hi, wr_lo, br):
    n = x.shape[0]
    assert n % TM_OUT == 0
    row = lambda width: pl.BlockSpec((TM_OUT, width), lambda i: (i, 0))
    return pl.pallas_call(
        _merge_out_kernel,
        grid=(n // TM_OUT,),
        in_specs=[row(D_ATTN), row(D_CONV), row(D_SSM), row(D_MODEL), _full((1, D_MODEL)),
                  _full((D_MODEL, D_MODEL)), _full((1, D_MODEL)), _full((D_MODEL, LANES)), _full((D_MODEL, LANES)),
                  _full((1, LANES))],
        out_specs=[row(D_MODEL), row(D_PACK), row(LANES), _full((1, LANES))],
        out_shape=[jax.ShapeDtypeStruct((n, D_MODEL), F32), jax.ShapeDtypeStruct((n, D_PACK), U32),
                   jax.ShapeDtypeStruct((n, LANES), F32), jax.ShapeDtypeStruct((1, LANES), F32)],
        scratch_shapes=[pltpu.VMEM((1, LANES), F32)],
        compiler_params=_params(("arbitrary",)),
        name="merge_out",
    )(oa, oc, os_, x, gn.reshape(1, D_MODEL), wo_bf, fg.reshape(1, D_MODEL), wr_hi, wr_lo, br)


def _dispatch_kernel(dest_ref, zl_ref, xn_ref, xs_ref, zbuf_ref, sem_z, sem_r):
    i = pl.program_id(0)
    tm = xn_ref.shape[0]

    def zero_copy(t):
        return pltpu.make_async_copy(zbuf_ref, xs_ref.at[pl.ds(pl.multiple_of(t * TM_MOE, TM_MOE), TM_MOE)], sem_z)

    @pl.when(i == 0)
    def _():
        zbuf_ref[...] = jnp.zeros(zbuf_ref.shape, U32)

        def z_start(k, c):
            t = zl_ref[k]

            @pl.when(t >= 0)
            def _():
                zero_copy(t).start()
            return c

        def z_wait(k, c):
            t = zl_ref[k]

            @pl.when(t >= 0)
            def _():
                zero_copy(t).wait()
            return c

        lax.fori_loop(0, zl_ref.shape[0], z_start, 0)
        lax.fori_loop(0, zl_ref.shape[0], z_wait, 0)

    def row_copy(r, k):
        return pltpu.make_async_copy(xn_ref.at[pl.ds(r, 1)], xs_ref.at[pl.ds(dest_ref[0, 0, 2 * r + k], 1)], sem_r)

    def r_start(r, c):
        row_copy(r, 0).start()
        row_copy(r, 1).start()
        return c

    def r_wait(r, c):
        row_copy(r, 0).wait()
        row_copy(r, 1).wait()
        return c

    lax.fori_loop(0, tm, r_start, 0, unroll=8)
    lax.fori_loop(0, tm, r_wait, 0, unroll=8)


def _dispatch(dest, zero_tiles, xn, n_rows_sorted):
    n = xn.shape[0]
    return pl.pallas_call(
        _dispatch_kernel,
        grid=(n // TM_OUT,),
        in_specs=[pl.BlockSpec((1, 1, 2 * TM_OUT), lambda i: (i, 0, 0), memory_space=pltpu.SMEM),
                  pl.BlockSpec(memory_space=pltpu.SMEM),
                  pl.BlockSpec((TM_OUT, D_PACK), lambda i: (i, 0))],
        out_specs=pl.BlockSpec(memory_space=pl.ANY),
        out_shape=jax.ShapeDtypeStruct((n_rows_sorted, D_PACK), U32),
        scratch_shapes=[pltpu.VMEM((TM_MOE, D_PACK), U32), pltpu.SemaphoreType.DMA(()), pltpu.SemaphoreType.DMA(())],
        compiler_params=_params(("arbitrary",)),
        name="moe_dispatch",
    )(dest, zero_tiles, xn)


def _moe_kernel(te_ref, nu_ref, xs_ref, wg_ref, wu_ref, wd_ref, ys_ref, wgb_ref, wub_ref, wdb_ref):
    i = pl.program_id(0)
    used = i < nu_ref[0]
    new_expert = (i == 0) | (te_ref[i] != te_ref[jnp.maximum(i - 1, 0)])

    @pl.when(used & new_expert)
    def _():
        wgb_ref[...] = wg_ref[0].astype(BF16)
        wub_ref[...] = wu_ref[0].astype(BF16)
        wdb_ref[...] = wd_ref[0].astype(BF16)

    @pl.when(used)
    def _():
        a, b = _unpack_pair(xs_ref[...])
        x = jnp.concatenate([a, b], axis=1).astype(BF16)
        gate = jnp.dot(x, wgb_ref[...], preferred_element_type=F32)
        up = jnp.dot(x, wub_ref[...], preferred_element_type=F32)
        h = (gate * jax.nn.sigmoid(gate) * up).astype(BF16)
        y = jnp.dot(h, wdb_ref[...], preferred_element_type=F32)
        ys_ref[...] = _pack_pair(y[:, :D_PACK], y[:, D_PACK:])

    @pl.when(jnp.logical_not(used))
    def _():
        ys_ref[...] = jnp.zeros(ys_ref.shape, U32)


def _moe(tile_expert, n_used, xs, wg, wu, wd):
    n_tiles = xs.shape[0] // TM_MOE
    grid_spec = pltpu.PrefetchScalarGridSpec(
        num_scalar_prefetch=2,
        grid=(n_tiles,),
        in_specs=[pl.BlockSpec((TM_MOE, D_PACK), lambda i, te, nu: (jnp.minimum(i, nu[0] - 1), 0)),
                  pl.BlockSpec((1, D_MODEL, D_EXPERT), lambda i, te, nu: (te[i], 0, 0)),
                  pl.BlockSpec((1, D_MODEL, D_EXPERT), lambda i, te, nu: (te[i], 0, 0)),
                  pl.BlockSpec((1, D_EXPERT, D_MODEL), lambda i, te, nu: (te[i], 0, 0))],
        out_specs=pl.BlockSpec((TM_MOE, D_PACK), lambda i, te, nu: (i, 0)),
        scratch_shapes=[pltpu.VMEM((D_MODEL, D_EXPERT), BF16), pltpu.VMEM((D_MODEL, D_EXPERT), BF16),
                        pltpu.VMEM((D_EXPERT, D_MODEL), BF16)],
    )
    return pl.pallas_call(
        _moe_kernel,
        grid_spec=grid_spec,
        out_shape=jax.ShapeDtypeStruct(xs.shape, U32),
        compiler_params=_params(("arbitrary",)),
        name="moe_experts",
    )(tile_expert, n_used, xs, wg, wu, wd)


def _combine_kernel(dest_ref, route_ref, x1_ref, ys_ref, fg_ref, o_ref, r1_ref, r2_ref, sem, *, final):
    tm = x1_ref.shape[0]

    def row_copy(r, k):
        dst = r1_ref if k == 0 else r2_ref
        return pltpu.make_async_copy(ys_ref.at[pl.ds(dest_ref[0, 0, 2 * r + k], 1)], dst.at[pl.ds(r, 1)], sem)

    def r_start(r, c):
        row_copy(r, 0).start()
        row_copy(r, 1).start()
        return c

    def r_wait(r, c):
        row_copy(r, 0).wait()
        row_copy(r, 1).wait()
        return c

    lax.fori_loop(0, tm, r_start, 0, unroll=8)
    lax.fori_loop(0, tm, r_wait, 0, unroll=8)
    route = route_ref[...]
    w1 = route[:, ROUTE_W1:ROUTE_W1 + 1]
    w2 = route[:, ROUTE_W2:ROUTE_W2 + 1]
    a1, b1 = _unpack_pair(r1_ref[...])
    a2, b2 = _unpack_pair(r2_ref[...])
    y = jnp.concatenate([w1 * a1 + w2 * a2, w1 * b1 + w2 * b2], axis=1)
    x2 = x1_ref[...] + y
    o_ref[...] = _rms(x2, fg_ref[...]) if final else x2


def _combine(dest, route, x1, ys, fg, final):
    n = x1.shape[0]
    row = lambda width: pl.BlockSpec((TM_OUT, width), lambda i: (i, 0))
    return pl.pallas_call(
        functools.partial(_combine_kernel, final=final),
        grid=(n // TM_OUT,),
        in_specs=[pl.BlockSpec((1, 1, 2 * TM_OUT), lambda i: (i, 0, 0), memory_space=pltpu.SMEM),
                  row(LANES), row(D_MODEL), pl.BlockSpec(memory_space=pl.ANY), _full((1, D_MODEL))],
        out_specs=row(D_MODEL),
        out_shape=jax.ShapeDtypeStruct((n, D_MODEL), F32),
        scratch_shapes=[pltpu.VMEM((TM_OUT, D_PACK), U32), pltpu.VMEM((TM_OUT, D_PACK), U32),
                        pltpu.SemaphoreType.DMA(())],
        compiler_params=_params(("arbitrary",)),
        name="moe_combine",
    )(dest, route, x1, ys, fg.reshape(1, D_MODEL))


def _routing_tables(route, counts, n_tiles):
    cnt = counts[0, ROUTER_LANE0:ROUTER_LANE0 + N_EXPERTS].astype(I32)
    padded = (cnt + TM_MOE - 1) // TM_MOE * TM_MOE
    ends = jnp.cumsum(padded)
    starts = ends - padded
    e = route[:, ROUTE_E1:ROUTE_E2 + 1].astype(I32)
    r = route[:, ROUTE_R1:ROUTE_R2 + 1].astype(I32)
    dest = (starts[e] + r).reshape(-1, 1, 2 * TM_OUT)
    n_used = ends[-1] // TM_MOE
    tiles = jnp.arange(n_tiles, dtype=I32)
    te = jnp.searchsorted(ends, jnp.minimum(tiles, n_used - 1) * TM_MOE, side="right").astype(I32)
    te = jnp.minimum(te, N_EXPERTS - 1)
    last = jnp.where(padded > cnt, ends // TM_MOE - 1, -1)
    tail = n_used + jnp.arange(N_EXPERTS, dtype=I32)
    tail = jnp.where(tail < n_tiles, tail, -1)
    zero_tiles = jnp.concatenate([last, tail]).astype(I32)
    return dest, te, n_used.reshape(1).astype(I32), zero_tiles


def _n_moe_tiles(n_tokens):
    return (2 * n_tokens) // TM_MOE + N_EXPERTS


def kernel(x_prompt, x_sample, cache_k, cache_v, state_conv, state_ssm_re, state_ssm_im, attn_norm_g, w_in, attn_sinks, conv_w, conv_b, conv_ln_g, conv_ln_b, ssm_a_re, ssm_a_im, ssm_log_dt, ssm_b_re, ssm_b_im, ssm_c_re, ssm_c_im, ssm_d, ssm_glu_w, ssm_glu_b, grp_norm_g, w_out, ffn_norm_g, w_group_router, b_group_router, w_expert_router, b_expert_router, w_gate, w_up, w_down, final_norm_g):
    p = dict(ssm_a_re=ssm_a_re, ssm_a_im=ssm_a_im, ssm_log_dt=ssm_log_dt, ssm_b_re=ssm_b_re, ssm_b_im=ssm_b_im,
             ssm_c_re=ssm_c_re, ssm_c_im=ssm_c_im, ssm_d=ssm_d, ssm_glu_w=ssm_glu_w, ssm_glu_b=ssm_glu_b)
    depth = w_in.shape[0]
    n_seq, t, _ = x_prompt.shape
    n_dec, s_new, _ = x_sample.shape
    win = cache_k.shape[2]
    n_p = n_seq * t
    n_s = n_dec * s_new
    n = n_p + n_s
    assert n_seq == 2 and t % SCAN_T == 0 and t % CONV_T == 0 and n_p % TM_IN == 0 and n_s % TM_IN == 0
    hist = CONV_WIDTH - 1
    n_tiles = _n_moe_tiles(n)

    x = jnp.concatenate([x_prompt.reshape(n_p, D_MODEL), x_sample.reshape(n_s, D_MODEL)], axis=0)
    outs = {k: [] for k in ("kp", "vp", "cp", "rp", "ip", "ks", "vs", "cs", "rs", "is")}
    for l in range(depth):
        q, kv, u, su = _in_proj(x, attn_norm_g[l], w_in[l].astype(BF16))

        oa_p = _attn_prompt(q, kv, attn_sinks[l], n_seq, t)
        ck = cache_k[l].reshape(n_dec, win, D_KV)
        cv = cache_v[l].reshape(n_dec, win, D_KV)
        oa_s = _attn_sample(q, kv, ck, cv, attn_sinks[l], s_new, n_p)
        kv_p = kv[:n_p].reshape(n_seq, t, 2 * D_KV)[:, t - win:].astype(F32)
        kv_s = kv[n_p:].reshape(n_dec, s_new, 2 * D_KV).astype(F32)
        outs["kp"].append(kv_p[..., :D_KV].reshape(n_seq, win, N_KV_HEADS, HEAD_DIM))
        outs["vp"].append(kv_p[..., D_KV:].reshape(n_seq, win, N_KV_HEADS, HEAD_DIM))
        outs["ks"].append(jnp.concatenate([ck[:, s_new:], kv_s[..., :D_KV]], axis=1).reshape(n_dec, win, N_KV_HEADS, HEAD_DIM))
        outs["vs"].append(jnp.concatenate([cv[:, s_new:], kv_s[..., D_KV:]], axis=1).reshape(n_dec, win, N_KV_HEADS, HEAD_DIM))

        oc_p = _conv_prompt(u, conv_w[l], conv_b[l], conv_ln_g[l], conv_ln_b[l], n_seq, t)
        u_s = u[n_p:].reshape(n_dec, s_new, D_CONV)
        oc_s = _conv_sample(state_conv[l].reshape(n_dec, hist * D_CONV), u_s.reshape(n_dec, s_new * D_CONV),
                            conv_w[l], conv_b[l], conv_ln_g[l], conv_ln_b[l], s_new)
        outs["cp"].append(u[:n_p].reshape(n_seq, t, D_CONV)[:, t - hist:])
        outs["cs"].append(jnp.concatenate([state_conv[l], u_s], axis=1)[:, s_new:])

        sp = _s5_params(p, l)
        os0, os1, hre, him = _s5_prompt(su, sp, t)
        su_tb = su[n_p:].reshape(n_dec, s_new, D_SSM).transpose(1, 0, 2).reshape(n_s, D_SSM)
        os_tb, hr_s, hi_s = _s5_sample(su_tb, state_ssm_re[l].reshape(n_dec, N_STATE),
                                       state_ssm_im[l].reshape(n_dec, N_STATE), sp, s_new)
        os_s = os_tb.reshape(s_new, n_dec, D_SSM).transpose(1, 0, 2).reshape(n_s, D_SSM)
        outs["rp"].append(hre.reshape(n_seq, SSM_GROUPS, SSM_STATE))
        outs["ip"].append(him.reshape(n_seq, SSM_GROUPS, SSM_STATE))
        outs["rs"].append(hr_s.reshape(n_dec, SSM_GROUPS, SSM_STATE))
        outs["is"].append(hi_s.reshape(n_dec, SSM_GROUPS, SSM_STATE))

        oa = jnp.concatenate([oa_p, oa_s.astype(BF16)], axis=0)
        oc = jnp.concatenate([oc_p, oc_s.reshape(n_s, D_CONV).astype(BF16)], axis=0)
        os_ = jnp.concatenate([os0, os1, os_s], axis=0)

        wr = jnp.zeros((D_MODEL, LANES), F32)
        wr = wr.at[:, :N_EXPERT_GROUPS].set(w_group_router[l]).at[:, ROUTER_LANE0:ROUTER_LANE0 + N_EXPERTS].set(w_expert_router[l])
        wr_hi = wr.astype(BF16)
        wr_lo = (wr - wr_hi.astype(F32)).astype(BF16)
        br = jnp.zeros((1, LANES), F32)
        br = br.at[0, :N_EXPERT_GROUPS].set(b_group_router[l]).at[0, ROUTER_LANE0:ROUTER_LANE0 + N_EXPERTS].set(b_expert_router[l])
        x1, xn, route, counts = _merge_out(oa, oc, os_, x, grp_norm_g[l], w_out[l].astype(BF16), ffn_norm_g[l],
                                           wr_hi, wr_lo, br)

        dest, te, n_used, zero_tiles = _routing_tables(route, counts, n_tiles)
        xs = _dispatch(dest, zero_tiles, xn, n_tiles * TM_MOE)
        ys = _moe(te, n_used, xs, w_gate[l], w_up[l], w_down[l])
        x = _combine(dest, route, x1, ys, final_norm_g, final=(l == depth - 1))

    y_p = x[:n_p].reshape(n_seq, t, D_MODEL)
    y_s = x[n_p:].reshape(n_dec, s_new, D_MODEL)
    st = lambda k: jnp.stack(outs[k])
    return (y_p, y_s, st("kp"), st("vp"), st("cp"), st("rp"), st("ip"),
            st("ks"), st("vs"), st("cs"), st("rs"), st("is"))
```

```python
import functools

import jax
import jax.numpy as jnp
from jax import lax
from jax.experimental import pallas as pl
from jax.experimental.pallas import tpu as pltpu

F32 = jnp.float32
BF16 = jnp.bfloat16
U32 = jnp.uint32
I32 = jnp.int32

D_MODEL = 1024
N_HEADS = 8
N_KV_HEADS = 2
HEAD_DIM = 64
WINDOW = 128
D_ATTN = N_HEADS * HEAD_DIM
D_KV = N_KV_HEADS * HEAD_DIM
D_CONV = 256
CONV_WIDTH = 31
D_SSM = 256
SSM_GROUPS = 16
SSM_GROUP_CH = 16
SSM_STATE = 64
N_STATE = SSM_GROUPS * SSM_STATE
D_IN = D_ATTN + 2 * D_KV + 2 * D_CONV + D_SSM
N_EXPERT_GROUPS = 4
EXPERTS_PER_GROUP = 8
N_EXPERTS = N_EXPERT_GROUPS * EXPERTS_PER_GROUP
D_EXPERT = 512
EPS = 1e-6
NEG_INF = -1e30
SCALE = HEAD_DIM ** -0.5

LANES = 128
SUBLANES = 8
HALF = LANES // 2
D_PACK = D_MODEL // 2

TM_IN = 512
TM_OUT = 256
TM_MOE = 512
CONV_T = 256
CONV_CHUNK = 64
CONV_HALO = 32
SCAN_T = 256
SCAN_PITCH = SCAN_T + SUBLANES
SAMPLE_BT = 16
VMEM_LIMIT = 48 * 1024 * 1024

ROUTER_LANE0 = N_EXPERT_GROUPS
RUN = SUBLANES
N_LOCAL = 2 * TM_OUT + N_EXPERTS * RUN
TOK_W1, TOK_W2, TOK_P1, TOK_P2 = range(4)
TILE_NCHUNK, TILE_CHUNK_E, TILE_CHUNK_REL = range(3)
CHUNK_GROUP = 8
TRASH_ROWS = CHUNK_GROUP * RUN
GLOB_START, GLOB_ZERO, GLOB_TE, GLOB_TE_HI, GLOB_NUSED, GLOB_PTILES = range(6)


def _params(sem, vmem=VMEM_LIMIT):
    return pltpu.CompilerParams(dimension_semantics=sem, vmem_limit_bytes=vmem)


def _full(shape):
    zeros = (0,) * len(shape)
    return pl.BlockSpec(shape, lambda *_: zeros)


def _swap_halves(x):
    return jnp.concatenate([x[:, HALF:], x[:, :HALF]], axis=1)


def _rms(x, g):
    return x * lax.rsqrt(jnp.mean(x * x, axis=-1, keepdims=True) + EPS) * g


def _pack_pair(a, b):
    return pltpu.pack_elementwise([a, b], packed_dtype=BF16)


def _unpack_pair(p):
    return tuple(pltpu.unpack_elementwise(p, index=k, packed_dtype=BF16, unpacked_dtype=F32) for k in range(2))


def _pick(i, refs, starts):
    val = refs[0][...].astype(F32)
    for ref, start in zip(refs[1:], starts[1:]):
        val = jnp.where(i >= start, ref[...].astype(F32), val)
    return val


def _source_specs(sources, tm, width):
    specs, starts, start = [], [], 0
    for arr in sources:
        assert arr.shape[0] % tm == 0 and arr.shape[1] == width
        n_blk = arr.shape[0] // tm
        specs.append(pl.BlockSpec((tm, width), lambda i, s=start, nb=n_blk: (jnp.clip(i - s, 0, nb - 1), 0)))
        starts.append(start)
        start += n_blk
    return specs, tuple(starts), start


def _in_proj_kernel(*refs, starts):
    x_refs = refs[:len(starts)]
    g_ref, w_ref, q_ref, kv_ref, u_ref, su_ref = refs[len(starts):]
    hn = _rms(_pick(pl.program_id(0), x_refs, starts), g_ref[...]).astype(BF16)
    z = jnp.dot(hn, w_ref[...], preferred_element_type=F32)
    q_ref[...] = z[:, :D_ATTN].astype(BF16)
    kv_ref[...] = z[:, D_ATTN:D_ATTN + 2 * D_KV].astype(BF16)
    c0 = D_ATTN + 2 * D_KV
    u_ref[...] = z[:, c0:c0 + D_CONV] * jax.nn.sigmoid(z[:, c0 + D_CONV:c0 + 2 * D_CONV])
    su_ref[...] = z[:, c0 + 2 * D_CONV:].astype(BF16)


def _in_proj(x_sources, g, w_bf):
    x_specs, starts, n_blk = _source_specs(x_sources, TM_IN, D_MODEL)
    n = n_blk * TM_IN
    row = lambda width: pl.BlockSpec((TM_IN, width), lambda i: (i, 0))
    return pl.pallas_call(
        functools.partial(_in_proj_kernel, starts=starts),
        grid=(n_blk,),
        in_specs=x_specs + [_full((1, D_MODEL)), _full((D_MODEL, D_IN))],
        out_specs=[row(D_ATTN), row(2 * D_KV), row(D_CONV), row(D_SSM)],
        out_shape=[jax.ShapeDtypeStruct((n, D_ATTN), BF16), jax.ShapeDtypeStruct((n, 2 * D_KV), BF16),
                   jax.ShapeDtypeStruct((n, D_CONV), F32), jax.ShapeDtypeStruct((n, D_SSM), BF16)],
        compiler_params=_params(("arbitrary",)),
        name="in_proj",
    )(*x_sources, g.reshape(1, D_MODEL), w_bf)


def _softmax_pv(s, mask, sink, vmat):
    s = jnp.where(mask, s * SCALE, NEG_INF)
    m = jnp.maximum(jnp.max(s, axis=-1, keepdims=True), sink)
    p = jnp.exp(s - m)
    denom = jnp.sum(p, axis=-1, keepdims=True) + jnp.exp(sink - m)
    return jnp.dot(p.astype(BF16), vmat, preferred_element_type=F32) / denom


def _attn_prompt_kernel(sink_ref, q_ref, kvc_ref, kvp_ref, o_ref):
    i = pl.program_id(1)
    q = q_ref[...]
    kvc = kvc_ref[...]
    kvp = kvp_ref[...]
    kk = jnp.concatenate([kvp[:, :LANES], kvc[:, :LANES]], axis=0)
    vv = jnp.concatenate([kvp[:, LANES:], kvc[:, LANES:]], axis=0)
    kk_sw = _swap_halves(kk)
    vv_sw = _swap_halves(vv)
    lo = lax.broadcasted_iota(I32, (1, LANES), 1) < HALF
    a = lax.broadcasted_iota(I32, (WINDOW, 2 * WINDOW), 0)
    c = lax.broadcasted_iota(I32, (WINDOW, 2 * WINDOW), 1)
    diff = a + WINDOW - c
    mask = (diff >= 0) & (diff < WINDOW) & ((c >= WINDOW) | (i > 0))
    zero = jnp.zeros((WINDOW, LANES), BF16)
    for j in range(D_ATTN // LANES):
        kvh = (2 * j) // (N_HEADS // N_KV_HEADS)
        qt = q[:, LANES * j:LANES * (j + 1)]
        mats = ((kk, vv), (kk_sw, vv_sw)) if kvh == 0 else ((kk_sw, vv_sw), (kk, vv))
        outs = []
        for par in range(2):
            kmat, vmat = mats[par]
            qm = jnp.where(lo if par == 0 else jnp.logical_not(lo), qt, zero)
            s = lax.dot_general(qm, kmat, (((1,), (1,)), ((), ())), preferred_element_type=F32)
            outs.append(_softmax_pv(s, mask, sink_ref[2 * j + par], vmat))
        o_ref[:, LANES * j:LANES * (j + 1)] = jnp.where(lo, outs[0], outs[1]).astype(BF16)


def _attn_prompt(q, kv, sinks, n_seq, t):
    nb = t // WINDOW
    blk = lambda width, f: pl.BlockSpec((WINDOW, width), f)
    return pl.pallas_call(
        _attn_prompt_kernel,
        grid=(n_seq, nb),
        in_specs=[pl.BlockSpec(memory_space=pltpu.SMEM),
                  blk(D_ATTN, lambda b, i: (b * nb + i, 0)),
                  blk(2 * D_KV, lambda b, i: (b * nb + i, 0)),
                  blk(2 * D_KV, lambda b, i: (b * nb + jnp.maximum(i - 1, 0), 0))],
        out_specs=blk(D_ATTN, lambda b, i: (b * nb + i, 0)),
        out_shape=jax.ShapeDtypeStruct((n_seq * t, D_ATTN), BF16),
        compiler_params=_params(("arbitrary", "arbitrary")),
        name="attn_prompt",
    )(sinks, q, kv, kv)


def _attn_sample_kernel(sink_ref, q_ref, kv_ref, ck_ref, cv_ref, o_ref, qf_ref, kvf_ref, *, s_new):
    rows = SUBLANES
    n_pair = q_ref.shape[0] // rows
    per = rows // s_new
    qf_ref[...] = q_ref[...].astype(F32)
    kvf_ref[...] = kv_ref[...].astype(F32)
    lane = lax.broadcasted_iota(I32, (1, LANES), 1)
    lo = lane < HALF
    rid = lax.broadcasted_iota(I32, (N_HEADS * rows, 1), 0)
    head = rid // rows
    seq = (rid % rows) // s_new
    tok = rid % s_new
    sink = jnp.zeros((N_HEADS * rows, 1), F32)
    for h in range(N_HEADS):
        sink = jnp.where(head == h, sink_ref[h], sink)
    mask_c = lane > tok

    def pair(p, carry):
        r0 = pl.multiple_of(p * rows, rows)
        q8 = qf_ref[pl.ds(r0, rows), :]
        kv8 = kvf_ref[pl.ds(r0, rows), :]
        knew = kv8[:, :LANES]
        vnew = kv8[:, LANES:]
        pieces = []
        for h in range(N_HEADS):
            qt = q8[:, LANES * (h // 2):LANES * (h // 2 + 1)]
            tgt = h // (N_HEADS // N_KV_HEADS)
            if h % 2 != tgt:
                qt = _swap_halves(qt)
            pieces.append(jnp.where(lo if tgt == 0 else jnp.logical_not(lo), qt, 0.0))
        qm = jnp.concatenate(pieces, axis=0)
        qb = qm.astype(BF16)
        s_c = jnp.zeros((N_HEADS * rows, LANES), F32)
        for bb in range(per):
            kc = ck_ref[p * per + bb].astype(BF16)
            s_bb = lax.dot_general(qb, kc, (((1,), (1,)), ((), ())), preferred_element_type=F32)
            s_c = jnp.where(seq == bb, s_bb, s_c)
        s_c = jnp.where(mask_c, s_c * SCALE, NEG_INF)
        m = jnp.maximum(jnp.max(s_c, axis=-1, keepdims=True), sink)
        s_n = []
        for k in range(rows):
            valid = (seq == k // s_new) & (tok >= k % s_new)
            sk = jnp.sum(qm * knew[k:k + 1, :], axis=-1, keepdims=True) * SCALE
            sk = jnp.where(valid, sk, NEG_INF)
            s_n.append(sk)
            m = jnp.maximum(m, sk)
        p_c = jnp.exp(s_c - m)
        denom = jnp.sum(p_c, axis=-1, keepdims=True) + jnp.exp(sink - m)
        pb = p_c.astype(BF16)
        o = jnp.zeros((N_HEADS * rows, LANES), F32)
        for bb in range(per):
            vc = cv_ref[p * per + bb].astype(BF16)
            o = jnp.where(seq == bb, jnp.dot(pb, vc, preferred_element_type=F32), o)
        for k in range(rows):
            pk = jnp.exp(s_n[k] - m)
            denom = denom + pk
            o = o + pk.astype(BF16).astype(F32) * vnew[k:k + 1, :]
        o = o / denom
        for j in range(D_ATTN // LANES):
            kvh = (2 * j) // (N_HEADS // N_KV_HEADS)
            pe = o[rows * 2 * j:rows * (2 * j + 1), :]
            po = o[rows * (2 * j + 1):rows * (2 * j + 2), :]
            if kvh == 0:
                po = _swap_halves(po)
            else:
                pe = _swap_halves(pe)
            o_ref[pl.ds(r0, rows), LANES * j:LANES * (j + 1)] = jnp.where(lo, pe, po)
        return carry

    lax.fori_loop(0, n_pair, pair, 0)


def _attn_sample(q, kv, cache_k, cache_v, sinks, s_new, row0):
    n_seq, win, _ = cache_k.shape
    n_rows = n_seq * s_new
    rows_blk = SAMPLE_BT * s_new
    assert win == WINDOW and SUBLANES % s_new == 0 and n_seq % SAMPLE_BT == 0 and row0 % rows_blk == 0
    blk0 = row0 // rows_blk
    return pl.pallas_call(
        functools.partial(_attn_sample_kernel, s_new=s_new),
        grid=(n_seq // SAMPLE_BT,),
        in_specs=[pl.BlockSpec(memory_space=pltpu.SMEM),
                  pl.BlockSpec((rows_blk, D_ATTN), lambda i: (blk0 + i, 0)),
                  pl.BlockSpec((rows_blk, 2 * D_KV), lambda i: (blk0 + i, 0)),
                  pl.BlockSpec((SAMPLE_BT, win, D_KV), lambda i: (i, 0, 0)),
                  pl.BlockSpec((SAMPLE_BT, win, D_KV), lambda i: (i, 0, 0))],
        out_specs=pl.BlockSpec((rows_blk, D_ATTN), lambda i: (i, 0)),
        out_shape=jax.ShapeDtypeStruct((n_rows, D_ATTN), F32),
        scratch_shapes=[pltpu.VMEM((rows_blk, D_ATTN), F32), pltpu.VMEM((rows_blk, 2 * D_KV), F32)],
        compiler_params=_params(("arbitrary",)),
        name="attn_sample",
    )(sinks, q, kv, cache_k, cache_v)


def _ln_silu(y, lg, lb):
    mu = jnp.mean(y, axis=-1, keepdims=True)
    var = jnp.mean(jnp.square(y - mu), axis=-1, keepdims=True)
    yn = (y - mu) * lax.rsqrt(var + EPS) * lg + lb
    return yn * jax.nn.sigmoid(yn)


def _conv_prompt_kernel(u_ref, w_ref, b_ref, lg_ref, lb_ref, o_ref, ext_ref):
    i = pl.program_id(1)

    @pl.when(i == 0)
    def _():
        ext_ref[0:CONV_HALO, :] = jnp.zeros((CONV_HALO, D_CONV), F32)
        ext_ref[CONV_HALO + CONV_T:, :] = jnp.zeros((SUBLANES, D_CONV), F32)

    @pl.when(i > 0)
    def _():
        ext_ref[0:CONV_HALO, :] = ext_ref[CONV_T:CONV_T + CONV_HALO, :]

    ext_ref[CONV_HALO:CONV_HALO + CONV_T, :] = u_ref[...]
    shift = CONV_HALO - (CONV_WIDTH - 1)
    for cidx in range(CONV_T // CONV_CHUNK):
        r0 = cidx * CONV_CHUNK
        acc = jnp.zeros((CONV_CHUNK, D_CONV), F32)
        for rho in range(SUBLANES):
            part = jnp.zeros((CONV_CHUNK + SUBLANES, D_CONV), F32)
            for j in range(CONV_WIDTH):
                if (j + shift) % SUBLANES == rho:
                    base = r0 + j + shift - rho
                    part = part + w_ref[j:j + 1, :] * ext_ref[base:base + CONV_CHUNK + SUBLANES, :]
            acc = acc + part[rho:rho + CONV_CHUNK, :]
        o_ref[r0:r0 + CONV_CHUNK, :] = _ln_silu(acc + b_ref[...], lg_ref[...], lb_ref[...]).astype(BF16)


def _conv_prompt(u, w, b, lg, lb, n_seq, t):
    nt = t // CONV_T
    vec = _full((1, D_CONV))
    return pl.pallas_call(
        _conv_prompt_kernel,
        grid=(n_seq, nt),
        in_specs=[pl.BlockSpec((CONV_T, D_CONV), lambda s, i: (s * nt + i, 0)),
                  _full((CONV_WIDTH, D_CONV)), vec, vec, vec],
        out_specs=pl.BlockSpec((CONV_T, D_CONV), lambda s, i: (s * nt + i, 0)),
        out_shape=jax.ShapeDtypeStruct((n_seq * t, D_CONV), BF16),
        scratch_shapes=[pltpu.VMEM((CONV_T + CONV_HALO + SUBLANES, D_CONV), F32)],
        compiler_params=_params(("arbitrary", "arbitrary")),
        name="conv_prompt",
    )(u, w, b.reshape(1, D_CONV), lg.reshape(1, D_CONV), lb.reshape(1, D_CONV))


def _conv_sample_kernel(st_ref, u_ref, w_ref, b_ref, lg_ref, lb_ref, o_ref, *, s_new):
    hist = CONV_WIDTH - 1
    for t in range(s_new):
        acc = jnp.zeros((st_ref.shape[0], D_CONV), F32)
        for j in range(CONV_WIDTH):
            idx = t + j
            if idx < hist:
                piece = st_ref[:, idx * D_CONV:(idx + 1) * D_CONV]
            else:
                piece = u_ref[:, (idx - hist) * D_CONV:(idx - hist + 1) * D_CONV]
            acc = acc + w_ref[j:j + 1, :] * piece
        o_ref[:, t * D_CONV:(t + 1) * D_CONV] = _ln_silu(acc + b_ref[...], lg_ref[...], lb_ref[...])


def _conv_sample(state2d, u2d, w, b, lg, lb, s_new):
    n_seq = u2d.shape[0]
    return pl.pallas_call(
        functools.partial(_conv_sample_kernel, s_new=s_new),
        out_shape=jax.ShapeDtypeStruct((n_seq, s_new * D_CONV), F32),
        compiler_params=pltpu.CompilerParams(vmem_limit_bytes=VMEM_LIMIT),
        name="conv_sample",
    )(state2d, u2d, w, b.reshape(1, D_CONV), lg.reshape(1, D_CONV), lb.reshape(1, D_CONV))


def _s5_discretize(a_re, a_im, log_dt):
    dt = jnp.exp(log_dt)
    mag = jnp.exp(a_re * dt)
    ang = a_im * dt
    lr = mag * jnp.cos(ang)
    li = mag * jnp.sin(ang)
    den = a_re * a_re + a_im * a_im
    cr = ((lr - 1.0) * a_re + li * a_im) / den
    ci = (li * a_re - (lr - 1.0) * a_im) / den
    return lr, li, cr, ci


def _s5_bbar(arow_ref, bre_ref, bim_ref):
    _, _, cr, ci = _s5_discretize(arow_ref[0:1, :], arow_ref[1:2, :], arow_ref[2:3, :])
    bre = bre_ref[...]
    bim = bim_ref[...]
    return (cr * bre - ci * bim).astype(BF16), (cr * bim + ci * bre).astype(BF16)


def _s5_readout(h_re, h_im, u, cre_ref, cim_ref, d_ref, gw_ref, gb_ref):
    y = (jnp.dot(h_re.astype(BF16), cre_ref[...], preferred_element_type=F32)
         - jnp.dot(h_im.astype(BF16), cim_ref[...], preferred_element_type=F32)
         + d_ref[...] * u.astype(F32))
    z = jax.nn.gelu(y)
    gate = jnp.dot(z.astype(BF16), gw_ref[...], preferred_element_type=F32) + gb_ref[...]
    return (z * jax.nn.sigmoid(gate)).astype(BF16)


def _s5_prompt_kernel(su0_ref, su1_ref, arow_ref, atile_ref, bre_ref, bim_ref, cre_ref, cim_ref, d_ref,
                      gw_ref, gb_ref, o0_ref, o1_ref, hre_ref, him_ref,
                      bbr_ref, bbi_ref, lam_ref, car_ref, bur_ref, bui_ref, hbr_ref, hbi_ref):
    i = pl.program_id(0)
    n_slab = N_STATE // LANES
    su_refs = (su0_ref, su1_ref)
    o_refs = (o0_ref, o1_ref)

    @pl.when(i == 0)
    def _():
        bbr, bbi = _s5_bbar(arow_ref, bre_ref, bim_ref)
        bbr_ref[...] = bbr
        bbi_ref[...] = bbi
        lr, li, _, _ = _s5_discretize(atile_ref[0], atile_ref[1], atile_ref[2])
        lam_ref[0] = lr
        lam_ref[1] = li
        car_ref[...] = jnp.zeros(car_ref.shape, F32)

    for s in range(2):
        u = su_refs[s][...]
        br = jnp.dot(u, bbr_ref[...], preferred_element_type=F32)
        bi = jnp.dot(u, bbi_ref[...], preferred_element_type=F32)
        for j in range(n_slab):
            bur_ref[s, j * SCAN_PITCH:j * SCAN_PITCH + SCAN_T, :] = br[:, LANES * j:LANES * (j + 1)]
            bui_ref[s, j * SCAN_PITCH:j * SCAN_PITCH + SCAN_T, :] = bi[:, LANES * j:LANES * (j + 1)]

    lr = lam_ref[0]
    li = lam_ref[1]

    def step(t, carry):
        new = []
        for s in range(2):
            hr, hi = carry[2 * s], carry[2 * s + 1]
            rows = pl.ds(t, n_slab, stride=SCAN_PITCH)
            nr = lr * hr - li * hi + bur_ref.at[s][rows, :]
            ni = lr * hi + li * hr + bui_ref.at[s][rows, :]
            hbr_ref.at[s][rows, :] = nr
            hbi_ref.at[s][rows, :] = ni
            new += [nr, ni]
        return tuple(new)

    carry = lax.fori_loop(0, SCAN_T, step, tuple(car_ref[k] for k in range(4)), unroll=8)
    for k in range(4):
        car_ref[k] = carry[k]

    @pl.when(i == pl.num_programs(0) - 1)
    def _():
        for s in range(2):
            hre_ref[s] = carry[2 * s]
            him_ref[s] = carry[2 * s + 1]

    for s in range(2):
        h_re = jnp.concatenate([hbr_ref[s, j * SCAN_PITCH:j * SCAN_PITCH + SCAN_T, :] for j in range(n_slab)], axis=1)
        h_im = jnp.concatenate([hbi_ref[s, j * SCAN_PITCH:j * SCAN_PITCH + SCAN_T, :] for j in range(n_slab)], axis=1)
        o_refs[s][...] = _s5_readout(h_re, h_im, su_refs[s][...], cre_ref, cim_ref, d_ref, gw_ref, gb_ref)


def _s5_prompt(su, prm, t):
    nt = t // SCAN_T
    n_slab = N_STATE // LANES
    blk0 = pl.BlockSpec((SCAN_T, D_SSM), lambda i: (i, 0))
    blk1 = pl.BlockSpec((SCAN_T, D_SSM), lambda i: (nt + i, 0))
    oblk = pl.BlockSpec((SCAN_T, D_SSM), lambda i: (i, 0))
    state = pl.BlockSpec((2, SUBLANES, LANES), lambda i: (0, 0, 0))
    slabs = pltpu.VMEM((2, n_slab * SCAN_PITCH, LANES), F32)
    return pl.pallas_call(
        _s5_prompt_kernel,
        grid=(nt,),
        in_specs=[blk0, blk1, _full((3, N_STATE)), _full((3, SUBLANES, LANES)),
                  _full((D_SSM, N_STATE)), _full((D_SSM, N_STATE)), _full((N_STATE, D_SSM)), _full((N_STATE, D_SSM)),
                  _full((1, D_SSM)), _full((D_SSM, D_SSM)), _full((1, D_SSM))],
        out_specs=[oblk, oblk, state, state],
        out_shape=[jax.ShapeDtypeStruct((t, D_SSM), BF16), jax.ShapeDtypeStruct((t, D_SSM), BF16),
                   jax.ShapeDtypeStruct((2, SUBLANES, LANES), F32), jax.ShapeDtypeStruct((2, SUBLANES, LANES), F32)],
        scratch_shapes=[pltpu.VMEM((D_SSM, N_STATE), BF16), pltpu.VMEM((D_SSM, N_STATE), BF16),
                        pltpu.VMEM((2, SUBLANES, LANES), F32), pltpu.VMEM((4, SUBLANES, LANES), F32),
                        slabs, slabs, slabs, slabs],
        compiler_params=_params(("arbitrary",)),
        name="s5_prompt",
    )(su, su, prm["arow"], prm["atile"], prm["bre"], prm["bim"], prm["cre"], prm["cim"],
      prm["d"], prm["gw"], prm["gb"])


def _s5_sample_kernel(su_ref, h0r_ref, h0i_ref, arow_ref, bre_ref, bim_ref, cre_ref, cim_ref, d_ref, gw_ref, gb_ref,
                      o_ref, hr_ref, hi_ref, hbr_ref, hbi_ref, *, s_new):
    n_seq = h0r_ref.shape[0]
    lr, li, _, _ = _s5_discretize(arow_ref[0:1, :], arow_ref[1:2, :], arow_ref[2:3, :])
    bbr, bbi = _s5_bbar(arow_ref, bre_ref, bim_ref)
    u = su_ref[...]
    hbr_ref[...] = jnp.dot(u, bbr, preferred_element_type=F32)
    hbi_ref[...] = jnp.dot(u, bbi, preferred_element_type=F32)
    hr_ref[...] = h0r_ref[...]
    hi_ref[...] = h0i_ref[...]
    for t in range(s_new):
        rows = slice(t * n_seq, (t + 1) * n_seq)
        hr = hr_ref[...]
        hi = hi_ref[...]
        nr = lr * hr - li * hi + hbr_ref[rows, :]
        ni = lr * hi + li * hr + hbi_ref[rows, :]
        hbr_ref[rows, :] = nr
        hbi_ref[rows, :] = ni
        hr_ref[...] = nr
        hi_ref[...] = ni
    o_ref[...] = _s5_readout(hbr_ref[...], hbi_ref[...], u, cre_ref, cim_ref, d_ref, gw_ref, gb_ref)


def _s5_sample(su_tb, h0r, h0i, prm, s_new):
    n_rows = su_tb.shape[0]
    n_seq = n_rows // s_new
    return pl.pallas_call(
        functools.partial(_s5_sample_kernel, s_new=s_new),
        out_shape=[jax.ShapeDtypeStruct((n_rows, D_SSM), BF16),
                   jax.ShapeDtypeStruct((n_seq, N_STATE), F32), jax.ShapeDtypeStruct((n_seq, N_STATE), F32)],
        scratch_shapes=[pltpu.VMEM((n_rows, N_STATE), F32), pltpu.VMEM((n_rows, N_STATE), F32)],
        compiler_params=pltpu.CompilerParams(vmem_limit_bytes=VMEM_LIMIT),
        name="s5_sample",
    )(su_tb, h0r, h0i, prm["arow"], prm["bre"], prm["bim"], prm["cre"], prm["cim"], prm["d"], prm["gw"], prm["gb"])


def _s5_params(p, l):
    eye = jnp.eye(SSM_GROUPS, dtype=F32)

    def b_diag(b):
        return jnp.einsum("gnc,gh->gchn", b, eye).reshape(D_SSM, N_STATE)

    def c_diag(c):
        return jnp.einsum("gcn,gh->gnhc", c, eye).reshape(N_STATE, D_SSM)

    ldt = jnp.broadcast_to(p["ssm_log_dt"][l][:, None], (SSM_GROUPS, SSM_STATE))
    a3 = jnp.stack([p["ssm_a_re"][l], p["ssm_a_im"][l], ldt])
    return {
        "arow": a3.reshape(3, N_STATE),
        "atile": a3.reshape(3, SUBLANES, LANES),
        "bre": b_diag(p["ssm_b_re"][l]), "bim": b_diag(p["ssm_b_im"][l]),
        "cre": c_diag(p["ssm_c_re"][l]).astype(BF16), "cim": c_diag(p["ssm_c_im"][l]).astype(BF16),
        "d": p["ssm_d"][l].reshape(1, D_SSM),
        "gw": p["ssm_glu_w"][l].astype(BF16), "gb": p["ssm_glu_b"][l].reshape(1, D_SSM),
    }


def _merge_out_kernel(*refs, starts):
    it = iter(refs)
    oa_refs, oc_refs, os_refs, x_refs = ([next(it) for _ in s] for s in starts)
    gn_ref, wo_ref, fg_ref, wrh_ref, wrl_ref, br_ref = (next(it) for _ in range(6))
    x1_ref, xn_ref, tokc_ref, tokl_ref, tile_ref, glob_ref = (next(it) for _ in range(6))
    carry_ref = next(it)
    i = pl.program_id(0)
    tm = TM_OUT

    @pl.when(i == 0)
    def _():
        carry_ref[...] = jnp.zeros(carry_ref.shape, F32)

    gn = gn_ref[...]
    mix = jnp.concatenate([
        _rms(_pick(i, oa_refs, starts[0]), gn[:, :D_ATTN]),
        _rms(_pick(i, oc_refs, starts[1]), gn[:, D_ATTN:D_ATTN + D_CONV]),
        _rms(_pick(i, os_refs, starts[2]), gn[:, D_ATTN + D_CONV:]),
    ], axis=1).astype(BF16)
    x1 = _pick(i, x_refs, starts[3]) + jnp.dot(mix, wo_ref[...], preferred_element_type=F32)
    x1_ref[...] = x1
    xn = _rms(x1, fg_ref[...])
    xn_ref[...] = _pack_pair(xn[:, :D_PACK], xn[:, D_PACK:])

    xh = xn.astype(BF16)
    xl = (xn - xh.astype(F32)).astype(BF16)
    wh = wrh_ref[...]
    logits = (jnp.dot(xh, wh, preferred_element_type=F32) + jnp.dot(xl, wh, preferred_element_type=F32)
              + jnp.dot(xh, wrl_ref[...], preferred_element_type=F32) + br_ref[...])

    lane = lax.broadcasted_iota(I32, (1, LANES), 1).astype(F32)
    far = float(LANES)

    def first_max(v):
        top = jnp.max(v, axis=-1, keepdims=True)
        return top, jnp.min(jnp.where(v == top, lane, far), axis=-1, keepdims=True)

    is_group = lane < N_EXPERT_GROUPS
    g_top, g_idx = first_max(jnp.where(is_group, logits, -jnp.inf))
    g_w = 1.0 / jnp.sum(jnp.where(is_group, jnp.exp(logits - g_top), 0.0), axis=-1, keepdims=True)
    e_lo = ROUTER_LANE0 + EXPERTS_PER_GROUP * g_idx
    el = jnp.where((lane >= e_lo) & (lane < e_lo + EXPERTS_PER_GROUP), logits, -jnp.inf)
    v1, i1 = first_max(el)
    v2, i2 = first_max(jnp.where(lane == i1, -jnp.inf, el))
    t2 = jnp.exp(v2 - v1)
    w1 = g_w / (1.0 + t2)
    w2 = g_w * t2 / (1.0 + t2)

    sel1 = lane == i1
    sel2 = lane == i2
    onehot = jnp.where(sel1 | sel2, 1.0, 0.0)
    r_i = lax.broadcasted_iota(I32, (tm, tm), 0)
    c_i = lax.broadcasted_iota(I32, (tm, tm), 1)
    lower = jnp.where(c_i < r_i, 1.0, 0.0).astype(BF16)
    rloc = jnp.dot(lower, onehot.astype(BF16), preferred_element_type=F32)
    cnt = jnp.sum(onehot, axis=0, keepdims=True)
    cnt_pad = jnp.floor((cnt + (RUN - 1.0)) * (1.0 / RUN)) * RUN
    lr = lax.broadcasted_iota(I32, (LANES, LANES), 0)
    lc = lax.broadcasted_iota(I32, (LANES, LANES), 1)
    before = jnp.where(lr < lc, 1.0, 0.0).astype(BF16)
    zrow = jnp.zeros((1, LANES), F32)
    stacked = jnp.concatenate([cnt_pad] + [zrow] * (SUBLANES - 1), axis=0).astype(BF16)
    lstart = jnp.dot(stacked, before, preferred_element_type=F32)[0:1, :]

    def at(sel, v):
        return jnp.sum(jnp.where(sel, v, 0.0), axis=-1, keepdims=True)

    pos1, pos2 = at(sel1, lstart + rloc), at(sel2, lstart + rloc)
    tok = jnp.zeros((tm, LANES), F32)
    for col, val in ((TOK_W1, w1), (TOK_W2, w2), (TOK_P1, pos1), (TOK_P2, pos2)):
        tok = jnp.where(lane == col, val, tok)
    tokc_ref[...] = tok[:, :tokc_ref.shape[1]]
    tokl_ref[...] = jnp.transpose(tok)[:SUBLANES, :]

    carry = carry_ref[...]
    per_expert = jnp.concatenate([lstart, lstart + cnt_pad, carry - lstart, jnp.zeros((LANES - 3, LANES), F32)], axis=0)
    cols = jnp.transpose(per_expert)
    row0 = lane * RUN
    is_e_col = (lr >= ROUTER_LANE0) & (lr < ROUTER_LANE0 + N_EXPERTS)
    own = is_e_col & (cols[:, 0:1] <= row0) & (row0 < cols[:, 1:2])
    chunk_e = jnp.sum(jnp.where(own, lr.astype(F32), 0.0), axis=0, keepdims=True)
    chunk_rel = row0 + jnp.sum(jnp.where(own, cols[:, 2:3], 0.0), axis=0, keepdims=True)
    n_chunk = jnp.sum(cnt_pad, axis=-1, keepdims=True) * (1.0 / RUN)
    rows = [zrow] * SUBLANES
    rows[TILE_NCHUNK] = jnp.broadcast_to(n_chunk, (1, LANES))
    rows[TILE_CHUNK_E], rows[TILE_CHUNK_REL] = chunk_e, chunk_rel
    tile_ref[...] = jnp.concatenate(rows, axis=0).astype(I32)
    total = carry + cnt_pad
    carry_ref[...] = total

    @pl.when(i == pl.num_programs(0) - 1)
    def _():
        is_e = (lane >= ROUTER_LANE0) & (lane < ROUTER_LANE0 + N_EXPERTS)
        ptiles = jnp.floor((total + (TM_MOE - 1.0)) * (1.0 / TM_MOE))
        upto = jnp.where(lr <= lc, 1.0, 0.0).astype(BF16)
        pt8 = jnp.concatenate([ptiles, jnp.zeros((SUBLANES - 1, LANES), F32)], axis=0).astype(BF16)
        tend = jnp.dot(pt8, upto, preferred_element_type=F32)[0:1, :]
        n_used = jnp.max(tend, axis=-1, keepdims=True)
        e_last = jnp.max(jnp.where(ptiles > 0.0, lane - ROUTER_LANE0, -1.0), axis=-1, keepdims=True)
        tend_col = jnp.transpose(jnp.broadcast_to(tend, (LANES, LANES)))

        def tile_expert(first_tile):
            hit = is_e_col & (tend_col <= lane + first_tile)
            return jnp.minimum(jnp.sum(jnp.where(hit, 1.0, 0.0), axis=0, keepdims=True), e_last)

        rows = [zrow] * SUBLANES
        rows[GLOB_START] = (tend - ptiles) * TM_MOE
        rows[GLOB_ZERO] = jnp.where(is_e & (total > 0.0), tend - 1.0, -1.0)
        rows[GLOB_TE] = tile_expert(0.0)
        rows[GLOB_TE_HI] = tile_expert(float(LANES))
        rows[GLOB_NUSED] = jnp.broadcast_to(n_used, (1, LANES))
        rows[GLOB_PTILES] = ptiles
        glob_ref[...] = jnp.concatenate(rows, axis=0).astype(I32)


def _merge_out(oa_src, oc_src, os_src, x_src, gn, wo_bf, fg, wr_hi, wr_lo, br):
    specs, starts, n_blk = [], [], None
    for src, width in ((oa_src, D_ATTN), (oc_src, D_CONV), (os_src, D_SSM), (x_src, D_MODEL)):
        sp, st, nb = _source_specs(src, TM_OUT, width)
        assert n_blk in (None, nb)
        n_blk = nb
        specs += sp
        starts.append(st)
    n = n_blk * TM_OUT
    row = lambda width: pl.BlockSpec((TM_OUT, width), lambda i: (i, 0))
    tbl = lambda width: pl.BlockSpec((SUBLANES, width), lambda i: (i, 0))
    return pl.pallas_call(
        functools.partial(_merge_out_kernel, starts=tuple(starts)),
        grid=(n_blk,),
        in_specs=specs + [_full((1, D_MODEL)), _full((D_MODEL, D_MODEL)), _full((1, D_MODEL)),
                          _full((D_MODEL, LANES)), _full((D_MODEL, LANES)), _full((1, LANES))],
        out_specs=[row(D_MODEL), row(D_PACK), row(4), tbl(TM_OUT), tbl(LANES), _full((SUBLANES, LANES))],
        out_shape=[jax.ShapeDtypeStruct((n, D_MODEL), F32), jax.ShapeDtypeStruct((n, D_PACK), U32),
                   jax.ShapeDtypeStruct((n, 4), F32), jax.ShapeDtypeStruct((n_blk * SUBLANES, TM_OUT), F32),
                   jax.ShapeDtypeStruct((n_blk * SUBLANES, LANES), I32), jax.ShapeDtypeStruct((SUBLANES, LANES), I32)],
        scratch_shapes=[pltpu.VMEM((1, LANES), F32)],
        compiler_params=_params(("arbitrary",)),
        name="merge_out",
    )(*oa_src, *oc_src, *os_src, *x_src, gn.reshape(1, D_MODEL), wo_bf, fg.reshape(1, D_MODEL), wr_hi, wr_lo, br)


def _chunk_groups(tile_ref):
    n_chunk = tile_ref[TILE_NCHUNK, 0]
    return lax.shift_right_logical(n_chunk + (CHUNK_GROUP - 1), CHUNK_GROUP.bit_length() - 1)


def _for_each_chunk(tile_ref, glob_ref, fn):
    n_chunk = tile_ref[TILE_NCHUNK, 0]
    n_group = _chunk_groups(tile_ref)

    def group(g, carry):
        for k in range(CHUNK_GROUP):
            c = g * CHUNK_GROUP + k
            seg = glob_ref[GLOB_START * LANES + tile_ref[TILE_CHUNK_E, c]]
            fn(pl.multiple_of(c * RUN, RUN), c < n_chunk, seg + tile_ref[TILE_CHUNK_REL, c], k)
        return carry

    lax.fori_loop(0, n_group, group, 0)
    return n_group * CHUNK_GROUP


def _wait_chunks(n, wait_one):
    def group(g, carry):
        for _ in range(CHUNK_GROUP):
            wait_one()
        return carry

    lax.fori_loop(0, lax.shift_right_logical(n, CHUNK_GROUP.bit_length() - 1), group, 0)


def _dispatch_kernel(tile_ref, glob_ref, tokl_ref, xn_ref, xs_ref, sbuf_ref, zbuf_ref, inflight_ref, sem_z, sem_r):
    i = pl.program_id(0)
    last = pl.num_programs(0) - 1
    slot = lax.rem(i, 2)
    trash0 = xs_ref.shape[0] - TRASH_ROWS

    def zero_copy(t):
        return pltpu.make_async_copy(zbuf_ref, xs_ref.at[pl.ds(pl.multiple_of(t * TM_MOE, TM_MOE), TM_MOE)], sem_z)

    def for_zero_tiles(fn):
        def seg_last(k, c):
            t = glob_ref[GLOB_ZERO * LANES + k]

            @pl.when(t >= 0)
            def _():
                fn(t)
            return c

        def unused(t, c):
            fn(t)
            return c

        lax.fori_loop(0, LANES, seg_last, 0)
        lax.fori_loop(glob_ref[GLOB_NUSED * LANES], trash0 // TM_MOE, unused, 0)

    @pl.when(i == 0)
    def _():
        zbuf_ref[...] = jnp.zeros(zbuf_ref.shape, U32)
        trash = pltpu.make_async_copy(zbuf_ref.at[pl.ds(0, TRASH_ROWS)], xs_ref.at[pl.ds(trash0, TRASH_ROWS)], sem_z)
        trash.start()
        for_zero_tiles(lambda t: zero_copy(t).start())
        for_zero_tiles(lambda t: zero_copy(t).wait())
        trash.wait()
        inflight_ref[0] = 0

    q = lax.broadcasted_iota(I32, (N_LOCAL, 1), 0).astype(F32)
    hit = (q == tokl_ref[TOK_P1:TOK_P1 + 1, :]) | (q == tokl_ref[TOK_P2:TOK_P2 + 1, :])
    sel = jnp.where(hit, 1.0, 0.0).astype(BF16)
    a, b = _unpack_pair(xn_ref[...])
    sa = jnp.dot(sel, a.astype(BF16), preferred_element_type=F32)
    sb = jnp.dot(sel, b.astype(BF16), preferred_element_type=F32)
    sbuf_ref[slot] = _pack_pair(sa, sb)

    def start_chunk(local_row, real, sorted_row, k):
        dst = pl.multiple_of(jnp.where(real, sorted_row, trash0 + k * RUN), RUN)
        pltpu.make_async_copy(sbuf_ref.at[slot, pl.ds(local_row, RUN)], xs_ref.at[pl.ds(dst, RUN)], sem_r).start()

    def drain(n):
        _wait_chunks(n, lambda: pltpu.make_async_copy(sbuf_ref.at[0, pl.ds(0, RUN)], xs_ref.at[pl.ds(0, RUN)],
                                                      sem_r).wait())

    drain(inflight_ref[0])
    n_issued = _for_each_chunk(tile_ref, glob_ref, start_chunk)
    inflight_ref[0] = n_issued

    @pl.when(i == last)
    def _():
        drain(n_issued)


def _dispatch(tile_tbl, glob_flat, tokl, xn, n_rows_sorted):
    n = xn.shape[0]
    return pl.pallas_call(
        _dispatch_kernel,
        grid=(n // TM_OUT,),
        in_specs=[pl.BlockSpec((SUBLANES, LANES), lambda i: (i, 0), memory_space=pltpu.SMEM),
                  pl.BlockSpec(memory_space=pltpu.SMEM),
                  pl.BlockSpec((SUBLANES, TM_OUT), lambda i: (i, 0)),
                  pl.BlockSpec((TM_OUT, D_PACK), lambda i: (i, 0))],
        out_specs=pl.BlockSpec(memory_space=pl.ANY),
        out_shape=jax.ShapeDtypeStruct((n_rows_sorted, D_PACK), U32),
        scratch_shapes=[pltpu.VMEM((2, N_LOCAL, D_PACK), U32), pltpu.VMEM((TM_MOE, D_PACK), U32),
                        pltpu.SMEM((1,), I32), pltpu.SemaphoreType.DMA(()), pltpu.SemaphoreType.DMA(())],
        compiler_params=_params(("arbitrary",)),
        name="moe_dispatch",
    )(tile_tbl, glob_flat, tokl, xn)


def _tile_expert(glob_ref, i):
    return glob_ref[GLOB_TE * LANES + i]


def _moe_kernel(glob_ref, xs_ref, wg_hbm, wu_hbm, wd_hbm, ys_ref, wgb_ref, wub_ref, wdb_ref,
                sg_ref, su_ref, sd_ref, ord_ref, sem, *, layer):
    i = pl.program_id(0)
    used = i < glob_ref[GLOB_NUSED * LANES]
    expert = _tile_expert(glob_ref, i)
    new_expert = (i == 0) | (expert != _tile_expert(glob_ref, jnp.maximum(i - 1, 0)))

    def weight_copies(e, slot):
        return [pltpu.make_async_copy(src.at[layer, e], dst.at[slot], sem.at[slot])
                for src, dst in ((wg_hbm, sg_ref), (wu_hbm, su_ref), (wd_hbm, sd_ref))]

    @pl.when(i == 0)
    def _():
        ord_ref[0] = 0
        for cp in weight_copies(expert, 0):
            cp.start()

    @pl.when(used & new_expert)
    def _():
        slot = lax.rem(ord_ref[0], 2)
        for cp in weight_copies(expert, slot):
            cp.wait()
        nxt = lax.while_loop(
            lambda k: (k < N_EXPERTS) & (glob_ref[GLOB_PTILES * LANES + ROUTER_LANE0 + jnp.minimum(k, N_EXPERTS - 1)] == 0),
            lambda k: k + 1, expert + 1)

        @pl.when(nxt < N_EXPERTS)
        def _():
            for cp in weight_copies(nxt, 1 - slot):
                cp.start()

        wgb_ref[...] = sg_ref[slot].astype(BF16)
        wub_ref[...] = su_ref[slot].astype(BF16)
        wdb_ref[...] = sd_ref[slot].astype(BF16)
        ord_ref[0] = ord_ref[0] + 1

    @pl.when(used)
    def _():
        a, b = _unpack_pair(xs_ref[...])
        x = jnp.concatenate([a, b], axis=1).astype(BF16)
        gate = jnp.dot(x, wgb_ref[...], preferred_element_type=F32)
        up = jnp.dot(x, wub_ref[...], preferred_element_type=F32)
        h = (gate * jax.nn.sigmoid(gate) * up).astype(BF16)
        y = jnp.dot(h, wdb_ref[...], preferred_element_type=F32)
        ys_ref[...] = _pack_pair(y[:, :D_PACK], y[:, D_PACK:])

    @pl.when(jnp.logical_not(used))
    def _():
        zero = jnp.zeros(ys_ref.shape, F32)
        ys_ref[...] = _pack_pair(zero, zero)


def _moe(glob_flat, xs, wg, wu, wd, layer):
    n_tiles = (xs.shape[0] - TRASH_ROWS) // TM_MOE
    assert n_tiles <= 2 * LANES
    hbm = pl.BlockSpec(memory_space=pl.ANY)
    grid_spec = pltpu.PrefetchScalarGridSpec(
        num_scalar_prefetch=1,
        grid=(n_tiles,),
        in_specs=[pl.BlockSpec((TM_MOE, D_PACK), lambda i, g: (jnp.minimum(i, g[GLOB_NUSED * LANES] - 1), 0)),
                  hbm, hbm, hbm],
        out_specs=pl.BlockSpec((TM_MOE, D_PACK), lambda i, g: (i, 0)),
        scratch_shapes=[pltpu.VMEM((D_MODEL, D_EXPERT), BF16), pltpu.VMEM((D_MODEL, D_EXPERT), BF16),
                        pltpu.VMEM((D_EXPERT, D_MODEL), BF16),
                        pltpu.VMEM((2, D_MODEL, D_EXPERT), F32), pltpu.VMEM((2, D_MODEL, D_EXPERT), F32),
                        pltpu.VMEM((2, D_EXPERT, D_MODEL), F32), pltpu.SMEM((1,), I32),
                        pltpu.SemaphoreType.DMA((2,))],
    )
    return pl.pallas_call(
        functools.partial(_moe_kernel, layer=layer),
        grid_spec=grid_spec,
        out_shape=jax.ShapeDtypeStruct((n_tiles * TM_MOE, D_PACK), U32),
        compiler_params=_params(("arbitrary",)),
        name="moe_experts",
    )(glob_flat, xs, wg, wu, wd)


def _combine_kernel(tile_ref, next_ref, glob_ref, tokc_ref, x1_ref, ys_ref, fg_ref, *rest, n_first, final):
    n_out = 2 if final else 1
    out_refs = rest[:n_out]
    lbuf_ref, inflight_ref, sem = rest[n_out:]
    i = pl.program_id(0)
    last = pl.num_programs(0) - 1
    slot = lax.rem(i, 2)

    def gather(tbl_ref, dst_slot):
        def start(local_row, real, sorted_row, k):
            src = pl.multiple_of(jnp.where(real, sorted_row, 0), RUN)
            pltpu.make_async_copy(ys_ref.at[pl.ds(src, RUN)], lbuf_ref.at[dst_slot, pl.ds(local_row, RUN)],
                                  sem.at[dst_slot]).start()

        inflight_ref[dst_slot] = _for_each_chunk(tbl_ref, glob_ref, start)

    @pl.when(i == 0)
    def _():
        lbuf_ref[...] = jnp.zeros(lbuf_ref.shape, U32)
        gather(tile_ref, 0)

    @pl.when(i < last)
    def _():
        gather(next_ref, 1 - slot)

    _wait_chunks(inflight_ref[slot], lambda: pltpu.make_async_copy(
        ys_ref.at[pl.ds(0, RUN)], lbuf_ref.at[slot, pl.ds(0, RUN)], sem.at[slot]).wait())

    tokc = tokc_ref[...]
    w1 = tokc[:, TOK_W1:TOK_W1 + 1]
    w2 = tokc[:, TOK_W2:TOK_W2 + 1]
    col = lax.broadcasted_iota(I32, (1, N_LOCAL), 1).astype(F32)
    sel1 = jnp.where(col == tokc[:, TOK_P1:TOK_P1 + 1], 1.0, 0.0).astype(BF16)
    sel2 = jnp.where(col == tokc[:, TOK_P2:TOK_P2 + 1], 1.0, 0.0).astype(BF16)
    halves = []
    for part in _unpack_pair(lbuf_ref[slot]):
        rows = part.astype(BF16)
        halves.append(w1 * jnp.dot(sel1, rows, preferred_element_type=F32)
                      + w2 * jnp.dot(sel2, rows, preferred_element_type=F32))
    x2 = x1_ref[...] + jnp.concatenate(halves, axis=1)
    if not final:
        out_refs[0][...] = x2
    else:
        y = _rms(x2, fg_ref[...])

        @pl.when(i < n_first)
        def _():
            out_refs[0][...] = y

        @pl.when(i >= n_first)
        def _():
            out_refs[1][...] = y


def _combine(tile_tbl, glob_flat, tokc, x1, ys, fg, n_first_rows, final):
    n = x1.shape[0]
    n_blk = n // TM_OUT
    n_first = n_first_rows // TM_OUT
    assert n_first_rows % TM_OUT == 0
    row = lambda width: pl.BlockSpec((TM_OUT, width), lambda i: (i, 0))
    if final:
        out_specs = [pl.BlockSpec((TM_OUT, D_MODEL), lambda i: (jnp.minimum(i, n_first - 1), 0)),
                     pl.BlockSpec((TM_OUT, D_MODEL), lambda i: (jnp.maximum(i - n_first, 0), 0))]
        out_shape = [jax.ShapeDtypeStruct((n_first_rows, D_MODEL), F32),
                     jax.ShapeDtypeStruct((n - n_first_rows, D_MODEL), F32)]
    else:
        out_specs = [row(D_MODEL)]
        out_shape = [jax.ShapeDtypeStruct((n, D_MODEL), F32)]
    return pl.pallas_call(
        functools.partial(_combine_kernel, n_first=n_first, final=final),
        grid=(n_blk,),
        in_specs=[pl.BlockSpec((SUBLANES, LANES), lambda i: (i, 0), memory_space=pltpu.SMEM),
                  pl.BlockSpec((SUBLANES, LANES), lambda i: (jnp.minimum(i + 1, n_blk - 1), 0), memory_space=pltpu.SMEM),
                  pl.BlockSpec(memory_space=pltpu.SMEM),
                  row(4), row(D_MODEL), pl.BlockSpec(memory_space=pl.ANY), _full((1, D_MODEL))],
        out_specs=out_specs,
        out_shape=out_shape,
        scratch_shapes=[pltpu.VMEM((2, N_LOCAL, D_PACK), U32), pltpu.SMEM((2,), I32),
                        pltpu.SemaphoreType.DMA((2,))],
        compiler_params=_params(("arbitrary",)),
        name="moe_combine",
    )(tile_tbl, tile_tbl, glob_flat, tokc, x1, ys, fg.reshape(1, D_MODEL))


def _n_moe_tiles(n_tokens):
    n_runs = (n_tokens // TM_OUT) * N_EXPERTS
    return (2 * n_tokens + n_runs * (RUN - 1) + N_EXPERTS * (TM_MOE - 1)) // TM_MOE + 1


def kernel(x_prompt, x_sample, cache_k, cache_v, state_conv, state_ssm_re, state_ssm_im, attn_norm_g, w_in, attn_sinks, conv_w, conv_b, conv_ln_g, conv_ln_b, ssm_a_re, ssm_a_im, ssm_log_dt, ssm_b_re, ssm_b_im, ssm_c_re, ssm_c_im, ssm_d, ssm_glu_w, ssm_glu_b, grp_norm_g, w_out, ffn_norm_g, w_group_router, b_group_router, w_expert_router, b_expert_router, w_gate, w_up, w_down, final_norm_g):
    p = dict(ssm_a_re=ssm_a_re, ssm_a_im=ssm_a_im, ssm_log_dt=ssm_log_dt, ssm_b_re=ssm_b_re, ssm_b_im=ssm_b_im,
             ssm_c_re=ssm_c_re, ssm_c_im=ssm_c_im, ssm_d=ssm_d, ssm_glu_w=ssm_glu_w, ssm_glu_b=ssm_glu_b)
    depth = w_in.shape[0]
    n_seq, t, _ = x_prompt.shape
    n_dec, s_new, _ = x_sample.shape
    win = cache_k.shape[2]
    n_p = n_seq * t
    n_s = n_dec * s_new
    n = n_p + n_s
    assert n_seq == 2 and t % SCAN_T == 0 and t % CONV_T == 0 and n_p % TM_IN == 0 and n_s % TM_IN == 0
    hist = CONV_WIDTH - 1
    n_tiles = _n_moe_tiles(n)

    x_src = [x_prompt.reshape(n_p, D_MODEL), x_sample.reshape(n_s, D_MODEL)]
    outs = {k: [] for k in ("kp", "vp", "cp", "rp", "ip", "ks", "vs", "cs", "rs", "is")}
    for l in range(depth):
        q, kv, u, su = _in_proj(x_src, attn_norm_g[l], w_in[l].astype(BF16))

        oa_p = _attn_prompt(q, kv, attn_sinks[l], n_seq, t)
        ck = cache_k[l].reshape(n_dec, win, D_KV)
        cv = cache_v[l].reshape(n_dec, win, D_KV)
        oa_s = _attn_sample(q, kv, ck, cv, attn_sinks[l], s_new, n_p)
        kv_p = kv[:n_p].reshape(n_seq, t, 2 * D_KV)[:, t - win:].astype(F32)
        kv_s = kv[n_p:].reshape(n_dec, s_new, 2 * D_KV).astype(F32)
        outs["kp"].append(kv_p[..., :D_KV].reshape(n_seq, win, N_KV_HEADS, HEAD_DIM))
        outs["vp"].append(kv_p[..., D_KV:].reshape(n_seq, win, N_KV_HEADS, HEAD_DIM))
        outs["ks"].append(jnp.concatenate([ck[:, s_new:], kv_s[..., :D_KV]], axis=1).reshape(n_dec, win, N_KV_HEADS, HEAD_DIM))
        outs["vs"].append(jnp.concatenate([cv[:, s_new:], kv_s[..., D_KV:]], axis=1).reshape(n_dec, win, N_KV_HEADS, HEAD_DIM))

        oc_p = _conv_prompt(u, conv_w[l], conv_b[l], conv_ln_g[l], conv_ln_b[l], n_seq, t)
        u_s = u[n_p:].reshape(n_dec, s_new, D_CONV)
        oc_s = _conv_sample(state_conv[l].reshape(n_dec, hist * D_CONV), u_s.reshape(n_dec, s_new * D_CONV),
                            conv_w[l], conv_b[l], conv_ln_g[l], conv_ln_b[l], s_new)
        outs["cp"].append(u[:n_p].reshape(n_seq, t, D_CONV)[:, t - hist:])
        outs["cs"].append(jnp.concatenate([state_conv[l], u_s], axis=1)[:, s_new:])

        sp = _s5_params(p, l)
        os0, os1, hre, him = _s5_prompt(su, sp, t)
        su_tb = su[n_p:].reshape(n_dec, s_new, D_SSM).transpose(1, 0, 2).reshape(n_s, D_SSM)
        os_tb, hr_s, hi_s = _s5_sample(su_tb, state_ssm_re[l].reshape(n_dec, N_STATE),
                                       state_ssm_im[l].reshape(n_dec, N_STATE), sp, s_new)
        os_s = os_tb.reshape(s_new, n_dec, D_SSM).transpose(1, 0, 2).reshape(n_s, D_SSM)
        outs["rp"].append(hre.reshape(n_seq, SSM_GROUPS, SSM_STATE))
        outs["ip"].append(him.reshape(n_seq, SSM_GROUPS, SSM_STATE))
        outs["rs"].append(hr_s.reshape(n_dec, SSM_GROUPS, SSM_STATE))
        outs["is"].append(hi_s.reshape(n_dec, SSM_GROUPS, SSM_STATE))

        unused = LANES - N_EXPERT_GROUPS - N_EXPERTS
        wr = jnp.pad(jnp.concatenate([w_group_router[l], w_expert_router[l]], axis=1), ((0, 0), (0, unused)))
        wr_hi = wr.astype(BF16)
        wr_lo = (wr - wr_hi.astype(F32)).astype(BF16)
        br = jnp.pad(jnp.concatenate([b_group_router[l], b_expert_router[l]]), (0, unused)).reshape(1, LANES)
        x1, xn, tokc, tokl, tile_tbl, glob = _merge_out(
            [oa_p, oa_s], [oc_p, oc_s.reshape(n_s, D_CONV)], [os0, os1, os_s], x_src,
            grp_norm_g[l], w_out[l].astype(BF16), ffn_norm_g[l], wr_hi, wr_lo, br)

        glob_flat = glob.reshape(SUBLANES * LANES)
        xs = _dispatch(tile_tbl, glob_flat, tokl, xn, n_tiles * TM_MOE + TRASH_ROWS)
        ys = _moe(glob_flat, xs, w_gate, w_up, w_down, l)
        res = _combine(tile_tbl, glob_flat, tokc, x1, ys, final_norm_g, n_p, final=(l == depth - 1))
        x_src = [res[0]]

    y_p = res[0].reshape(n_seq, t, D_MODEL)
    y_s = res[1].reshape(n_dec, s_new, D_MODEL)
    st = lambda k: jnp.stack(outs[k])
    return (y_p, y_s, st("kp"), st("vp"), st("cp"), st("rp"), st("ip"),
            st("ks"), st("vs"), st("cs"), st("rs"), st("is"))
```

```python
import functools

import jax
import jax.numpy as jnp
from jax import lax
from jax.experimental import pallas as pl
from jax.experimental.pallas import tpu as pltpu

F32 = jnp.float32
BF16 = jnp.bfloat16
U32 = jnp.uint32
I32 = jnp.int32

D_MODEL = 1024
N_HEADS = 8
N_KV_HEADS = 2
HEAD_DIM = 64
WINDOW = 128
D_ATTN = N_HEADS * HEAD_DIM
D_KV = N_KV_HEADS * HEAD_DIM
D_CONV = 256
CONV_WIDTH = 31
D_SSM = 256
SSM_GROUPS = 16
SSM_GROUP_CH = 16
SSM_STATE = 64
N_STATE = SSM_GROUPS * SSM_STATE
D_IN = D_ATTN + 2 * D_KV + 2 * D_CONV + D_SSM
N_EXPERT_GROUPS = 4
EXPERTS_PER_GROUP = 8
N_EXPERTS = N_EXPERT_GROUPS * EXPERTS_PER_GROUP
D_EXPERT = 512
EPS = 1e-6
NEG_INF = -1e30
SCALE = HEAD_DIM ** -0.5

LANES = 128
SUBLANES = 8
HALF = LANES // 2
D_PACK = D_MODEL // 2

TM_IN = 512
TM_OUT = 256
TM_MOE = 512
CONV_T = 256
CONV_CHUNK = 64
CONV_HALO = 32
SCAN_T = 256
SCAN_PITCH = SCAN_T + SUBLANES
SAMPLE_BT = 16
ATTN_QB = 4
MERGE_SUB = 2
VMEM_LIMIT = 48 * 1024 * 1024

ROUTER_LANE0 = N_EXPERT_GROUPS
RUN = SUBLANES
N_LOCAL = 2 * TM_OUT + N_EXPERTS * RUN
TOK_W1, TOK_W2, TOK_P1, TOK_P2 = range(4)
TILE_NCHUNK, TILE_CHUNK_E, TILE_CHUNK_REL = range(3)
CHUNK_GROUP = 8
TRASH_ROWS = CHUNK_GROUP * RUN
GLOB_START, GLOB_ZERO, GLOB_TE, GLOB_TE_HI, GLOB_NUSED, GLOB_PTILES = range(6)


def _params(sem, vmem=VMEM_LIMIT):
    return pltpu.CompilerParams(dimension_semantics=sem, vmem_limit_bytes=vmem)


def _full(shape):
    zeros = (0,) * len(shape)
    return pl.BlockSpec(shape, lambda *_: zeros)


def _swap_halves(x):
    return jnp.concatenate([x[:, HALF:], x[:, :HALF]], axis=1)


def _rms(x, g):
    return x * lax.rsqrt(jnp.mean(x * x, axis=-1, keepdims=True) + EPS) * g


def _pack_pair(a, b):
    return pltpu.pack_elementwise([a, b], packed_dtype=BF16)


def _unpack_pair(p):
    return tuple(pltpu.unpack_elementwise(p, index=k, packed_dtype=BF16, unpacked_dtype=F32) for k in range(2))


def _pick(i, refs, starts):
    val = refs[0][...].astype(F32)
    for ref, start in zip(refs[1:], starts[1:]):
        val = jnp.where(i >= start, ref[...].astype(F32), val)
    return val


def _source_specs(sources, tm, width):
    specs, starts, start = [], [], 0
    for arr in sources:
        assert arr.shape[0] % tm == 0 and arr.shape[1] == width
        n_blk = arr.shape[0] // tm
        specs.append(pl.BlockSpec((tm, width), lambda i, s=start, nb=n_blk: (jnp.clip(i - s, 0, nb - 1), 0)))
        starts.append(start)
        start += n_blk
    return specs, tuple(starts), start


def _in_proj_kernel(*refs, starts):
    x_refs = refs[:len(starts)]
    g_ref, w_ref, q_ref, kv_ref, u_ref, su_ref = refs[len(starts):]
    hn = _rms(_pick(pl.program_id(0), x_refs, starts), g_ref[...]).astype(BF16)
    z = jnp.dot(hn, w_ref[...], preferred_element_type=F32)
    q_ref[...] = z[:, :D_ATTN].astype(BF16)
    kv_ref[...] = z[:, D_ATTN:D_ATTN + 2 * D_KV].astype(BF16)
    c0 = D_ATTN + 2 * D_KV
    u_ref[...] = z[:, c0:c0 + D_CONV] * jax.nn.sigmoid(z[:, c0 + D_CONV:c0 + 2 * D_CONV])
    su_ref[...] = z[:, c0 + 2 * D_CONV:].astype(BF16)


def _in_proj(x_sources, g, w_bf):
    x_specs, starts, n_blk = _source_specs(x_sources, TM_IN, D_MODEL)
    n = n_blk * TM_IN
    row = lambda width: pl.BlockSpec((TM_IN, width), lambda i: (i, 0))
    return pl.pallas_call(
        functools.partial(_in_proj_kernel, starts=starts),
        grid=(n_blk,),
        in_specs=x_specs + [_full((1, D_MODEL)), _full((D_MODEL, D_IN))],
        out_specs=[row(D_ATTN), row(2 * D_KV), row(D_CONV), row(D_SSM)],
        out_shape=[jax.ShapeDtypeStruct((n, D_ATTN), BF16), jax.ShapeDtypeStruct((n, 2 * D_KV), BF16),
                   jax.ShapeDtypeStruct((n, D_CONV), F32), jax.ShapeDtypeStruct((n, D_SSM), BF16)],
        compiler_params=_params(("arbitrary",)),
        name="in_proj",
    )(*x_sources, g.reshape(1, D_MODEL), w_bf)


def _softmax_pv(s, mask, sink, vmat):
    s = jnp.where(mask, s * SCALE, NEG_INF)
    m = jnp.maximum(jnp.max(s, axis=-1, keepdims=True), sink)
    p = jnp.exp(s - m)
    denom = jnp.sum(p, axis=-1, keepdims=True) + jnp.exp(sink - m)
    return jnp.dot(p.astype(BF16), vmat, preferred_element_type=F32) / denom


def _attn_prompt_kernel(sink_ref, q_ref, kvc_ref, kvp_ref, o_ref):
    i = pl.program_id(1)
    lo = lax.broadcasted_iota(I32, (1, LANES), 1) < HALF
    a = lax.broadcasted_iota(I32, (WINDOW, 2 * WINDOW), 0)
    c = lax.broadcasted_iota(I32, (WINDOW, 2 * WINDOW), 1)
    diff = a + WINDOW - c
    band = (diff >= 0) & (diff < WINDOW)
    zero = jnp.zeros((WINDOW, LANES), BF16)
    for sub in range(ATTN_QB):
        rows = slice(sub * WINDOW, (sub + 1) * WINDOW)
        q = q_ref[rows, :]
        kvc = kvc_ref[rows, :]
        kvp = kvp_ref[...] if sub == 0 else kvc_ref[(sub - 1) * WINDOW:sub * WINDOW, :]
        mask = band & ((c >= WINDOW) | (i > 0)) if sub == 0 else band
        kk = jnp.concatenate([kvp[:, :LANES], kvc[:, :LANES]], axis=0)
        vv = jnp.concatenate([kvp[:, LANES:], kvc[:, LANES:]], axis=0)
        kk_sw = _swap_halves(kk)
        vv_sw = _swap_halves(vv)
        for j in range(D_ATTN // LANES):
            kvh = (2 * j) // (N_HEADS // N_KV_HEADS)
            qt = q[:, LANES * j:LANES * (j + 1)]
            mats = ((kk, vv), (kk_sw, vv_sw)) if kvh == 0 else ((kk_sw, vv_sw), (kk, vv))
            outs = []
            for par in range(2):
                kmat, vmat = mats[par]
                qm = jnp.where(lo if par == 0 else jnp.logical_not(lo), qt, zero)
                s = lax.dot_general(qm, kmat, (((1,), (1,)), ((), ())), preferred_element_type=F32)
                outs.append(_softmax_pv(s, mask, sink_ref[2 * j + par], vmat))
            o_ref[rows, LANES * j:LANES * (j + 1)] = jnp.where(lo, outs[0], outs[1]).astype(BF16)


def _attn_prompt(q, kv, sinks, n_seq, t):
    rows = ATTN_QB * WINDOW
    assert t % rows == 0
    nb = t // rows
    cur = lambda width: pl.BlockSpec((rows, width), lambda b, i: (b * nb + i, 0))
    prev = pl.BlockSpec((WINDOW, 2 * D_KV), lambda b, i: ((b * nb + i) * ATTN_QB - jnp.minimum(i, 1), 0))
    return pl.pallas_call(
        _attn_prompt_kernel,
        grid=(n_seq, nb),
        in_specs=[pl.BlockSpec(memory_space=pltpu.SMEM), cur(D_ATTN), cur(2 * D_KV), prev],
        out_specs=cur(D_ATTN),
        out_shape=jax.ShapeDtypeStruct((n_seq * t, D_ATTN), BF16),
        compiler_params=_params(("arbitrary", "arbitrary")),
        name="attn_prompt",
    )(sinks, q, kv, kv)


def _attn_sample_kernel(sink_ref, q_ref, kv_ref, ck_ref, cv_ref, o_ref, qf_ref, kvf_ref, *, s_new):
    rows = SUBLANES
    n_pair = q_ref.shape[0] // rows
    per = rows // s_new
    qf_ref[...] = q_ref[...].astype(F32)
    kvf_ref[...] = kv_ref[...].astype(F32)
    lane = lax.broadcasted_iota(I32, (1, LANES), 1)
    lo = lane < HALF
    rid = lax.broadcasted_iota(I32, (N_HEADS * rows, 1), 0)
    head = rid // rows
    seq = (rid % rows) // s_new
    tok = rid % s_new
    sink = jnp.zeros((N_HEADS * rows, 1), F32)
    for h in range(N_HEADS):
        sink = jnp.where(head == h, sink_ref[h], sink)
    mask_c = lane > tok

    def pair(p, carry):
        r0 = pl.multiple_of(p * rows, rows)
        q8 = qf_ref[pl.ds(r0, rows), :]
        kv8 = kvf_ref[pl.ds(r0, rows), :]
        knew = kv8[:, :LANES]
        vnew = kv8[:, LANES:]
        pieces = []
        for h in range(N_HEADS):
            qt = q8[:, LANES * (h // 2):LANES * (h // 2 + 1)]
            tgt = h // (N_HEADS // N_KV_HEADS)
            if h % 2 != tgt:
                qt = _swap_halves(qt)
            pieces.append(jnp.where(lo if tgt == 0 else jnp.logical_not(lo), qt, 0.0))
        qm = jnp.concatenate(pieces, axis=0)
        qb = qm.astype(BF16)
        s_c = jnp.zeros((N_HEADS * rows, LANES), F32)
        for bb in range(per):
            kc = ck_ref[p * per + bb].astype(BF16)
            s_bb = lax.dot_general(qb, kc, (((1,), (1,)), ((), ())), preferred_element_type=F32)
            s_c = jnp.where(seq == bb, s_bb, s_c)
        s_c = jnp.where(mask_c, s_c * SCALE, NEG_INF)
        m = jnp.maximum(jnp.max(s_c, axis=-1, keepdims=True), sink)
        s_n = []
        for k in range(rows):
            valid = (seq == k // s_new) & (tok >= k % s_new)
            sk = jnp.sum(qm * knew[k:k + 1, :], axis=-1, keepdims=True) * SCALE
            sk = jnp.where(valid, sk, NEG_INF)
            s_n.append(sk)
            m = jnp.maximum(m, sk)
        p_c = jnp.exp(s_c - m)
        denom = jnp.sum(p_c, axis=-1, keepdims=True) + jnp.exp(sink - m)
        pb = p_c.astype(BF16)
        o = jnp.zeros((N_HEADS * rows, LANES), F32)
        for bb in range(per):
            vc = cv_ref[p * per + bb].astype(BF16)
            o = jnp.where(seq == bb, jnp.dot(pb, vc, preferred_element_type=F32), o)
        for k in range(rows):
            pk = jnp.exp(s_n[k] - m)
            denom = denom + pk
            o = o + pk.astype(BF16).astype(F32) * vnew[k:k + 1, :]
        o = o / denom
        for j in range(D_ATTN // LANES):
            kvh = (2 * j) // (N_HEADS // N_KV_HEADS)
            pe = o[rows * 2 * j:rows * (2 * j + 1), :]
            po = o[rows * (2 * j + 1):rows * (2 * j + 2), :]
            if kvh == 0:
                po = _swap_halves(po)
            else:
                pe = _swap_halves(pe)
            o_ref[pl.ds(r0, rows), LANES * j:LANES * (j + 1)] = jnp.where(lo, pe, po)
        return carry

    lax.fori_loop(0, n_pair, pair, 0)


def _attn_sample(q, kv, cache_k, cache_v, sinks, s_new, row0, n_seq, seq0):
    win = cache_k.shape[1]
    n_rows = n_seq * s_new
    rows_blk = SAMPLE_BT * s_new
    assert win == WINDOW and SUBLANES % s_new == 0 and n_seq % SAMPLE_BT == 0 and row0 % rows_blk == 0
    assert seq0 % SAMPLE_BT == 0
    blk0 = row0 // rows_blk
    cblk0 = seq0 // SAMPLE_BT
    return pl.pallas_call(
        functools.partial(_attn_sample_kernel, s_new=s_new),
        grid=(n_seq // SAMPLE_BT,),
        in_specs=[pl.BlockSpec(memory_space=pltpu.SMEM),
                  pl.BlockSpec((rows_blk, D_ATTN), lambda i: (blk0 + i, 0)),
                  pl.BlockSpec((rows_blk, 2 * D_KV), lambda i: (blk0 + i, 0)),
                  pl.BlockSpec((SAMPLE_BT, win, D_KV), lambda i: (cblk0 + i, 0, 0)),
                  pl.BlockSpec((SAMPLE_BT, win, D_KV), lambda i: (cblk0 + i, 0, 0))],
        out_specs=pl.BlockSpec((rows_blk, D_ATTN), lambda i: (i, 0)),
        out_shape=jax.ShapeDtypeStruct((n_rows, D_ATTN), F32),
        scratch_shapes=[pltpu.VMEM((rows_blk, D_ATTN), F32), pltpu.VMEM((rows_blk, 2 * D_KV), F32)],
        compiler_params=_params(("arbitrary",)),
        name="attn_sample",
    )(sinks, q, kv, cache_k, cache_v)


def _ln_silu(y, lg, lb):
    mu = jnp.mean(y, axis=-1, keepdims=True)
    var = jnp.mean(jnp.square(y - mu), axis=-1, keepdims=True)
    yn = (y - mu) * lax.rsqrt(var + EPS) * lg + lb
    return yn * jax.nn.sigmoid(yn)


def _conv_prompt_kernel(u_ref, w_ref, b_ref, lg_ref, lb_ref, o_ref, ext_ref):
    i = pl.program_id(1)

    @pl.when(i == 0)
    def _():
        ext_ref[0:CONV_HALO, :] = jnp.zeros((CONV_HALO, D_CONV), F32)
        ext_ref[CONV_HALO + CONV_T:, :] = jnp.zeros((SUBLANES, D_CONV), F32)

    @pl.when(i > 0)
    def _():
        ext_ref[0:CONV_HALO, :] = ext_ref[CONV_T:CONV_T + CONV_HALO, :]

    ext_ref[CONV_HALO:CONV_HALO + CONV_T, :] = u_ref[...]
    shift = CONV_HALO - (CONV_WIDTH - 1)
    for cidx in range(CONV_T // CONV_CHUNK):
        r0 = cidx * CONV_CHUNK
        acc = jnp.zeros((CONV_CHUNK, D_CONV), F32)
        for rho in range(SUBLANES):
            part = jnp.zeros((CONV_CHUNK + SUBLANES, D_CONV), F32)
            for j in range(CONV_WIDTH):
                if (j + shift) % SUBLANES == rho:
                    base = r0 + j + shift - rho
                    part = part + w_ref[j:j + 1, :] * ext_ref[base:base + CONV_CHUNK + SUBLANES, :]
            acc = acc + part[rho:rho + CONV_CHUNK, :]
        o_ref[r0:r0 + CONV_CHUNK, :] = _ln_silu(acc + b_ref[...], lg_ref[...], lb_ref[...]).astype(BF16)


def _conv_prompt(u, w, b, lg, lb, n_seq, t):
    nt = t // CONV_T
    vec = _full((1, D_CONV))
    return pl.pallas_call(
        _conv_prompt_kernel,
        grid=(n_seq, nt),
        in_specs=[pl.BlockSpec((CONV_T, D_CONV), lambda s, i: (s * nt + i, 0)),
                  _full((CONV_WIDTH, D_CONV)), vec, vec, vec],
        out_specs=pl.BlockSpec((CONV_T, D_CONV), lambda s, i: (s * nt + i, 0)),
        out_shape=jax.ShapeDtypeStruct((n_seq * t, D_CONV), BF16),
        scratch_shapes=[pltpu.VMEM((CONV_T + CONV_HALO + SUBLANES, D_CONV), F32)],
        compiler_params=_params(("arbitrary", "arbitrary")),
        name="conv_prompt",
    )(u, w, b.reshape(1, D_CONV), lg.reshape(1, D_CONV), lb.reshape(1, D_CONV))


def _conv_sample_kernel(st_ref, u_ref, w_ref, b_ref, lg_ref, lb_ref, o_ref, *, s_new):
    hist = CONV_WIDTH - 1
    for t in range(s_new):
        acc = jnp.zeros((st_ref.shape[0], D_CONV), F32)
        for j in range(CONV_WIDTH):
            idx = t + j
            if idx < hist:
                piece = st_ref[:, idx * D_CONV:(idx + 1) * D_CONV]
            else:
                piece = u_ref[:, (idx - hist) * D_CONV:(idx - hist + 1) * D_CONV]
            acc = acc + w_ref[j:j + 1, :] * piece
        o_ref[:, t * D_CONV:(t + 1) * D_CONV] = _ln_silu(acc + b_ref[...], lg_ref[...], lb_ref[...])


def _conv_sample(state2d, u2d, w, b, lg, lb, s_new, layer):
    n_seq = u2d.shape[0]
    vec = _full((1, D_CONV))
    return pl.pallas_call(
        functools.partial(_conv_sample_kernel, s_new=s_new),
        grid=(1,),
        in_specs=[pl.BlockSpec((n_seq, state2d.shape[1]), lambda i: (layer, 0)), _full(u2d.shape),
                  _full((CONV_WIDTH, D_CONV)), vec, vec, vec],
        out_specs=_full((n_seq, s_new * D_CONV)),
        out_shape=jax.ShapeDtypeStruct((n_seq, s_new * D_CONV), F32),
        compiler_params=_params(("arbitrary",)),
        name="conv_sample",
    )(state2d, u2d, w, b.reshape(1, D_CONV), lg.reshape(1, D_CONV), lb.reshape(1, D_CONV))


def _s5_discretize(a_re, a_im, log_dt):
    dt = jnp.exp(log_dt)
    mag = jnp.exp(a_re * dt)
    ang = a_im * dt
    lr = mag * jnp.cos(ang)
    li = mag * jnp.sin(ang)
    den = a_re * a_re + a_im * a_im
    cr = ((lr - 1.0) * a_re + li * a_im) / den
    ci = (li * a_re - (lr - 1.0) * a_im) / den
    return lr, li, cr, ci


def _s5_bbar(arow_ref, bre_ref, bim_ref):
    _, _, cr, ci = _s5_discretize(arow_ref[0:1, :], arow_ref[1:2, :], arow_ref[2:3, :])
    bre = bre_ref[...]
    bim = bim_ref[...]
    return (cr * bre - ci * bim).astype(BF16), (cr * bim + ci * bre).astype(BF16)


def _s5_readout(h_re, h_im, u, cre_ref, cim_ref, d_ref, gw_ref, gb_ref):
    y = (jnp.dot(h_re.astype(BF16), cre_ref[...], preferred_element_type=F32)
         - jnp.dot(h_im.astype(BF16), cim_ref[...], preferred_element_type=F32)
         + d_ref[...] * u.astype(F32))
    z = jax.nn.gelu(y)
    gate = jnp.dot(z.astype(BF16), gw_ref[...], preferred_element_type=F32) + gb_ref[...]
    return (z * jax.nn.sigmoid(gate)).astype(BF16)


def _s5_prompt_kernel(su0_ref, su1_ref, arow_ref, atile_ref, bre_ref, bim_ref, cre_ref, cim_ref, d_ref,
                      gw_ref, gb_ref, o0_ref, o1_ref, hre_ref, him_ref,
                      bbr_ref, bbi_ref, lam_ref, car_ref, bur_ref, bui_ref, hbr_ref, hbi_ref):
    i = pl.program_id(0)
    n_slab = N_STATE // LANES
    su_refs = (su0_ref, su1_ref)
    o_refs = (o0_ref, o1_ref)

    @pl.when(i == 0)
    def _():
        bbr, bbi = _s5_bbar(arow_ref, bre_ref, bim_ref)
        bbr_ref[...] = bbr
        bbi_ref[...] = bbi
        lr, li, _, _ = _s5_discretize(atile_ref[0], atile_ref[1], atile_ref[2])
        lam_ref[0] = lr
        lam_ref[1] = li
        car_ref[...] = jnp.zeros(car_ref.shape, F32)

    for s in range(2):
        u = su_refs[s][...]
        br = jnp.dot(u, bbr_ref[...], preferred_element_type=F32)
        bi = jnp.dot(u, bbi_ref[...], preferred_element_type=F32)
        for j in range(n_slab):
            bur_ref[s, j * SCAN_PITCH:j * SCAN_PITCH + SCAN_T, :] = br[:, LANES * j:LANES * (j + 1)]
            bui_ref[s, j * SCAN_PITCH:j * SCAN_PITCH + SCAN_T, :] = bi[:, LANES * j:LANES * (j + 1)]

    lr = lam_ref[0]
    li = lam_ref[1]

    def step(t, carry):
        new = []
        for s in range(2):
            hr, hi = carry[2 * s], carry[2 * s + 1]
            rows = pl.ds(t, n_slab, stride=SCAN_PITCH)
            nr = lr * hr - li * hi + bur_ref.at[s][rows, :]
            ni = lr * hi + li * hr + bui_ref.at[s][rows, :]
            hbr_ref.at[s][rows, :] = nr
            hbi_ref.at[s][rows, :] = ni
            new += [nr, ni]
        return tuple(new)

    carry = lax.fori_loop(0, SCAN_T, step, tuple(car_ref[k] for k in range(4)), unroll=8)
    for k in range(4):
        car_ref[k] = carry[k]

    @pl.when(i == pl.num_programs(0) - 1)
    def _():
        for s in range(2):
            hre_ref[s] = carry[2 * s]
            him_ref[s] = carry[2 * s + 1]

    for s in range(2):
        h_re = jnp.concatenate([hbr_ref[s, j * SCAN_PITCH:j * SCAN_PITCH + SCAN_T, :] for j in range(n_slab)], axis=1)
        h_im = jnp.concatenate([hbi_ref[s, j * SCAN_PITCH:j * SCAN_PITCH + SCAN_T, :] for j in range(n_slab)], axis=1)
        o_refs[s][...] = _s5_readout(h_re, h_im, su_refs[s][...], cre_ref, cim_ref, d_ref, gw_ref, gb_ref)


def _s5_prompt(su, prm, t):
    nt = t // SCAN_T
    n_slab = N_STATE // LANES
    blk0 = pl.BlockSpec((SCAN_T, D_SSM), lambda i: (i, 0))
    blk1 = pl.BlockSpec((SCAN_T, D_SSM), lambda i: (nt + i, 0))
    oblk = pl.BlockSpec((SCAN_T, D_SSM), lambda i: (i, 0))
    state = pl.BlockSpec((2, SUBLANES, LANES), lambda i: (0, 0, 0))
    slabs = pltpu.VMEM((2, n_slab * SCAN_PITCH, LANES), F32)
    return pl.pallas_call(
        _s5_prompt_kernel,
        grid=(nt,),
        in_specs=[blk0, blk1, _full((3, N_STATE)), _full((3, SUBLANES, LANES)),
                  _full((D_SSM, N_STATE)), _full((D_SSM, N_STATE)), _full((N_STATE, D_SSM)), _full((N_STATE, D_SSM)),
                  _full((1, D_SSM)), _full((D_SSM, D_SSM)), _full((1, D_SSM))],
        out_specs=[oblk, oblk, state, state],
        out_shape=[jax.ShapeDtypeStruct((t, D_SSM), BF16), jax.ShapeDtypeStruct((t, D_SSM), BF16),
                   jax.ShapeDtypeStruct((2, SUBLANES, LANES), F32), jax.ShapeDtypeStruct((2, SUBLANES, LANES), F32)],
        scratch_shapes=[pltpu.VMEM((D_SSM, N_STATE), BF16), pltpu.VMEM((D_SSM, N_STATE), BF16),
                        pltpu.VMEM((2, SUBLANES, LANES), F32), pltpu.VMEM((4, SUBLANES, LANES), F32),
                        slabs, slabs, slabs, slabs],
        compiler_params=_params(("arbitrary",)),
        name="s5_prompt",
    )(su, su, prm["arow"], prm["atile"], prm["bre"], prm["bim"], prm["cre"], prm["cim"],
      prm["d"], prm["gw"], prm["gb"])


def _s5_sample_kernel(su_ref, h0r_ref, h0i_ref, arow_ref, bre_ref, bim_ref, cre_ref, cim_ref, d_ref, gw_ref, gb_ref,
                      o_ref, hr_ref, hi_ref, hbr_ref, hbi_ref, *, s_new):
    n_seq = h0r_ref.shape[0]
    lr, li, _, _ = _s5_discretize(arow_ref[0:1, :], arow_ref[1:2, :], arow_ref[2:3, :])
    bbr, bbi = _s5_bbar(arow_ref, bre_ref, bim_ref)
    u = su_ref[...]
    hbr_ref[...] = jnp.dot(u, bbr, preferred_element_type=F32)
    hbi_ref[...] = jnp.dot(u, bbi, preferred_element_type=F32)
    hr_ref[...] = h0r_ref[...]
    hi_ref[...] = h0i_ref[...]
    for t in range(s_new):
        rows = slice(t * n_seq, (t + 1) * n_seq)
        hr = hr_ref[...]
        hi = hi_ref[...]
        nr = lr * hr - li * hi + hbr_ref[rows, :]
        ni = lr * hi + li * hr + hbi_ref[rows, :]
        hbr_ref[rows, :] = nr
        hbi_ref[rows, :] = ni
        hr_ref[...] = nr
        hi_ref[...] = ni
    o_ref[...] = _s5_readout(hbr_ref[...], hbi_ref[...], u, cre_ref, cim_ref, d_ref, gw_ref, gb_ref)


def _s5_sample(su_tb, h0r, h0i, prm, s_new):
    n_rows = su_tb.shape[0]
    n_seq = n_rows // s_new
    return pl.pallas_call(
        functools.partial(_s5_sample_kernel, s_new=s_new),
        out_shape=[jax.ShapeDtypeStruct((n_rows, D_SSM), BF16),
                   jax.ShapeDtypeStruct((n_seq, N_STATE), F32), jax.ShapeDtypeStruct((n_seq, N_STATE), F32)],
        scratch_shapes=[pltpu.VMEM((n_rows, N_STATE), F32), pltpu.VMEM((n_rows, N_STATE), F32)],
        compiler_params=pltpu.CompilerParams(vmem_limit_bytes=VMEM_LIMIT),
        name="s5_sample",
    )(su_tb, h0r, h0i, prm["arow"], prm["bre"], prm["bim"], prm["cre"], prm["cim"], prm["d"], prm["gw"], prm["gb"])


def _s5_params(p, l):
    eye = jnp.eye(SSM_GROUPS, dtype=F32)

    def b_diag(b):
        return jnp.einsum("gnc,gh->gchn", b, eye).reshape(D_SSM, N_STATE)

    def c_diag(c):
        return jnp.einsum("gcn,gh->gnhc", c, eye).reshape(N_STATE, D_SSM)

    ldt = jnp.broadcast_to(p["ssm_log_dt"][l][:, None], (SSM_GROUPS, SSM_STATE))
    a3 = jnp.stack([p["ssm_a_re"][l], p["ssm_a_im"][l], ldt])
    return {
        "arow": a3.reshape(3, N_STATE),
        "atile": a3.reshape(3, SUBLANES, LANES),
        "bre": b_diag(p["ssm_b_re"][l]), "bim": b_diag(p["ssm_b_im"][l]),
        "cre": c_diag(p["ssm_c_re"][l]).astype(BF16), "cim": c_diag(p["ssm_c_im"][l]).astype(BF16),
        "d": p["ssm_d"][l].reshape(1, D_SSM),
        "gw": p["ssm_glu_w"][l].astype(BF16), "gb": p["ssm_glu_b"][l].reshape(1, D_SSM),
    }


def _merge_out_kernel(*refs, starts):
    it = iter(refs)
    oa_refs, oc_refs, os_refs, x_refs = ([next(it) for _ in s] for s in starts)
    gn_ref, wo_ref, fg_ref, wrh_ref, wrl_ref, br_ref = (next(it) for _ in range(6))
    x1_ref, xn_ref, tokc_ref, tokl_ref, tile_ref, glob_ref = (next(it) for _ in range(6))
    carry_ref = next(it)
    i = pl.program_id(0)
    tm = TM_OUT

    @pl.when(i == 0)
    def _():
        carry_ref[...] = jnp.zeros(carry_ref.shape, F32)

    gn = gn_ref[...]
    lane = lax.broadcasted_iota(I32, (1, LANES), 1).astype(F32)
    far = float(LANES)
    is_group = lane < N_EXPERT_GROUPS
    r_i = lax.broadcasted_iota(I32, (tm, tm), 0)
    c_i = lax.broadcasted_iota(I32, (tm, tm), 1)
    lower = jnp.where(c_i < r_i, 1.0, 0.0).astype(BF16)
    lr = lax.broadcasted_iota(I32, (LANES, LANES), 0)
    lc = lax.broadcasted_iota(I32, (LANES, LANES), 1)
    before = jnp.where(lr < lc, 1.0, 0.0).astype(BF16)
    is_e_col = (lr >= ROUTER_LANE0) & (lr < ROUTER_LANE0 + N_EXPERTS)
    zrow = jnp.zeros((1, LANES), F32)
    row0 = lane * RUN

    def first_max(v):
        top = jnp.max(v, axis=-1, keepdims=True)
        return top, jnp.min(jnp.where(v == top, lane, far), axis=-1, keepdims=True)

    def at(sel, v):
        return jnp.sum(jnp.where(sel, v, 0.0), axis=-1, keepdims=True)

    oa_all = _pick(i, oa_refs, starts[0])
    oc_all = _pick(i, oc_refs, starts[1])
    os_all = _pick(i, os_refs, starts[2])
    x_all = _pick(i, x_refs, starts[3])
    carry = carry_ref[...]

    for sub in range(MERGE_SUB):
        rows_t = slice(sub * tm, (sub + 1) * tm)
        rows_8 = slice(sub * SUBLANES, (sub + 1) * SUBLANES)
        mix = jnp.concatenate([
            _rms(oa_all[rows_t], gn[:, :D_ATTN]),
            _rms(oc_all[rows_t], gn[:, D_ATTN:D_ATTN + D_CONV]),
            _rms(os_all[rows_t], gn[:, D_ATTN + D_CONV:]),
        ], axis=1).astype(BF16)
        x1 = x_all[rows_t] + jnp.dot(mix, wo_ref[...], preferred_element_type=F32)
        x1_ref[rows_t, :] = x1
        xn = _rms(x1, fg_ref[...])
        xn_ref[rows_t, :] = _pack_pair(xn[:, :D_PACK], xn[:, D_PACK:])

        xh = xn.astype(BF16)
        xl = (xn - xh.astype(F32)).astype(BF16)
        wh = wrh_ref[...]
        logits = (jnp.dot(xh, wh, preferred_element_type=F32) + jnp.dot(xl, wh, preferred_element_type=F32)
                  + jnp.dot(xh, wrl_ref[...], preferred_element_type=F32) + br_ref[...])

        g_top, g_idx = first_max(jnp.where(is_group, logits, -jnp.inf))
        g_w = 1.0 / jnp.sum(jnp.where(is_group, jnp.exp(logits - g_top), 0.0), axis=-1, keepdims=True)
        e_lo = ROUTER_LANE0 + EXPERTS_PER_GROUP * g_idx
        el = jnp.where((lane >= e_lo) & (lane < e_lo + EXPERTS_PER_GROUP), logits, -jnp.inf)
        v1, i1 = first_max(el)
        v2, i2 = first_max(jnp.where(lane == i1, -jnp.inf, el))
        t2 = jnp.exp(v2 - v1)
        w1 = g_w / (1.0 + t2)
        w2 = g_w * t2 / (1.0 + t2)

        sel1 = lane == i1
        sel2 = lane == i2
        onehot = jnp.where(sel1 | sel2, 1.0, 0.0)
        rloc = jnp.dot(lower, onehot.astype(BF16), preferred_element_type=F32)
        cnt = jnp.sum(onehot, axis=0, keepdims=True)
        cnt_pad = jnp.floor((cnt + (RUN - 1.0)) * (1.0 / RUN)) * RUN
        stacked = jnp.concatenate([cnt_pad] + [zrow] * (SUBLANES - 1), axis=0).astype(BF16)
        lstart = jnp.dot(stacked, before, preferred_element_type=F32)[0:1, :]

        pos1, pos2 = at(sel1, lstart + rloc), at(sel2, lstart + rloc)
        tok = jnp.zeros((tm, LANES), F32)
        for col, val in ((TOK_W1, w1), (TOK_W2, w2), (TOK_P1, pos1), (TOK_P2, pos2)):
            tok = jnp.where(lane == col, val, tok)
        tokc_ref[rows_t, :] = tok[:, :tokc_ref.shape[1]]
        tokl_ref[rows_8, :] = jnp.transpose(tok)[:SUBLANES, :]

        per_expert = jnp.concatenate([lstart, lstart + cnt_pad, carry - lstart,
                                      jnp.zeros((LANES - 3, LANES), F32)], axis=0)
        cols = jnp.transpose(per_expert)
        own = is_e_col & (cols[:, 0:1] <= row0) & (row0 < cols[:, 1:2])
        chunk_e = jnp.sum(jnp.where(own, lr.astype(F32), 0.0), axis=0, keepdims=True)
        chunk_rel = row0 + jnp.sum(jnp.where(own, cols[:, 2:3], 0.0), axis=0, keepdims=True)
        n_chunk = jnp.sum(cnt_pad, axis=-1, keepdims=True) * (1.0 / RUN)
        rows = [zrow] * SUBLANES
        rows[TILE_NCHUNK] = jnp.broadcast_to(n_chunk, (1, LANES))
        rows[TILE_CHUNK_E], rows[TILE_CHUNK_REL] = chunk_e, chunk_rel
        tile_ref[rows_8, :] = jnp.concatenate(rows, axis=0).astype(I32)
        carry = carry + cnt_pad

    total = carry
    carry_ref[...] = total

    @pl.when(i == pl.num_programs(0) - 1)
    def _():
        is_e = (lane >= ROUTER_LANE0) & (lane < ROUTER_LANE0 + N_EXPERTS)
        ptiles = jnp.floor((total + (TM_MOE - 1.0)) * (1.0 / TM_MOE))
        upto = jnp.where(lr <= lc, 1.0, 0.0).astype(BF16)
        pt8 = jnp.concatenate([ptiles, jnp.zeros((SUBLANES - 1, LANES), F32)], axis=0).astype(BF16)
        tend = jnp.dot(pt8, upto, preferred_element_type=F32)[0:1, :]
        n_used = jnp.max(tend, axis=-1, keepdims=True)
        e_last = jnp.max(jnp.where(ptiles > 0.0, lane - ROUTER_LANE0, -1.0), axis=-1, keepdims=True)
        tend_col = jnp.transpose(jnp.broadcast_to(tend, (LANES, LANES)))

        def tile_expert(first_tile):
            hit = is_e_col & (tend_col <= lane + first_tile)
            return jnp.minimum(jnp.sum(jnp.where(hit, 1.0, 0.0), axis=0, keepdims=True), e_last)

        rows = [zrow] * SUBLANES
        rows[GLOB_START] = (tend - ptiles) * TM_MOE
        rows[GLOB_ZERO] = jnp.where(is_e & (total > 0.0), tend - 1.0, -1.0)
        rows[GLOB_TE] = tile_expert(0.0)
        rows[GLOB_TE_HI] = tile_expert(float(LANES))
        rows[GLOB_NUSED] = jnp.broadcast_to(n_used, (1, LANES))
        rows[GLOB_PTILES] = ptiles
        glob_ref[...] = jnp.concatenate(rows, axis=0).astype(I32)


def _merge_out(oa_src, oc_src, os_src, x_src, gn, wo_bf, fg, wr_hi, wr_lo, br):
    specs, starts, n_blk = [], [], None
    for src, width in ((oa_src, D_ATTN), (oc_src, D_CONV), (os_src, D_SSM), (x_src, D_MODEL)):
        sp, st, nb = _source_specs(src, MERGE_SUB * TM_OUT, width)
        assert n_blk in (None, nb)
        n_blk = nb
        specs += sp
        starts.append(st)
    n = n_blk * MERGE_SUB * TM_OUT
    row = lambda width: pl.BlockSpec((MERGE_SUB * TM_OUT, width), lambda i: (i, 0))
    tbl = lambda width: pl.BlockSpec((MERGE_SUB * SUBLANES, width), lambda i: (i, 0))
    return pl.pallas_call(
        functools.partial(_merge_out_kernel, starts=tuple(starts)),
        grid=(n_blk,),
        in_specs=specs + [_full((1, D_MODEL)), _full((D_MODEL, D_MODEL)), _full((1, D_MODEL)),
                          _full((D_MODEL, LANES)), _full((D_MODEL, LANES)), _full((1, LANES))],
        out_specs=[row(D_MODEL), row(D_PACK), row(4), tbl(TM_OUT), tbl(LANES), _full((SUBLANES, LANES))],
        out_shape=[jax.ShapeDtypeStruct((n, D_MODEL), F32), jax.ShapeDtypeStruct((n, D_PACK), U32),
                   jax.ShapeDtypeStruct((n, 4), F32),
                   jax.ShapeDtypeStruct((n_blk * MERGE_SUB * SUBLANES, TM_OUT), F32),
                   jax.ShapeDtypeStruct((n_blk * MERGE_SUB * SUBLANES, LANES), I32),
                   jax.ShapeDtypeStruct((SUBLANES, LANES), I32)],
        scratch_shapes=[pltpu.VMEM((1, LANES), F32)],
        compiler_params=_params(("arbitrary",)),
        name="merge_out",
    )(*oa_src, *oc_src, *os_src, *x_src, gn.reshape(1, D_MODEL), wo_bf, fg.reshape(1, D_MODEL), wr_hi, wr_lo, br)


def _chunk_groups(tile_ref):
    n_chunk = tile_ref[TILE_NCHUNK, 0]
    return lax.shift_right_logical(n_chunk + (CHUNK_GROUP - 1), CHUNK_GROUP.bit_length() - 1)


def _for_each_chunk(tile_ref, glob_ref, fn):
    n_chunk = tile_ref[TILE_NCHUNK, 0]
    n_group = _chunk_groups(tile_ref)

    def group(g, carry):
        for k in range(CHUNK_GROUP):
            c = g * CHUNK_GROUP + k
            seg = glob_ref[GLOB_START * LANES + tile_ref[TILE_CHUNK_E, c]]
            fn(pl.multiple_of(c * RUN, RUN), c < n_chunk, seg + tile_ref[TILE_CHUNK_REL, c], k)
        return carry

    lax.fori_loop(0, n_group, group, 0)
    return n_group * CHUNK_GROUP


def _wait_chunks(n, wait_one):
    def group(g, carry):
        for _ in range(CHUNK_GROUP):
            wait_one()
        return carry

    lax.fori_loop(0, lax.shift_right_logical(n, CHUNK_GROUP.bit_length() - 1), group, 0)


def _dispatch_kernel(tile_ref, glob_ref, tokl_ref, xn_ref, xs_ref, sbuf_ref, zbuf_ref, inflight_ref, sem_z, sem_r):
    i = pl.program_id(0)
    last = pl.num_programs(0) - 1
    slot = lax.rem(i, 2)
    trash0 = xs_ref.shape[0] - TRASH_ROWS

    def zero_copy(t):
        return pltpu.make_async_copy(zbuf_ref, xs_ref.at[pl.ds(pl.multiple_of(t * TM_MOE, TM_MOE), TM_MOE)], sem_z)

    def for_zero_tiles(fn):
        def seg_last(k, c):
            t = glob_ref[GLOB_ZERO * LANES + k]

            @pl.when(t >= 0)
            def _():
                fn(t)
            return c

        def unused(t, c):
            fn(t)
            return c

        lax.fori_loop(0, LANES, seg_last, 0)
        lax.fori_loop(glob_ref[GLOB_NUSED * LANES], trash0 // TM_MOE, unused, 0)

    @pl.when(i == 0)
    def _():
        zbuf_ref[...] = jnp.zeros(zbuf_ref.shape, U32)
        trash = pltpu.make_async_copy(zbuf_ref.at[pl.ds(0, TRASH_ROWS)], xs_ref.at[pl.ds(trash0, TRASH_ROWS)], sem_z)
        trash.start()
        for_zero_tiles(lambda t: zero_copy(t).start())
        for_zero_tiles(lambda t: zero_copy(t).wait())
        trash.wait()
        inflight_ref[0] = 0

    q = lax.broadcasted_iota(I32, (N_LOCAL, 1), 0).astype(F32)
    hit = (q == tokl_ref[TOK_P1:TOK_P1 + 1, :]) | (q == tokl_ref[TOK_P2:TOK_P2 + 1, :])
    sel = jnp.where(hit, 1.0, 0.0).astype(BF16)
    a, b = _unpack_pair(xn_ref[...])
    sa = jnp.dot(sel, a.astype(BF16), preferred_element_type=F32)
    sb = jnp.dot(sel, b.astype(BF16), preferred_element_type=F32)
    sbuf_ref[slot] = _pack_pair(sa, sb)

    def start_chunk(local_row, real, sorted_row, k):
        dst = pl.multiple_of(jnp.where(real, sorted_row, trash0 + k * RUN), RUN)
        pltpu.make_async_copy(sbuf_ref.at[slot, pl.ds(local_row, RUN)], xs_ref.at[pl.ds(dst, RUN)], sem_r).start()

    def drain(n):
        _wait_chunks(n, lambda: pltpu.make_async_copy(sbuf_ref.at[0, pl.ds(0, RUN)], xs_ref.at[pl.ds(0, RUN)],
                                                      sem_r).wait())

    drain(inflight_ref[0])
    n_issued = _for_each_chunk(tile_ref, glob_ref, start_chunk)
    inflight_ref[0] = n_issued

    @pl.when(i == last)
    def _():
        drain(n_issued)


def _dispatch(tile_tbl, glob_flat, tokl, xn, n_rows_sorted):
    n = xn.shape[0]
    return pl.pallas_call(
        _dispatch_kernel,
        grid=(n // TM_OUT,),
        in_specs=[pl.BlockSpec((SUBLANES, LANES), lambda i: (i, 0), memory_space=pltpu.SMEM),
                  pl.BlockSpec(memory_space=pltpu.SMEM),
                  pl.BlockSpec((SUBLANES, TM_OUT), lambda i: (i, 0)),
                  pl.BlockSpec((TM_OUT, D_PACK), lambda i: (i, 0))],
        out_specs=pl.BlockSpec(memory_space=pl.ANY),
        out_shape=jax.ShapeDtypeStruct((n_rows_sorted, D_PACK), U32),
        scratch_shapes=[pltpu.VMEM((2, N_LOCAL, D_PACK), U32), pltpu.VMEM((TM_MOE, D_PACK), U32),
                        pltpu.SMEM((1,), I32), pltpu.SemaphoreType.DMA(()), pltpu.SemaphoreType.DMA(())],
        compiler_params=_params(("arbitrary",)),
        name="moe_dispatch",
    )(tile_tbl, glob_flat, tokl, xn)


def _tile_expert(glob_ref, i):
    return glob_ref[GLOB_TE * LANES + i]


def _moe_kernel(glob_ref, xs_ref, wg_hbm, wu_hbm, wd_hbm, ys_ref, wgb_ref, wub_ref, wdb_ref,
                sg_ref, su_ref, sd_ref, ord_ref, sem, *, layer):
    i = pl.program_id(0)
    used = i < glob_ref[GLOB_NUSED * LANES]
    expert = _tile_expert(glob_ref, i)
    new_expert = (i == 0) | (expert != _tile_expert(glob_ref, jnp.maximum(i - 1, 0)))

    def weight_copies(e, slot):
        return [pltpu.make_async_copy(src.at[layer, e], dst.at[slot], sem.at[slot])
                for src, dst in ((wg_hbm, sg_ref), (wu_hbm, su_ref), (wd_hbm, sd_ref))]

    @pl.when(i == 0)
    def _():
        ord_ref[0] = 0
        for cp in weight_copies(expert, 0):
            cp.start()

    @pl.when(used & new_expert)
    def _():
        slot = lax.rem(ord_ref[0], 2)
        for cp in weight_copies(expert, slot):
            cp.wait()
        nxt = lax.while_loop(
            lambda k: (k < N_EXPERTS) & (glob_ref[GLOB_PTILES * LANES + ROUTER_LANE0 + jnp.minimum(k, N_EXPERTS - 1)] == 0),
            lambda k: k + 1, expert + 1)

        @pl.when(nxt < N_EXPERTS)
        def _():
            for cp in weight_copies(nxt, 1 - slot):
                cp.start()

        wgb_ref[...] = sg_ref[slot].astype(BF16)
        wub_ref[...] = su_ref[slot].astype(BF16)
        wdb_ref[...] = sd_ref[slot].astype(BF16)
        ord_ref[0] = ord_ref[0] + 1

    @pl.when(used)
    def _():
        a, b = _unpack_pair(xs_ref[...])
        x = jnp.concatenate([a, b], axis=1).astype(BF16)
        gate = jnp.dot(x, wgb_ref[...], preferred_element_type=F32)
        up = jnp.dot(x, wub_ref[...], preferred_element_type=F32)
        h = (gate * jax.nn.sigmoid(gate) * up).astype(BF16)
        y = jnp.dot(h, wdb_ref[...], preferred_element_type=F32)
        ys_ref[...] = _pack_pair(y[:, :D_PACK], y[:, D_PACK:])

    @pl.when(jnp.logical_not(used))
    def _():
        zero = jnp.zeros(ys_ref.shape, F32)
        ys_ref[...] = _pack_pair(zero, zero)


def _moe(glob_flat, xs, wg, wu, wd, layer):
    n_tiles = (xs.shape[0] - TRASH_ROWS) // TM_MOE
    assert n_tiles <= 2 * LANES
    hbm = pl.BlockSpec(memory_space=pl.ANY)
    grid_spec = pltpu.PrefetchScalarGridSpec(
        num_scalar_prefetch=1,
        grid=(n_tiles,),
        in_specs=[pl.BlockSpec((TM_MOE, D_PACK), lambda i, g: (jnp.minimum(i, g[GLOB_NUSED * LANES] - 1), 0)),
                  hbm, hbm, hbm],
        out_specs=pl.BlockSpec((TM_MOE, D_PACK), lambda i, g: (i, 0)),
        scratch_shapes=[pltpu.VMEM((D_MODEL, D_EXPERT), BF16), pltpu.VMEM((D_MODEL, D_EXPERT), BF16),
                        pltpu.VMEM((D_EXPERT, D_MODEL), BF16),
                        pltpu.VMEM((2, D_MODEL, D_EXPERT), F32), pltpu.VMEM((2, D_MODEL, D_EXPERT), F32),
                        pltpu.VMEM((2, D_EXPERT, D_MODEL), F32), pltpu.SMEM((1,), I32),
                        pltpu.SemaphoreType.DMA((2,))],
    )
    return pl.pallas_call(
        functools.partial(_moe_kernel, layer=layer),
        grid_spec=grid_spec,
        out_shape=jax.ShapeDtypeStruct((n_tiles * TM_MOE, D_PACK), U32),
        compiler_params=_params(("arbitrary",)),
        name="moe_experts",
    )(glob_flat, xs, wg, wu, wd)


def _combine_kernel(tile_ref, next_ref, glob_ref, tokc_ref, x1_ref, ys_ref, fg_ref, *rest, n_first, final):
    n_out = 2 if final else 1
    out_refs = rest[:n_out]
    lbuf_ref, inflight_ref, sem = rest[n_out:]
    i = pl.program_id(0)
    last = pl.num_programs(0) - 1
    slot = lax.rem(i, 2)

    def gather(tbl_ref, dst_slot):
        def start(local_row, real, sorted_row, k):
            src = pl.multiple_of(jnp.where(real, sorted_row, 0), RUN)
            pltpu.make_async_copy(ys_ref.at[pl.ds(src, RUN)], lbuf_ref.at[dst_slot, pl.ds(local_row, RUN)],
                                  sem.at[dst_slot]).start()

        inflight_ref[dst_slot] = _for_each_chunk(tbl_ref, glob_ref, start)

    @pl.when(i == 0)
    def _():
        lbuf_ref[...] = jnp.zeros(lbuf_ref.shape, U32)
        gather(tile_ref, 0)

    @pl.when(i < last)
    def _():
        gather(next_ref, 1 - slot)

    _wait_chunks(inflight_ref[slot], lambda: pltpu.make_async_copy(
        ys_ref.at[pl.ds(0, RUN)], lbuf_ref.at[slot, pl.ds(0, RUN)], sem.at[slot]).wait())

    tokc = tokc_ref[...]
    w1 = tokc[:, TOK_W1:TOK_W1 + 1]
    w2 = tokc[:, TOK_W2:TOK_W2 + 1]
    col = lax.broadcasted_iota(I32, (1, N_LOCAL), 1).astype(F32)
    sel1 = jnp.where(col == tokc[:, TOK_P1:TOK_P1 + 1], 1.0, 0.0).astype(BF16)
    sel2 = jnp.where(col == tokc[:, TOK_P2:TOK_P2 + 1], 1.0, 0.0).astype(BF16)
    halves = []
    for part in _unpack_pair(lbuf_ref[slot]):
        rows = part.astype(BF16)
        halves.append(w1 * jnp.dot(sel1, rows, preferred_element_type=F32)
                      + w2 * jnp.dot(sel2, rows, preferred_element_type=F32))
    x2 = x1_ref[...] + jnp.concatenate(halves, axis=1)
    if not final:
        out_refs[0][...] = x2
    else:
        y = _rms(x2, fg_ref[...])

        @pl.when(i < n_first)
        def _():
            out_refs[0][...] = y

        @pl.when(i >= n_first)
        def _():
            out_refs[1][...] = y


def _combine(tile_tbl, glob_flat, tokc, x1, ys, fg, n_first_rows, final):
    n = x1.shape[0]
    n_blk = n // TM_OUT
    n_first = n_first_rows // TM_OUT
    assert n_first_rows % TM_OUT == 0
    row = lambda width: pl.BlockSpec((TM_OUT, width), lambda i: (i, 0))
    if final:
        out_specs = [pl.BlockSpec((TM_OUT, D_MODEL), lambda i: (jnp.minimum(i, n_first - 1), 0)),
                     pl.BlockSpec((TM_OUT, D_MODEL), lambda i: (jnp.maximum(i - n_first, 0), 0))]
        out_shape = [jax.ShapeDtypeStruct((n_first_rows, D_MODEL), F32),
                     jax.ShapeDtypeStruct((n - n_first_rows, D_MODEL), F32)]
    else:
        out_specs = [row(D_MODEL)]
        out_shape = [jax.ShapeDtypeStruct((n, D_MODEL), F32)]
    return pl.pallas_call(
        functools.partial(_combine_kernel, n_first=n_first, final=final),
        grid=(n_blk,),
        in_specs=[pl.BlockSpec((SUBLANES, LANES), lambda i: (i, 0), memory_space=pltpu.SMEM),
                  pl.BlockSpec((SUBLANES, LANES), lambda i: (jnp.minimum(i + 1, n_blk - 1), 0), memory_space=pltpu.SMEM),
                  pl.BlockSpec(memory_space=pltpu.SMEM),
                  row(4), row(D_MODEL), pl.BlockSpec(memory_space=pl.ANY), _full((1, D_MODEL))],
        out_specs=out_specs,
        out_shape=out_shape,
        scratch_shapes=[pltpu.VMEM((2, N_LOCAL, D_PACK), U32), pltpu.SMEM((2,), I32),
                        pltpu.SemaphoreType.DMA((2,))],
        compiler_params=_params(("arbitrary",)),
        name="moe_combine",
    )(tile_tbl, tile_tbl, glob_flat, tokc, x1, ys, fg.reshape(1, D_MODEL))


def _n_moe_tiles(n_tokens):
    n_runs = (n_tokens // TM_OUT) * N_EXPERTS
    return (2 * n_tokens + n_runs * (RUN - 1) + N_EXPERTS * (TM_MOE - 1)) // TM_MOE + 1


def kernel(x_prompt, x_sample, cache_k, cache_v, state_conv, state_ssm_re, state_ssm_im, attn_norm_g, w_in, attn_sinks, conv_w, conv_b, conv_ln_g, conv_ln_b, ssm_a_re, ssm_a_im, ssm_log_dt, ssm_b_re, ssm_b_im, ssm_c_re, ssm_c_im, ssm_d, ssm_glu_w, ssm_glu_b, grp_norm_g, w_out, ffn_norm_g, w_group_router, b_group_router, w_expert_router, b_expert_router, w_gate, w_up, w_down, final_norm_g):
    p = dict(ssm_a_re=ssm_a_re, ssm_a_im=ssm_a_im, ssm_log_dt=ssm_log_dt, ssm_b_re=ssm_b_re, ssm_b_im=ssm_b_im,
             ssm_c_re=ssm_c_re, ssm_c_im=ssm_c_im, ssm_d=ssm_d, ssm_glu_w=ssm_glu_w, ssm_glu_b=ssm_glu_b)
    depth = w_in.shape[0]
    n_seq, t, _ = x_prompt.shape
    n_dec, s_new, _ = x_sample.shape
    win = cache_k.shape[2]
    n_p = n_seq * t
    n_s = n_dec * s_new
    n = n_p + n_s
    assert n_seq == 2 and t % SCAN_T == 0 and t % CONV_T == 0 and n_p % TM_IN == 0 and n_s % TM_IN == 0
    hist = CONV_WIDTH - 1
    n_tiles = _n_moe_tiles(n)

    x_src = [x_prompt.reshape(n_p, D_MODEL), x_sample.reshape(n_s, D_MODEL)]
    outs = {k: [] for k in ("kp", "vp", "cp", "rp", "ip", "kv_new", "u_new", "rs", "is")}
    ck_all = cache_k.reshape(depth * n_dec, win, D_KV)
    cv_all = cache_v.reshape(depth * n_dec, win, D_KV)
    conv_all = state_conv.reshape(depth * n_dec, hist * D_CONV)
    for l in range(depth):
        q, kv, u, su = _in_proj(x_src, attn_norm_g[l], w_in[l].astype(BF16))

        oa_p = _attn_prompt(q, kv, attn_sinks[l], n_seq, t)
        oa_s = _attn_sample(q, kv, ck_all, cv_all, attn_sinks[l], s_new, n_p, n_dec, l * n_dec)
        kv_p = kv[:n_p].reshape(n_seq, t, 2 * D_KV)[:, t - win:].astype(F32)
        outs["kp"].append(kv_p[..., :D_KV].reshape(n_seq, win, N_KV_HEADS, HEAD_DIM))
        outs["vp"].append(kv_p[..., D_KV:].reshape(n_seq, win, N_KV_HEADS, HEAD_DIM))
        outs["kv_new"].append(kv[n_p:].reshape(n_dec, s_new, 2 * D_KV))

        oc_p = _conv_prompt(u, conv_w[l], conv_b[l], conv_ln_g[l], conv_ln_b[l], n_seq, t)
        u_s = u[n_p:].reshape(n_dec, s_new, D_CONV)
        oc_s = _conv_sample(conv_all, u_s.reshape(n_dec, s_new * D_CONV),
                            conv_w[l], conv_b[l], conv_ln_g[l], conv_ln_b[l], s_new, l)
        outs["cp"].append(u[:n_p].reshape(n_seq, t, D_CONV)[:, t - hist:])
        outs["u_new"].append(u_s)

        sp = _s5_params(p, l)
        os0, os1, hre, him = _s5_prompt(su, sp, t)
        su_tb = su[n_p:].reshape(n_dec, s_new, D_SSM).transpose(1, 0, 2).reshape(n_s, D_SSM)
        os_tb, hr_s, hi_s = _s5_sample(su_tb, state_ssm_re[l].reshape(n_dec, N_STATE),
                                       state_ssm_im[l].reshape(n_dec, N_STATE), sp, s_new)
        os_s = os_tb.reshape(s_new, n_dec, D_SSM).transpose(1, 0, 2).reshape(n_s, D_SSM)
        outs["rp"].append(hre.reshape(n_seq, SSM_GROUPS, SSM_STATE))
        outs["ip"].append(him.reshape(n_seq, SSM_GROUPS, SSM_STATE))
        outs["rs"].append(hr_s.reshape(n_dec, SSM_GROUPS, SSM_STATE))
        outs["is"].append(hi_s.reshape(n_dec, SSM_GROUPS, SSM_STATE))

        unused = LANES - N_EXPERT_GROUPS - N_EXPERTS
        wr = jnp.pad(jnp.concatenate([w_group_router[l], w_expert_router[l]], axis=1), ((0, 0), (0, unused)))
        wr_hi = wr.astype(BF16)
        wr_lo = (wr - wr_hi.astype(F32)).astype(BF16)
        br = jnp.pad(jnp.concatenate([b_group_router[l], b_expert_router[l]]), (0, unused)).reshape(1, LANES)
        x1, xn, tokc, tokl, tile_tbl, glob = _merge_out(
            [oa_p, oa_s], [oc_p, oc_s.reshape(n_s, D_CONV)], [os0, os1, os_s], x_src,
            grp_norm_g[l], w_out[l].astype(BF16), ffn_norm_g[l], wr_hi, wr_lo, br)

        glob_flat = glob.reshape(SUBLANES * LANES)
        xs = _dispatch(tile_tbl, glob_flat, tokl, xn, n_tiles * TM_MOE + TRASH_ROWS)
        ys = _moe(glob_flat, xs, w_gate, w_up, w_down, l)
        res = _combine(tile_tbl, glob_flat, tokc, x1, ys, final_norm_g, n_p, final=(l == depth - 1))
        x_src = [res[0]]

    y_p = res[0].reshape(n_seq, t, D_MODEL)
    y_s = res[1].reshape(n_dec, s_new, D_MODEL)
    st = lambda k: jnp.stack(outs[k])
    kv_new = st("kv_new").astype(F32)
    heads = lambda z: z.reshape(depth, n_dec, s_new, N_KV_HEADS, HEAD_DIM)
    k_s = jnp.concatenate([cache_k[:, :, s_new:], heads(kv_new[..., :D_KV])], axis=2)
    v_s = jnp.concatenate([cache_v[:, :, s_new:], heads(kv_new[..., D_KV:])], axis=2)
    conv_s = jnp.concatenate([state_conv[:, :, s_new:], st("u_new")], axis=2)
    return (y_p, y_s, st("kp"), st("vp"), st("cp"), st("rp"), st("ip"), k_s, v_s, conv_s, st("rs"), st("is"))
```

```python
import functools

import jax
import jax.numpy as jnp
from jax import lax
from jax.experimental import pallas as pl
from jax.experimental.pallas import tpu as pltpu

F32 = jnp.float32
BF16 = jnp.bfloat16
U32 = jnp.uint32
I32 = jnp.int32

D_MODEL = 1024
N_HEADS = 8
N_KV_HEADS = 2
HEAD_DIM = 64
WINDOW = 128
D_ATTN = N_HEADS * HEAD_DIM
D_KV = N_KV_HEADS * HEAD_DIM
D_CONV = 256
CONV_WIDTH = 31
D_SSM = 256
SSM_GROUPS = 16
SSM_GROUP_CH = 16
SSM_STATE = 64
N_STATE = SSM_GROUPS * SSM_STATE
D_IN = D_ATTN + 2 * D_KV + 2 * D_CONV + D_SSM
N_EXPERT_GROUPS = 4
EXPERTS_PER_GROUP = 8
N_EXPERTS = N_EXPERT_GROUPS * EXPERTS_PER_GROUP
D_EXPERT = 512
EPS = 1e-6
NEG_INF = -1e30
SCALE = HEAD_DIM ** -0.5

LANES = 128
SUBLANES = 8
HALF = LANES // 2
D_PACK = D_MODEL // 2

TM_IN = 512
TM_OUT = 256
TM_MOE = 512
CONV_T = 256
CONV_CHUNK = 64
CONV_HALO = 32
SCAN_T = 256
SCAN_PITCH = SCAN_T + SUBLANES
SAMPLE_BT = 16
ATTN_QB = 4
MERGE_SUB = 2
VMEM_LIMIT = 48 * 1024 * 1024

ROUTER_LANE0 = N_EXPERT_GROUPS
RUN = SUBLANES
N_LOCAL = 2 * TM_OUT + N_EXPERTS * RUN
TOK_W1, TOK_W2, TOK_P1, TOK_P2 = range(4)
TILE_NCHUNK, TILE_CHUNK_E, TILE_CHUNK_REL = range(3)
CHUNK_GROUP = 8
TRASH_ROWS = CHUNK_GROUP * RUN
GLOB_START, GLOB_ZERO, GLOB_TE, GLOB_TE_HI, GLOB_NUSED, GLOB_PTILES = range(6)


def _params(sem, vmem=VMEM_LIMIT):
    return pltpu.CompilerParams(dimension_semantics=sem, vmem_limit_bytes=vmem)


def _full(shape):
    zeros = (0,) * len(shape)
    return pl.BlockSpec(shape, lambda *_: zeros)


def _swap_halves(x):
    return jnp.concatenate([x[:, HALF:], x[:, :HALF]], axis=1)


def _rms(x, g):
    return x * lax.rsqrt(jnp.mean(x * x, axis=-1, keepdims=True) + EPS) * g


def _pack_pair(a, b):
    return pltpu.pack_elementwise([a, b], packed_dtype=BF16)


def _unpack_pair(p):
    return tuple(pltpu.unpack_elementwise(p, index=k, packed_dtype=BF16, unpacked_dtype=F32) for k in range(2))


def _pick(i, refs, starts):
    val = refs[0][...].astype(F32)
    for ref, start in zip(refs[1:], starts[1:]):
        val = jnp.where(i >= start, ref[...].astype(F32), val)
    return val


def _source_specs(sources, tm, width):
    specs, starts, start = [], [], 0
    for arr in sources:
        assert arr.shape[0] % tm == 0 and arr.shape[1] == width
        n_blk = arr.shape[0] // tm
        specs.append(pl.BlockSpec((tm, width), lambda i, s=start, nb=n_blk: (jnp.clip(i - s, 0, nb - 1), 0)))
        starts.append(start)
        start += n_blk
    return specs, tuple(starts), start


def _in_proj_kernel(*refs, starts):
    x_refs = refs[:len(starts)]
    g_ref, w_ref, q_ref, kv_ref, u_ref, su_ref, wbf_ref = refs[len(starts):]

    @pl.when(pl.program_id(0) == 0)
    def _():
        wbf_ref[...] = w_ref[...].astype(BF16)

    hn = _rms(_pick(pl.program_id(0), x_refs, starts), g_ref[...]).astype(BF16)
    z = jnp.dot(hn, wbf_ref[...], preferred_element_type=F32)
    q_ref[...] = z[:, :D_ATTN].astype(BF16)
    kv_ref[...] = z[:, D_ATTN:D_ATTN + 2 * D_KV].astype(BF16)
    c0 = D_ATTN + 2 * D_KV
    u_ref[...] = z[:, c0:c0 + D_CONV] * jax.nn.sigmoid(z[:, c0 + D_CONV:c0 + 2 * D_CONV])
    su_ref[...] = z[:, c0 + 2 * D_CONV:].astype(BF16)


def _layer_block(shape, layer):
    zeros = (0,) * len(shape)
    return pl.BlockSpec((None,) + tuple(shape), lambda *_: (layer,) + zeros)


def _in_proj(x_sources, g, w, layer):
    x_specs, starts, n_blk = _source_specs(x_sources, TM_IN, D_MODEL)
    n = n_blk * TM_IN
    row = lambda width: pl.BlockSpec((TM_IN, width), lambda i: (i, 0))
    return pl.pallas_call(
        functools.partial(_in_proj_kernel, starts=starts),
        grid=(n_blk,),
        in_specs=x_specs + [_full((1, D_MODEL)), _layer_block((D_MODEL, D_IN), layer)],
        out_specs=[row(D_ATTN), row(2 * D_KV), row(D_CONV), row(D_SSM)],
        out_shape=[jax.ShapeDtypeStruct((n, D_ATTN), BF16), jax.ShapeDtypeStruct((n, 2 * D_KV), BF16),
                   jax.ShapeDtypeStruct((n, D_CONV), F32), jax.ShapeDtypeStruct((n, D_SSM), BF16)],
        scratch_shapes=[pltpu.VMEM((D_MODEL, D_IN), BF16)],
        compiler_params=_params(("arbitrary",)),
        name="in_proj",
    )(*x_sources, g.reshape(1, D_MODEL), w)


def _softmax_pv(s, mask, sink, vmat):
    s = jnp.where(mask, s, NEG_INF)
    m = jnp.maximum(jnp.max(s, axis=-1, keepdims=True), sink)
    p = jnp.exp(s - m)
    denom = jnp.sum(p, axis=-1, keepdims=True) + jnp.exp(sink - m)
    return jnp.dot(p.astype(BF16), vmat, preferred_element_type=F32) / denom


def _attn_prompt_kernel(sink_ref, q_ref, kvc_ref, kvp_ref, o_ref):
    i = pl.program_id(1)
    lo = lax.broadcasted_iota(I32, (1, LANES), 1) < HALF
    a = lax.broadcasted_iota(I32, (WINDOW, 2 * WINDOW), 0)
    c = lax.broadcasted_iota(I32, (WINDOW, 2 * WINDOW), 1)
    diff = a + WINDOW - c
    band = (diff >= 0) & (diff < WINDOW)
    zero = jnp.zeros((WINDOW, LANES), BF16)
    for sub in range(ATTN_QB):
        rows = slice(sub * WINDOW, (sub + 1) * WINDOW)
        q = q_ref[rows, :] * jnp.asarray(SCALE, BF16)
        kvc = kvc_ref[rows, :]
        kvp = kvp_ref[...] if sub == 0 else kvc_ref[(sub - 1) * WINDOW:sub * WINDOW, :]
        mask = band & ((c >= WINDOW) | (i > 0)) if sub == 0 else band
        kk = jnp.concatenate([kvp[:, :LANES], kvc[:, :LANES]], axis=0)
        vv = jnp.concatenate([kvp[:, LANES:], kvc[:, LANES:]], axis=0)
        kk_sw = _swap_halves(kk)
        vv_sw = _swap_halves(vv)
        for j in range(D_ATTN // LANES):
            kvh = (2 * j) // (N_HEADS // N_KV_HEADS)
            qt = q[:, LANES * j:LANES * (j + 1)]
            mats = ((kk, vv), (kk_sw, vv_sw)) if kvh == 0 else ((kk_sw, vv_sw), (kk, vv))
            outs = []
            for par in range(2):
                kmat, vmat = mats[par]
                qm = jnp.where(lo if par == 0 else jnp.logical_not(lo), qt, zero)
                s = lax.dot_general(qm, kmat, (((1,), (1,)), ((), ())), preferred_element_type=F32)
                outs.append(_softmax_pv(s, mask, sink_ref[2 * j + par], vmat))
            o_ref[rows, LANES * j:LANES * (j + 1)] = jnp.where(lo, outs[0], outs[1]).astype(BF16)


def _attn_prompt(q, kv, sinks, n_seq, t):
    rows = ATTN_QB * WINDOW
    assert t % rows == 0
    nb = t // rows
    cur = lambda width: pl.BlockSpec((rows, width), lambda b, i: (b * nb + i, 0))
    prev = pl.BlockSpec((WINDOW, 2 * D_KV), lambda b, i: ((b * nb + i) * ATTN_QB - jnp.minimum(i, 1), 0))
    return pl.pallas_call(
        _attn_prompt_kernel,
        grid=(n_seq, nb),
        in_specs=[pl.BlockSpec(memory_space=pltpu.SMEM), cur(D_ATTN), cur(2 * D_KV), prev],
        out_specs=cur(D_ATTN),
        out_shape=jax.ShapeDtypeStruct((n_seq * t, D_ATTN), BF16),
        compiler_params=_params(("arbitrary", "arbitrary")),
        name="attn_prompt",
    )(sinks, q, kv, kv)


def _attn_sample_kernel(sink_ref, q_ref, kv_ref, ck_ref, cv_ref, o_ref, qf_ref, kvf_ref, *, s_new):
    rows = SUBLANES
    n_pair = q_ref.shape[0] // rows
    per = rows // s_new
    qf_ref[...] = q_ref[...].astype(F32) * SCALE
    kvf_ref[...] = kv_ref[...].astype(F32)
    lane = lax.broadcasted_iota(I32, (1, LANES), 1)
    lo = lane < HALF
    rid = lax.broadcasted_iota(I32, (N_HEADS * rows, 1), 0)
    head = rid // rows
    seq = (rid % rows) // s_new
    tok = rid % s_new
    sink = jnp.zeros((N_HEADS * rows, 1), F32)
    for h in range(N_HEADS):
        sink = jnp.where(head == h, sink_ref[h], sink)
    mask_c = lane > tok

    def pair(p, carry):
        r0 = pl.multiple_of(p * rows, rows)
        q8 = qf_ref[pl.ds(r0, rows), :]
        kv8 = kvf_ref[pl.ds(r0, rows), :]
        knew = kv8[:, :LANES]
        vnew = kv8[:, LANES:]
        pieces = []
        for h in range(N_HEADS):
            qt = q8[:, LANES * (h // 2):LANES * (h // 2 + 1)]
            tgt = h // (N_HEADS // N_KV_HEADS)
            if h % 2 != tgt:
                qt = _swap_halves(qt)
            pieces.append(jnp.where(lo if tgt == 0 else jnp.logical_not(lo), qt, 0.0))
        qm = jnp.concatenate(pieces, axis=0)
        qb = qm.astype(BF16)
        s_c = jnp.zeros((N_HEADS * rows, LANES), F32)
        for bb in range(per):
            kc = ck_ref[p * per + bb].astype(BF16)
            s_bb = lax.dot_general(qb, kc, (((1,), (1,)), ((), ())), preferred_element_type=F32)
            s_c = jnp.where(seq == bb, s_bb, s_c)
        s_c = jnp.where(mask_c, s_c, NEG_INF)
        m = jnp.maximum(jnp.max(s_c, axis=-1, keepdims=True), sink)
        s_n = []
        for k in range(rows):
            valid = (seq == k // s_new) & (tok >= k % s_new)
            sk = jnp.sum(qm * knew[k:k + 1, :], axis=-1, keepdims=True)
            sk = jnp.where(valid, sk, NEG_INF)
            s_n.append(sk)
            m = jnp.maximum(m, sk)
        p_c = jnp.exp(s_c - m)
        denom = jnp.sum(p_c, axis=-1, keepdims=True) + jnp.exp(sink - m)
        pb = p_c.astype(BF16)
        o = jnp.zeros((N_HEADS * rows, LANES), F32)
        for bb in range(per):
            vc = cv_ref[p * per + bb].astype(BF16)
            o = jnp.where(seq == bb, jnp.dot(pb, vc, preferred_element_type=F32), o)
        for k in range(rows):
            pk = jnp.exp(s_n[k] - m)
            denom = denom + pk
            o = o + pk.astype(BF16).astype(F32) * vnew[k:k + 1, :]
        o = o / denom
        for j in range(D_ATTN // LANES):
            kvh = (2 * j) // (N_HEADS // N_KV_HEADS)
            pe = o[rows * 2 * j:rows * (2 * j + 1), :]
            po = o[rows * (2 * j + 1):rows * (2 * j + 2), :]
            if kvh == 0:
                po = _swap_halves(po)
            else:
                pe = _swap_halves(pe)
            o_ref[pl.ds(r0, rows), LANES * j:LANES * (j + 1)] = jnp.where(lo, pe, po)
        return carry

    lax.fori_loop(0, n_pair, pair, 0)


def _attn_sample(q, kv, cache_k, cache_v, sinks, s_new, row0, n_seq, seq0):
    win = cache_k.shape[1]
    n_rows = n_seq * s_new
    rows_blk = SAMPLE_BT * s_new
    assert win == WINDOW and SUBLANES % s_new == 0 and n_seq % SAMPLE_BT == 0 and row0 % rows_blk == 0
    assert seq0 % SAMPLE_BT == 0
    blk0 = row0 // rows_blk
    cblk0 = seq0 // SAMPLE_BT
    return pl.pallas_call(
        functools.partial(_attn_sample_kernel, s_new=s_new),
        grid=(n_seq // SAMPLE_BT,),
        in_specs=[pl.BlockSpec(memory_space=pltpu.SMEM),
                  pl.BlockSpec((rows_blk, D_ATTN), lambda i: (blk0 + i, 0)),
                  pl.BlockSpec((rows_blk, 2 * D_KV), lambda i: (blk0 + i, 0)),
                  pl.BlockSpec((SAMPLE_BT, win, D_KV), lambda i: (cblk0 + i, 0, 0)),
                  pl.BlockSpec((SAMPLE_BT, win, D_KV), lambda i: (cblk0 + i, 0, 0))],
        out_specs=pl.BlockSpec((rows_blk, D_ATTN), lambda i: (i, 0)),
        out_shape=jax.ShapeDtypeStruct((n_rows, D_ATTN), F32),
        scratch_shapes=[pltpu.VMEM((rows_blk, D_ATTN), F32), pltpu.VMEM((rows_blk, 2 * D_KV), F32)],
        compiler_params=_params(("arbitrary",)),
        name="attn_sample",
    )(sinks, q, kv, cache_k, cache_v)


def _shift_cache_kernel(ck_ref, cv_ref, kn_ref, vn_ref, ko_ref, vo_ref, sem):
    win = ck_ref.shape[2]
    s_new = kn_ref.shape[2]
    keep = win - s_new
    copies = []
    for k, (old, new, out) in enumerate(((ck_ref, kn_ref, ko_ref), (cv_ref, vn_ref, vo_ref))):
        copies.append(pltpu.make_async_copy(old.at[:, :, pl.ds(s_new, keep)], out.at[:, :, pl.ds(0, keep)],
                                            sem.at[2 * k]))
        copies.append(pltpu.make_async_copy(new, out.at[:, :, pl.ds(keep, s_new)], sem.at[2 * k + 1]))
    for cp in copies:
        cp.start()
    for cp in copies:
        cp.wait()


def _shift_cache(cache_k, cache_v, k_new, v_new):
    hbm = pl.BlockSpec(memory_space=pl.ANY)
    return pl.pallas_call(
        _shift_cache_kernel,
        in_specs=[hbm] * 4,
        out_specs=[hbm, hbm],
        out_shape=[jax.ShapeDtypeStruct(cache_k.shape, cache_k.dtype), jax.ShapeDtypeStruct(cache_v.shape, cache_v.dtype)],
        scratch_shapes=[pltpu.SemaphoreType.DMA((4,))],
        name="cache_shift",
    )(cache_k, cache_v, k_new, v_new)


def _ln_silu(y, lg, lb):
    mu = jnp.mean(y, axis=-1, keepdims=True)
    var = jnp.mean(jnp.square(y - mu), axis=-1, keepdims=True)
    yn = (y - mu) * lax.rsqrt(var + EPS) * lg + lb
    return yn * jax.nn.sigmoid(yn)


def _conv_prompt_kernel(u_ref, w_ref, b_ref, lg_ref, lb_ref, o_ref, ext_ref):
    i = pl.program_id(1)

    @pl.when(i == 0)
    def _():
        ext_ref[0:CONV_HALO, :] = jnp.zeros((CONV_HALO, D_CONV), F32)
        ext_ref[CONV_HALO + CONV_T:, :] = jnp.zeros((SUBLANES, D_CONV), F32)

    @pl.when(i > 0)
    def _():
        ext_ref[0:CONV_HALO, :] = ext_ref[CONV_T:CONV_T + CONV_HALO, :]

    ext_ref[CONV_HALO:CONV_HALO + CONV_T, :] = u_ref[...]
    shift = CONV_HALO - (CONV_WIDTH - 1)
    for cidx in range(CONV_T // CONV_CHUNK):
        r0 = cidx * CONV_CHUNK
        acc = jnp.zeros((CONV_CHUNK, D_CONV), F32)
        for rho in range(SUBLANES):
            part = jnp.zeros((CONV_CHUNK + SUBLANES, D_CONV), F32)
            for j in range(CONV_WIDTH):
                if (j + shift) % SUBLANES == rho:
                    base = r0 + j + shift - rho
                    part = part + w_ref[j:j + 1, :] * ext_ref[base:base + CONV_CHUNK + SUBLANES, :]
            acc = acc + part[rho:rho + CONV_CHUNK, :]
        o_ref[r0:r0 + CONV_CHUNK, :] = _ln_silu(acc + b_ref[...], lg_ref[...], lb_ref[...]).astype(BF16)


def _conv_prompt(u, w, b, lg, lb, n_seq, t):
    nt = t // CONV_T
    vec = _full((1, D_CONV))
    return pl.pallas_call(
        _conv_prompt_kernel,
        grid=(n_seq, nt),
        in_specs=[pl.BlockSpec((CONV_T, D_CONV), lambda s, i: (s * nt + i, 0)),
                  _full((CONV_WIDTH, D_CONV)), vec, vec, vec],
        out_specs=pl.BlockSpec((CONV_T, D_CONV), lambda s, i: (s * nt + i, 0)),
        out_shape=jax.ShapeDtypeStruct((n_seq * t, D_CONV), BF16),
        scratch_shapes=[pltpu.VMEM((CONV_T + CONV_HALO + SUBLANES, D_CONV), F32)],
        compiler_params=_params(("arbitrary", "arbitrary")),
        name="conv_prompt",
    )(u, w, b.reshape(1, D_CONV), lg.reshape(1, D_CONV), lb.reshape(1, D_CONV))


def _conv_sample_kernel(st_ref, u_ref, w_ref, b_ref, lg_ref, lb_ref, o_ref, *, s_new):
    hist = CONV_WIDTH - 1
    for t in range(s_new):
        acc = jnp.zeros((st_ref.shape[0], D_CONV), F32)
        for j in range(CONV_WIDTH):
            idx = t + j
            if idx < hist:
                piece = st_ref[:, idx * D_CONV:(idx + 1) * D_CONV]
            else:
                piece = u_ref[:, (idx - hist) * D_CONV:(idx - hist + 1) * D_CONV]
            acc = acc + w_ref[j:j + 1, :] * piece
        o_ref[:, t * D_CONV:(t + 1) * D_CONV] = _ln_silu(acc + b_ref[...], lg_ref[...], lb_ref[...])


def _conv_sample(state2d, u2d, w, b, lg, lb, s_new, layer):
    n_seq = u2d.shape[0]
    vec = _full((1, D_CONV))
    return pl.pallas_call(
        functools.partial(_conv_sample_kernel, s_new=s_new),
        grid=(1,),
        in_specs=[pl.BlockSpec((n_seq, state2d.shape[1]), lambda i: (layer, 0)), _full(u2d.shape),
                  _full((CONV_WIDTH, D_CONV)), vec, vec, vec],
        out_specs=_full((n_seq, s_new * D_CONV)),
        out_shape=jax.ShapeDtypeStruct((n_seq, s_new * D_CONV), F32),
        compiler_params=_params(("arbitrary",)),
        name="conv_sample",
    )(state2d, u2d, w, b.reshape(1, D_CONV), lg.reshape(1, D_CONV), lb.reshape(1, D_CONV))


def _s5_discretize(a_re, a_im, log_dt):
    dt = jnp.exp(log_dt)
    mag = jnp.exp(a_re * dt)
    ang = a_im * dt
    lr = mag * jnp.cos(ang)
    li = mag * jnp.sin(ang)
    den = a_re * a_re + a_im * a_im
    cr = ((lr - 1.0) * a_re + li * a_im) / den
    ci = (li * a_re - (lr - 1.0) * a_im) / den
    return lr, li, cr, ci


def _s5_bbar(arow_ref, bre_ref, bim_ref):
    _, _, cr, ci = _s5_discretize(arow_ref[0:1, :], arow_ref[1:2, :], arow_ref[2:3, :])
    bre = bre_ref[...]
    bim = bim_ref[...]
    return (cr * bre - ci * bim).astype(BF16), (cr * bim + ci * bre).astype(BF16)


def _s5_readout(h_re, h_im, u, cre_ref, cim_ref, d_ref, gw_ref, gb_ref):
    y = (jnp.dot(h_re.astype(BF16), cre_ref[...], preferred_element_type=F32)
         - jnp.dot(h_im.astype(BF16), cim_ref[...], preferred_element_type=F32)
         + d_ref[...] * u.astype(F32))
    z = jax.nn.gelu(y)
    gate = jnp.dot(z.astype(BF16), gw_ref[...], preferred_element_type=F32) + gb_ref[...]
    return (z * jax.nn.sigmoid(gate)).astype(BF16)


def _s5_prompt_kernel(su0_ref, su1_ref, arow_ref, atile_ref, bre_ref, bim_ref, cre_ref, cim_ref, d_ref,
                      gw_ref, gb_ref, o0_ref, o1_ref, hre_ref, him_ref,
                      bbr_ref, bbi_ref, lam_ref, car_ref, bur_ref, bui_ref, hbr_ref, hbi_ref):
    i = pl.program_id(0)
    n_slab = N_STATE // LANES
    su_refs = (su0_ref, su1_ref)
    o_refs = (o0_ref, o1_ref)

    @pl.when(i == 0)
    def _():
        bbr, bbi = _s5_bbar(arow_ref, bre_ref, bim_ref)
        bbr_ref[...] = bbr
        bbi_ref[...] = bbi
        lr, li, _, _ = _s5_discretize(atile_ref[0], atile_ref[1], atile_ref[2])
        lam_ref[0] = lr
        lam_ref[1] = li
        car_ref[...] = jnp.zeros(car_ref.shape, F32)

    for s in range(2):
        u = su_refs[s][...]
        br = jnp.dot(u, bbr_ref[...], preferred_element_type=F32)
        bi = jnp.dot(u, bbi_ref[...], preferred_element_type=F32)
        for j in range(n_slab):
            bur_ref[s, j * SCAN_PITCH:j * SCAN_PITCH + SCAN_T, :] = br[:, LANES * j:LANES * (j + 1)]
            bui_ref[s, j * SCAN_PITCH:j * SCAN_PITCH + SCAN_T, :] = bi[:, LANES * j:LANES * (j + 1)]

    lr = lam_ref[0]
    li = lam_ref[1]

    def step(t, carry):
        new = []
        for s in range(2):
            hr, hi = carry[2 * s], carry[2 * s + 1]
            rows = pl.ds(t, n_slab, stride=SCAN_PITCH)
            nr = lr * hr - li * hi + bur_ref.at[s][rows, :]
            ni = lr * hi + li * hr + bui_ref.at[s][rows, :]
            hbr_ref.at[s][rows, :] = nr
            hbi_ref.at[s][rows, :] = ni
            new += [nr, ni]
        return tuple(new)

    carry = lax.fori_loop(0, SCAN_T, step, tuple(car_ref[k] for k in range(4)), unroll=8)
    for k in range(4):
        car_ref[k] = carry[k]

    @pl.when(i == pl.num_programs(0) - 1)
    def _():
        for s in range(2):
            hre_ref[s] = carry[2 * s]
            him_ref[s] = carry[2 * s + 1]

    for s in range(2):
        h_re = jnp.concatenate([hbr_ref[s, j * SCAN_PITCH:j * SCAN_PITCH + SCAN_T, :] for j in range(n_slab)], axis=1)
        h_im = jnp.concatenate([hbi_ref[s, j * SCAN_PITCH:j * SCAN_PITCH + SCAN_T, :] for j in range(n_slab)], axis=1)
        o_refs[s][...] = _s5_readout(h_re, h_im, su_refs[s][...], cre_ref, cim_ref, d_ref, gw_ref, gb_ref)


def _s5_prompt(su, prm, t):
    nt = t // SCAN_T
    n_slab = N_STATE // LANES
    blk0 = pl.BlockSpec((SCAN_T, D_SSM), lambda i: (i, 0))
    blk1 = pl.BlockSpec((SCAN_T, D_SSM), lambda i: (nt + i, 0))
    oblk = pl.BlockSpec((SCAN_T, D_SSM), lambda i: (i, 0))
    state = pl.BlockSpec((2, SUBLANES, LANES), lambda i: (0, 0, 0))
    slabs = pltpu.VMEM((2, n_slab * SCAN_PITCH, LANES), F32)
    return pl.pallas_call(
        _s5_prompt_kernel,
        grid=(nt,),
        in_specs=[blk0, blk1, _full((3, N_STATE)), _full((3, SUBLANES, LANES)),
                  _full((D_SSM, N_STATE)), _full((D_SSM, N_STATE)), _full((N_STATE, D_SSM)), _full((N_STATE, D_SSM)),
                  _full((1, D_SSM)), _full((D_SSM, D_SSM)), _full((1, D_SSM))],
        out_specs=[oblk, oblk, state, state],
        out_shape=[jax.ShapeDtypeStruct((t, D_SSM), BF16), jax.ShapeDtypeStruct((t, D_SSM), BF16),
                   jax.ShapeDtypeStruct((2, SUBLANES, LANES), F32), jax.ShapeDtypeStruct((2, SUBLANES, LANES), F32)],
        scratch_shapes=[pltpu.VMEM((D_SSM, N_STATE), BF16), pltpu.VMEM((D_SSM, N_STATE), BF16),
                        pltpu.VMEM((2, SUBLANES, LANES), F32), pltpu.VMEM((4, SUBLANES, LANES), F32),
                        slabs, slabs, slabs, slabs],
        compiler_params=_params(("arbitrary",)),
        name="s5_prompt",
    )(su, su, prm["arow"], prm["atile"], prm["bre"], prm["bim"], prm["cre"], prm["cim"],
      prm["d"], prm["gw"], prm["gb"])


def _s5_sample_kernel(su_ref, h0r_ref, h0i_ref, arow_ref, bre_ref, bim_ref, cre_ref, cim_ref, d_ref, gw_ref, gb_ref,
                      o_ref, hr_ref, hi_ref, hbr_ref, hbi_ref, *, s_new):
    n_seq = h0r_ref.shape[0]
    lr, li, _, _ = _s5_discretize(arow_ref[0:1, :], arow_ref[1:2, :], arow_ref[2:3, :])
    bbr, bbi = _s5_bbar(arow_ref, bre_ref, bim_ref)
    u = su_ref[...]
    hbr_ref[...] = jnp.dot(u, bbr, preferred_element_type=F32)
    hbi_ref[...] = jnp.dot(u, bbi, preferred_element_type=F32)
    hr_ref[...] = h0r_ref[...]
    hi_ref[...] = h0i_ref[...]
    for t in range(s_new):
        rows = slice(t * n_seq, (t + 1) * n_seq)
        hr = hr_ref[...]
        hi = hi_ref[...]
        nr = lr * hr - li * hi + hbr_ref[rows, :]
        ni = lr * hi + li * hr + hbi_ref[rows, :]
        hbr_ref[rows, :] = nr
        hbi_ref[rows, :] = ni
        hr_ref[...] = nr
        hi_ref[...] = ni
    o_ref[...] = _s5_readout(hbr_ref[...], hbi_ref[...], u, cre_ref, cim_ref, d_ref, gw_ref, gb_ref)


def _s5_sample(su_tb, h0r, h0i, prm, s_new):
    n_rows = su_tb.shape[0]
    n_seq = n_rows // s_new
    return pl.pallas_call(
        functools.partial(_s5_sample_kernel, s_new=s_new),
        out_shape=[jax.ShapeDtypeStruct((n_rows, D_SSM), BF16),
                   jax.ShapeDtypeStruct((n_seq, N_STATE), F32), jax.ShapeDtypeStruct((n_seq, N_STATE), F32)],
        scratch_shapes=[pltpu.VMEM((n_rows, N_STATE), F32), pltpu.VMEM((n_rows, N_STATE), F32)],
        compiler_params=pltpu.CompilerParams(vmem_limit_bytes=VMEM_LIMIT),
        name="s5_sample",
    )(su_tb, h0r, h0i, prm["arow"], prm["bre"], prm["bim"], prm["cre"], prm["cim"], prm["d"], prm["gw"], prm["gb"])


def _s5_params(p, l):
    eye = jnp.eye(SSM_GROUPS, dtype=F32)

    def b_diag(b):
        return jnp.einsum("gnc,gh->gchn", b, eye).reshape(D_SSM, N_STATE)

    def c_diag(c):
        return jnp.einsum("gcn,gh->gnhc", c, eye).reshape(N_STATE, D_SSM)

    ldt = jnp.broadcast_to(p["ssm_log_dt"][l][:, None], (SSM_GROUPS, SSM_STATE))
    a3 = jnp.stack([p["ssm_a_re"][l], p["ssm_a_im"][l], ldt])
    return {
        "arow": a3.reshape(3, N_STATE),
        "atile": a3.reshape(3, SUBLANES, LANES),
        "bre": b_diag(p["ssm_b_re"][l]), "bim": b_diag(p["ssm_b_im"][l]),
        "cre": c_diag(p["ssm_c_re"][l]).astype(BF16), "cim": c_diag(p["ssm_c_im"][l]).astype(BF16),
        "d": p["ssm_d"][l].reshape(1, D_SSM),
        "gw": p["ssm_glu_w"][l].astype(BF16), "gb": p["ssm_glu_b"][l].reshape(1, D_SSM),
    }


def _merge_out_kernel(*refs, starts):
    it = iter(refs)
    oa_refs, oc_refs, os_refs, x_refs = ([next(it) for _ in s] for s in starts)
    gn_ref, wo_ref, fg_ref, wrh_ref, wrl_ref, br_ref = (next(it) for _ in range(6))
    x1_ref, xn_ref, tokc_ref, tokl_ref, tile_ref, glob_ref = (next(it) for _ in range(6))
    carry_ref, wobf_ref = next(it), next(it)
    i = pl.program_id(0)
    tm = TM_OUT

    @pl.when(i == 0)
    def _():
        carry_ref[...] = jnp.zeros(carry_ref.shape, F32)
        wobf_ref[...] = wo_ref[...].astype(BF16)

    gn = gn_ref[...]
    lane = lax.broadcasted_iota(I32, (1, LANES), 1).astype(F32)
    far = float(LANES)
    is_group = lane < N_EXPERT_GROUPS
    r_i = lax.broadcasted_iota(I32, (tm, tm), 0)
    c_i = lax.broadcasted_iota(I32, (tm, tm), 1)
    lower = jnp.where(c_i < r_i, 1.0, 0.0).astype(BF16)
    lr = lax.broadcasted_iota(I32, (LANES, LANES), 0)
    lc = lax.broadcasted_iota(I32, (LANES, LANES), 1)
    before = jnp.where(lr < lc, 1.0, 0.0).astype(BF16)
    is_e_col = (lr >= ROUTER_LANE0) & (lr < ROUTER_LANE0 + N_EXPERTS)
    zrow = jnp.zeros((1, LANES), F32)
    row0 = lane * RUN

    def first_max(v):
        top = jnp.max(v, axis=-1, keepdims=True)
        return top, jnp.min(jnp.where(v == top, lane, far), axis=-1, keepdims=True)

    def at(sel, v):
        return jnp.sum(jnp.where(sel, v, 0.0), axis=-1, keepdims=True)

    oa_all = _pick(i, oa_refs, starts[0])
    oc_all = _pick(i, oc_refs, starts[1])
    os_all = _pick(i, os_refs, starts[2])
    x_all = _pick(i, x_refs, starts[3])
    carry = carry_ref[...]

    for sub in range(MERGE_SUB):
        rows_t = slice(sub * tm, (sub + 1) * tm)
        rows_8 = slice(sub * SUBLANES, (sub + 1) * SUBLANES)
        mix = jnp.concatenate([
            _rms(oa_all[rows_t], gn[:, :D_ATTN]),
            _rms(oc_all[rows_t], gn[:, D_ATTN:D_ATTN + D_CONV]),
            _rms(os_all[rows_t], gn[:, D_ATTN + D_CONV:]),
        ], axis=1).astype(BF16)
        x1 = x_all[rows_t] + jnp.dot(mix, wobf_ref[...], preferred_element_type=F32)
        x1_ref[rows_t, :] = x1
        xn = _rms(x1, fg_ref[...])
        xn_ref[rows_t, :] = _pack_pair(xn[:, :D_PACK], xn[:, D_PACK:])

        xh = xn.astype(BF16)
        xl = (xn - xh.astype(F32)).astype(BF16)
        wh = wrh_ref[...]
        logits = (jnp.dot(xh, wh, preferred_element_type=F32) + jnp.dot(xl, wh, preferred_element_type=F32)
                  + jnp.dot(xh, wrl_ref[...], preferred_element_type=F32) + br_ref[...])

        g_top, g_idx = first_max(jnp.where(is_group, logits, -jnp.inf))
        g_w = 1.0 / jnp.sum(jnp.where(is_group, jnp.exp(logits - g_top), 0.0), axis=-1, keepdims=True)
        e_lo = ROUTER_LANE0 + EXPERTS_PER_GROUP * g_idx
        el = jnp.where((lane >= e_lo) & (lane < e_lo + EXPERTS_PER_GROUP), logits, -jnp.inf)
        v1, i1 = first_max(el)
        v2, i2 = first_max(jnp.where(lane == i1, -jnp.inf, el))
        t2 = jnp.exp(v2 - v1)
        w1 = g_w / (1.0 + t2)
        w2 = g_w * t2 / (1.0 + t2)

        sel1 = lane == i1
        sel2 = lane == i2
        onehot = jnp.where(sel1 | sel2, 1.0, 0.0)
        rloc = jnp.dot(lower, onehot.astype(BF16), preferred_element_type=F32)
        cnt = jnp.sum(onehot, axis=0, keepdims=True)
        cnt_pad = jnp.floor((cnt + (RUN - 1.0)) * (1.0 / RUN)) * RUN
        stacked = jnp.concatenate([cnt_pad] + [zrow] * (SUBLANES - 1), axis=0).astype(BF16)
        lstart = jnp.dot(stacked, before, preferred_element_type=F32)[0:1, :]

        pos1, pos2 = at(sel1, lstart + rloc), at(sel2, lstart + rloc)
        tok = jnp.zeros((tm, LANES), F32)
        for col, val in ((TOK_W1, w1), (TOK_W2, w2), (TOK_P1, pos1), (TOK_P2, pos2)):
            tok = jnp.where(lane == col, val, tok)
        tokc_ref[rows_t, :] = tok[:, :tokc_ref.shape[1]]
        tokl_ref[rows_8, :] = jnp.transpose(tok)[:SUBLANES, :]

        per_expert = jnp.concatenate([lstart, lstart + cnt_pad, carry - lstart,
                                      jnp.zeros((LANES - 3, LANES), F32)], axis=0)
        cols = jnp.transpose(per_expert)
        own = is_e_col & (cols[:, 0:1] <= row0) & (row0 < cols[:, 1:2])
        chunk_e = jnp.sum(jnp.where(own, lr.astype(F32), 0.0), axis=0, keepdims=True)
        chunk_rel = row0 + jnp.sum(jnp.where(own, cols[:, 2:3], 0.0), axis=0, keepdims=True)
        n_chunk = jnp.sum(cnt_pad, axis=-1, keepdims=True) * (1.0 / RUN)
        rows = [zrow] * SUBLANES
        rows[TILE_NCHUNK] = jnp.broadcast_to(n_chunk, (1, LANES))
        rows[TILE_CHUNK_E], rows[TILE_CHUNK_REL] = chunk_e, chunk_rel
        tile_ref[rows_8, :] = jnp.concatenate(rows, axis=0).astype(I32)
        carry = carry + cnt_pad

    total = carry
    carry_ref[...] = total

    @pl.when(i == pl.num_programs(0) - 1)
    def _():
        is_e = (lane >= ROUTER_LANE0) & (lane < ROUTER_LANE0 + N_EXPERTS)
        ptiles = jnp.floor((total + (TM_MOE - 1.0)) * (1.0 / TM_MOE))
        upto = jnp.where(lr <= lc, 1.0, 0.0).astype(BF16)
        pt8 = jnp.concatenate([ptiles, jnp.zeros((SUBLANES - 1, LANES), F32)], axis=0).astype(BF16)
        tend = jnp.dot(pt8, upto, preferred_element_type=F32)[0:1, :]
        n_used = jnp.max(tend, axis=-1, keepdims=True)
        e_last = jnp.max(jnp.where(ptiles > 0.0, lane - ROUTER_LANE0, -1.0), axis=-1, keepdims=True)
        tend_col = jnp.transpose(jnp.broadcast_to(tend, (LANES, LANES)))

        def tile_expert(first_tile):
            hit = is_e_col & (tend_col <= lane + first_tile)
            return jnp.minimum(jnp.sum(jnp.where(hit, 1.0, 0.0), axis=0, keepdims=True), e_last)

        rows = [zrow] * SUBLANES
        rows[GLOB_START] = (tend - ptiles) * TM_MOE
        rows[GLOB_ZERO] = jnp.where(is_e & (total > 0.0), tend - 1.0, -1.0)
        rows[GLOB_TE] = tile_expert(0.0)
        rows[GLOB_TE_HI] = tile_expert(float(LANES))
        rows[GLOB_NUSED] = jnp.broadcast_to(n_used, (1, LANES))
        rows[GLOB_PTILES] = ptiles
        glob_ref[...] = jnp.concatenate(rows, axis=0).astype(I32)


def _merge_out(oa_src, oc_src, os_src, x_src, gn, wo, layer, fg, wr_hi, wr_lo, br):
    specs, starts, n_blk = [], [], None
    for src, width in ((oa_src, D_ATTN), (oc_src, D_CONV), (os_src, D_SSM), (x_src, D_MODEL)):
        sp, st, nb = _source_specs(src, MERGE_SUB * TM_OUT, width)
        assert n_blk in (None, nb)
        n_blk = nb
        specs += sp
        starts.append(st)
    n = n_blk * MERGE_SUB * TM_OUT
    row = lambda width: pl.BlockSpec((MERGE_SUB * TM_OUT, width), lambda i: (i, 0))
    tbl = lambda width: pl.BlockSpec((MERGE_SUB * SUBLANES, width), lambda i: (i, 0))
    return pl.pallas_call(
        functools.partial(_merge_out_kernel, starts=tuple(starts)),
        grid=(n_blk,),
        in_specs=specs + [_full((1, D_MODEL)), _layer_block((D_MODEL, D_MODEL), layer), _full((1, D_MODEL)),
                          _full((D_MODEL, LANES)), _full((D_MODEL, LANES)), _full((1, LANES))],
        out_specs=[row(D_MODEL), row(D_PACK), row(4), tbl(TM_OUT), tbl(LANES), _full((SUBLANES, LANES))],
        out_shape=[jax.ShapeDtypeStruct((n, D_MODEL), F32), jax.ShapeDtypeStruct((n, D_PACK), U32),
                   jax.ShapeDtypeStruct((n, 4), F32),
                   jax.ShapeDtypeStruct((n_blk * MERGE_SUB * SUBLANES, TM_OUT), F32),
                   jax.ShapeDtypeStruct((n_blk * MERGE_SUB * SUBLANES, LANES), I32),
                   jax.ShapeDtypeStruct((SUBLANES, LANES), I32)],
        scratch_shapes=[pltpu.VMEM((1, LANES), F32), pltpu.VMEM((D_MODEL, D_MODEL), BF16)],
        compiler_params=_params(("arbitrary",)),
        name="merge_out",
    )(*oa_src, *oc_src, *os_src, *x_src, gn.reshape(1, D_MODEL), wo, fg.reshape(1, D_MODEL), wr_hi, wr_lo, br)


def _chunk_groups(tile_ref):
    n_chunk = tile_ref[TILE_NCHUNK, 0]
    return lax.shift_right_logical(n_chunk + (CHUNK_GROUP - 1), CHUNK_GROUP.bit_length() - 1)


def _for_each_chunk(tile_ref, glob_ref, fn):
    n_chunk = tile_ref[TILE_NCHUNK, 0]
    n_group = _chunk_groups(tile_ref)

    def group(g, carry):
        for k in range(CHUNK_GROUP):
            c = g * CHUNK_GROUP + k
            seg = glob_ref[GLOB_START * LANES + tile_ref[TILE_CHUNK_E, c]]
            fn(pl.multiple_of(c * RUN, RUN), c < n_chunk, seg + tile_ref[TILE_CHUNK_REL, c], k)
        return carry

    lax.fori_loop(0, n_group, group, 0)
    return n_group * CHUNK_GROUP


def _wait_chunks(n, wait_one):
    def group(g, carry):
        for _ in range(CHUNK_GROUP):
            wait_one()
        return carry

    lax.fori_loop(0, lax.shift_right_logical(n, CHUNK_GROUP.bit_length() - 1), group, 0)


def _dispatch_kernel(tile_ref, glob_ref, tokl_ref, xn_ref, xs_ref, sbuf_ref, zbuf_ref, inflight_ref, sem_z, sem_r):
    i = pl.program_id(0)
    last = pl.num_programs(0) - 1
    slot = lax.rem(i, 2)
    trash0 = xs_ref.shape[0] - TRASH_ROWS

    def zero_copy(t):
        return pltpu.make_async_copy(zbuf_ref, xs_ref.at[pl.ds(pl.multiple_of(t * TM_MOE, TM_MOE), TM_MOE)], sem_z)

    def for_zero_tiles(fn):
        def seg_last(k, c):
            t = glob_ref[GLOB_ZERO * LANES + k]

            @pl.when(t >= 0)
            def _():
                fn(t)
            return c

        def unused(t, c):
            fn(t)
            return c

        lax.fori_loop(0, LANES, seg_last, 0)
        lax.fori_loop(glob_ref[GLOB_NUSED * LANES], trash0 // TM_MOE, unused, 0)

    @pl.when(i == 0)
    def _():
        zbuf_ref[...] = jnp.zeros(zbuf_ref.shape, U32)
        trash = pltpu.make_async_copy(zbuf_ref.at[pl.ds(0, TRASH_ROWS)], xs_ref.at[pl.ds(trash0, TRASH_ROWS)], sem_z)
        trash.start()
        for_zero_tiles(lambda t: zero_copy(t).start())
        for_zero_tiles(lambda t: zero_copy(t).wait())
        trash.wait()
        inflight_ref[0] = 0

    q = lax.broadcasted_iota(I32, (N_LOCAL, 1), 0).astype(F32)
    hit = (q == tokl_ref[TOK_P1:TOK_P1 + 1, :]) | (q == tokl_ref[TOK_P2:TOK_P2 + 1, :])
    sel = jnp.where(hit, 1.0, 0.0).astype(BF16)
    a, b = _unpack_pair(xn_ref[...])
    sa = jnp.dot(sel, a.astype(BF16), preferred_element_type=F32)
    sb = jnp.dot(sel, b.astype(BF16), preferred_element_type=F32)
    sbuf_ref[slot] = _pack_pair(sa, sb)

    def start_chunk(local_row, real, sorted_row, k):
        dst = pl.multiple_of(jnp.where(real, sorted_row, trash0 + k * RUN), RUN)
        pltpu.make_async_copy(sbuf_ref.at[slot, pl.ds(local_row, RUN)], xs_ref.at[pl.ds(dst, RUN)], sem_r).start()

    def drain(n):
        _wait_chunks(n, lambda: pltpu.make_async_copy(sbuf_ref.at[0, pl.ds(0, RUN)], xs_ref.at[pl.ds(0, RUN)],
                                                      sem_r).wait())

    drain(inflight_ref[0])
    n_issued = _for_each_chunk(tile_ref, glob_ref, start_chunk)
    inflight_ref[0] = n_issued

    @pl.when(i == last)
    def _():
        drain(n_issued)


def _dispatch(tile_tbl, glob_flat, tokl, xn, n_rows_sorted):
    n = xn.shape[0]
    return pl.pallas_call(
        _dispatch_kernel,
        grid=(n // TM_OUT,),
        in_specs=[pl.BlockSpec((SUBLANES, LANES), lambda i: (i, 0), memory_space=pltpu.SMEM),
                  pl.BlockSpec(memory_space=pltpu.SMEM),
                  pl.BlockSpec((SUBLANES, TM_OUT), lambda i: (i, 0)),
                  pl.BlockSpec((TM_OUT, D_PACK), lambda i: (i, 0))],
        out_specs=pl.BlockSpec(memory_space=pl.ANY),
        out_shape=jax.ShapeDtypeStruct((n_rows_sorted, D_PACK), U32),
        scratch_shapes=[pltpu.VMEM((2, N_LOCAL, D_PACK), U32), pltpu.VMEM((TM_MOE, D_PACK), U32),
                        pltpu.SMEM((1,), I32), pltpu.SemaphoreType.DMA(()), pltpu.SemaphoreType.DMA(())],
        compiler_params=_params(("arbitrary",)),
        name="moe_dispatch",
    )(tile_tbl, glob_flat, tokl, xn)


def _tile_expert(glob_ref, i):
    return glob_ref[GLOB_TE * LANES + i]


def _moe_kernel(glob_ref, xs_ref, wg_hbm, wu_hbm, wd_hbm, ys_ref, wgb_ref, wub_ref, wdb_ref,
                sg_ref, su_ref, sd_ref, ord_ref, sem, *, layer):
    i = pl.program_id(0)
    used = i < glob_ref[GLOB_NUSED * LANES]
    expert = _tile_expert(glob_ref, i)
    new_expert = (i == 0) | (expert != _tile_expert(glob_ref, jnp.maximum(i - 1, 0)))

    def weight_copies(e, slot):
        return [pltpu.make_async_copy(src.at[layer, e], dst.at[slot], sem.at[slot])
                for src, dst in ((wg_hbm, sg_ref), (wu_hbm, su_ref), (wd_hbm, sd_ref))]

    @pl.when(i == 0)
    def _():
        ord_ref[0] = 0
        for cp in weight_copies(expert, 0):
            cp.start()

    @pl.when(used & new_expert)
    def _():
        slot = lax.rem(ord_ref[0], 2)
        for cp in weight_copies(expert, slot):
            cp.wait()
        nxt = lax.while_loop(
            lambda k: (k < N_EXPERTS) & (glob_ref[GLOB_PTILES * LANES + ROUTER_LANE0 + jnp.minimum(k, N_EXPERTS - 1)] == 0),
            lambda k: k + 1, expert + 1)

        @pl.when(nxt < N_EXPERTS)
        def _():
            for cp in weight_copies(nxt, 1 - slot):
                cp.start()

        wgb_ref[...] = sg_ref[slot].astype(BF16)
        wub_ref[...] = su_ref[slot].astype(BF16)
        wdb_ref[...] = sd_ref[slot].astype(BF16)
        ord_ref[0] = ord_ref[0] + 1

    @pl.when(used)
    def _():
        a, b = _unpack_pair(xs_ref[...])
        x = jnp.concatenate([a, b], axis=1).astype(BF16)
        gate = jnp.dot(x, wgb_ref[...], preferred_element_type=F32)
        up = jnp.dot(x, wub_ref[...], preferred_element_type=F32)
        h = (gate * jax.nn.sigmoid(gate) * up).astype(BF16)
        y = jnp.dot(h, wdb_ref[...], preferred_element_type=F32)
        ys_ref[...] = _pack_pair(y[:, :D_PACK], y[:, D_PACK:])

    @pl.when(jnp.logical_not(used))
    def _():
        zero = jnp.zeros(ys_ref.shape, F32)
        ys_ref[...] = _pack_pair(zero, zero)


def _moe(glob_flat, xs, wg, wu, wd, layer):
    n_tiles = (xs.shape[0] - TRASH_ROWS) // TM_MOE
    assert n_tiles <= 2 * LANES
    hbm = pl.BlockSpec(memory_space=pl.ANY)
    grid_spec = pltpu.PrefetchScalarGridSpec(
        num_scalar_prefetch=1,
        grid=(n_tiles,),
        in_specs=[pl.BlockSpec((TM_MOE, D_PACK), lambda i, g: (jnp.minimum(i, g[GLOB_NUSED * LANES] - 1), 0)),
                  hbm, hbm, hbm],
        out_specs=pl.BlockSpec((TM_MOE, D_PACK), lambda i, g: (i, 0)),
        scratch_shapes=[pltpu.VMEM((D_MODEL, D_EXPERT), BF16), pltpu.VMEM((D_MODEL, D_EXPERT), BF16),
                        pltpu.VMEM((D_EXPERT, D_MODEL), BF16),
                        pltpu.VMEM((2, D_MODEL, D_EXPERT), F32), pltpu.VMEM((2, D_MODEL, D_EXPERT), F32),
                        pltpu.VMEM((2, D_EXPERT, D_MODEL), F32), pltpu.SMEM((1,), I32),
                        pltpu.SemaphoreType.DMA((2,))],
    )
    return pl.pallas_call(
        functools.partial(_moe_kernel, layer=layer),
        grid_spec=grid_spec,
        out_shape=jax.ShapeDtypeStruct((n_tiles * TM_MOE, D_PACK), U32),
        compiler_params=_params(("arbitrary",)),
        name="moe_experts",
    )(glob_flat, xs, wg, wu, wd)


def _combine_kernel(tile_ref, next_ref, glob_ref, tokc_ref, x1_ref, ys_ref, fg_ref, *rest, n_first, final):
    n_out = 2 if final else 1
    out_refs = rest[:n_out]
    lbuf_ref, inflight_ref, sem = rest[n_out:]
    i = pl.program_id(0)
    last = pl.num_programs(0) - 1
    slot = lax.rem(i, 2)

    def gather(tbl_ref, dst_slot):
        def start(local_row, real, sorted_row, k):
            src = pl.multiple_of(jnp.where(real, sorted_row, 0), RUN)
            pltpu.make_async_copy(ys_ref.at[pl.ds(src, RUN)], lbuf_ref.at[dst_slot, pl.ds(local_row, RUN)],
                                  sem.at[dst_slot]).start()

        inflight_ref[dst_slot] = _for_each_chunk(tbl_ref, glob_ref, start)

    @pl.when(i == 0)
    def _():
        lbuf_ref[...] = jnp.zeros(lbuf_ref.shape, U32)
        gather(tile_ref, 0)

    @pl.when(i < last)
    def _():
        gather(next_ref, 1 - slot)

    _wait_chunks(inflight_ref[slot], lambda: pltpu.make_async_copy(
        ys_ref.at[pl.ds(0, RUN)], lbuf_ref.at[slot, pl.ds(0, RUN)], sem.at[slot]).wait())

    tokc = tokc_ref[...]
    w1 = tokc[:, TOK_W1:TOK_W1 + 1]
    w2 = tokc[:, TOK_W2:TOK_W2 + 1]
    col = lax.broadcasted_iota(I32, (1, N_LOCAL), 1).astype(F32)
    sel1 = jnp.where(col == tokc[:, TOK_P1:TOK_P1 + 1], 1.0, 0.0).astype(BF16)
    sel2 = jnp.where(col == tokc[:, TOK_P2:TOK_P2 + 1], 1.0, 0.0).astype(BF16)
    halves = []
    for part in _unpack_pair(lbuf_ref[slot]):
        rows = part.astype(BF16)
        halves.append(w1 * jnp.dot(sel1, rows, preferred_element_type=F32)
                      + w2 * jnp.dot(sel2, rows, preferred_element_type=F32))
    x2 = x1_ref[...] + jnp.concatenate(halves, axis=1)
    if not final:
        out_refs[0][...] = x2
    else:
        y = _rms(x2, fg_ref[...])

        @pl.when(i < n_first)
        def _():
            out_refs[0][...] = y

        @pl.when(i >= n_first)
        def _():
            out_refs[1][...] = y


def _combine(tile_tbl, glob_flat, tokc, x1, ys, fg, n_first_rows, final):
    n = x1.shape[0]
    n_blk = n // TM_OUT
    n_first = n_first_rows // TM_OUT
    assert n_first_rows % TM_OUT == 0
    row = lambda width: pl.BlockSpec((TM_OUT, width), lambda i: (i, 0))
    if final:
        out_specs = [pl.BlockSpec((TM_OUT, D_MODEL), lambda i: (jnp.minimum(i, n_first - 1), 0)),
                     pl.BlockSpec((TM_OUT, D_MODEL), lambda i: (jnp.maximum(i - n_first, 0), 0))]
        out_shape = [jax.ShapeDtypeStruct((n_first_rows, D_MODEL), F32),
                     jax.ShapeDtypeStruct((n - n_first_rows, D_MODEL), F32)]
    else:
        out_specs = [row(D_MODEL)]
        out_shape = [jax.ShapeDtypeStruct((n, D_MODEL), F32)]
    return pl.pallas_call(
        functools.partial(_combine_kernel, n_first=n_first, final=final),
        grid=(n_blk,),
        in_specs=[pl.BlockSpec((SUBLANES, LANES), lambda i: (i, 0), memory_space=pltpu.SMEM),
                  pl.BlockSpec((SUBLANES, LANES), lambda i: (jnp.minimum(i + 1, n_blk - 1), 0), memory_space=pltpu.SMEM),
                  pl.BlockSpec(memory_space=pltpu.SMEM),
                  row(4), row(D_MODEL), pl.BlockSpec(memory_space=pl.ANY), _full((1, D_MODEL))],
        out_specs=out_specs,
        out_shape=out_shape,
        scratch_shapes=[pltpu.VMEM((2, N_LOCAL, D_PACK), U32), pltpu.SMEM((2,), I32),
                        pltpu.SemaphoreType.DMA((2,))],
        compiler_params=_params(("arbitrary",)),
        name="moe_combine",
    )(tile_tbl, tile_tbl, glob_flat, tokc, x1, ys, fg.reshape(1, D_MODEL))


def _n_moe_tiles(n_tokens):
    n_runs = (n_tokens // TM_OUT) * N_EXPERTS
    return (2 * n_tokens + n_runs * (RUN - 1) + N_EXPERTS * (TM_MOE - 1)) // TM_MOE + 1


def kernel(x_prompt, x_sample, cache_k, cache_v, state_conv, state_ssm_re, state_ssm_im, attn_norm_g, w_in, attn_sinks, conv_w, conv_b, conv_ln_g, conv_ln_b, ssm_a_re, ssm_a_im, ssm_log_dt, ssm_b_re, ssm_b_im, ssm_c_re, ssm_c_im, ssm_d, ssm_glu_w, ssm_glu_b, grp_norm_g, w_out, ffn_norm_g, w_group_router, b_group_router, w_expert_router, b_expert_router, w_gate, w_up, w_down, final_norm_g):
    p = dict(ssm_a_re=ssm_a_re, ssm_a_im=ssm_a_im, ssm_log_dt=ssm_log_dt, ssm_b_re=ssm_b_re, ssm_b_im=ssm_b_im,
             ssm_c_re=ssm_c_re, ssm_c_im=ssm_c_im, ssm_d=ssm_d, ssm_glu_w=ssm_glu_w, ssm_glu_b=ssm_glu_b)
    depth = w_in.shape[0]
    n_seq, t, _ = x_prompt.shape
    n_dec, s_new, _ = x_sample.shape
    win = cache_k.shape[2]
    n_p = n_seq * t
    n_s = n_dec * s_new
    n = n_p + n_s
    assert n_seq == 2 and t % SCAN_T == 0 and t % CONV_T == 0 and n_p % TM_IN == 0 and n_s % TM_IN == 0
    hist = CONV_WIDTH - 1
    n_tiles = _n_moe_tiles(n)

    x_src = [x_prompt.reshape(n_p, D_MODEL), x_sample.reshape(n_s, D_MODEL)]
    outs = {k: [] for k in ("kp", "vp", "cp", "rp", "ip", "kv_new", "u_new", "rs", "is")}
    ck_all = cache_k.reshape(depth * n_dec, win, D_KV)
    cv_all = cache_v.reshape(depth * n_dec, win, D_KV)
    conv_all = state_conv.reshape(depth * n_dec, hist * D_CONV)
    for l in range(depth):
        q, kv, u, su = _in_proj(x_src, attn_norm_g[l], w_in, l)

        oa_p = _attn_prompt(q, kv, attn_sinks[l], n_seq, t)
        oa_s = _attn_sample(q, kv, ck_all, cv_all, attn_sinks[l], s_new, n_p, n_dec, l * n_dec)
        kv_p = kv[:n_p].reshape(n_seq, t, 2 * D_KV)[:, t - win:].astype(F32)
        outs["kp"].append(kv_p[..., :D_KV].reshape(n_seq, win, N_KV_HEADS, HEAD_DIM))
        outs["vp"].append(kv_p[..., D_KV:].reshape(n_seq, win, N_KV_HEADS, HEAD_DIM))
        outs["kv_new"].append(kv[n_p:].reshape(n_dec, s_new, 2 * D_KV))

        oc_p = _conv_prompt(u, conv_w[l], conv_b[l], conv_ln_g[l], conv_ln_b[l], n_seq, t)
        u_s = u[n_p:].reshape(n_dec, s_new, D_CONV)
        oc_s = _conv_sample(conv_all, u_s.reshape(n_dec, s_new * D_CONV),
                            conv_w[l], conv_b[l], conv_ln_g[l], conv_ln_b[l], s_new, l)
        outs["cp"].append(u[:n_p].reshape(n_seq, t, D_CONV)[:, t - hist:])
        outs["u_new"].append(u_s)

        sp = _s5_params(p, l)
        os0, os1, hre, him = _s5_prompt(su, sp, t)
        su_tb = su[n_p:].reshape(n_dec, s_new, D_SSM).transpose(1, 0, 2).reshape(n_s, D_SSM)
        os_tb, hr_s, hi_s = _s5_sample(su_tb, state_ssm_re[l].reshape(n_dec, N_STATE),
                                       state_ssm_im[l].reshape(n_dec, N_STATE), sp, s_new)
        os_s = os_tb.reshape(s_new, n_dec, D_SSM).transpose(1, 0, 2).reshape(n_s, D_SSM)
        outs["rp"].append(hre.reshape(n_seq, SSM_GROUPS, SSM_STATE))
        outs["ip"].append(him.reshape(n_seq, SSM_GROUPS, SSM_STATE))
        outs["rs"].append(hr_s.reshape(n_dec, SSM_GROUPS, SSM_STATE))
        outs["is"].append(hi_s.reshape(n_dec, SSM_GROUPS, SSM_STATE))

        unused = LANES - N_EXPERT_GROUPS - N_EXPERTS
        wr = jnp.pad(jnp.concatenate([w_group_router[l], w_expert_router[l]], axis=1), ((0, 0), (0, unused)))
        wr_hi = wr.astype(BF16)
        wr_lo = (wr - wr_hi.astype(F32)).astype(BF16)
        br = jnp.pad(jnp.concatenate([b_group_router[l], b_expert_router[l]]), (0, unused)).reshape(1, LANES)
        x1, xn, tokc, tokl, tile_tbl, glob = _merge_out(
            [oa_p, oa_s], [oc_p, oc_s.reshape(n_s, D_CONV)], [os0, os1, os_s], x_src,
            grp_norm_g[l], w_out, l, ffn_norm_g[l], wr_hi, wr_lo, br)

        glob_flat = glob.reshape(SUBLANES * LANES)
        xs = _dispatch(tile_tbl, glob_flat, tokl, xn, n_tiles * TM_MOE + TRASH_ROWS)
        ys = _moe(glob_flat, xs, w_gate, w_up, w_down, l)
        res = _combine(tile_tbl, glob_flat, tokc, x1, ys, final_norm_g, n_p, final=(l == depth - 1))
        x_src = [res[0]]

    y_p = res[0].reshape(n_seq, t, D_MODEL)
    y_s = res[1].reshape(n_dec, s_new, D_MODEL)
    st = lambda k: jnp.stack(outs[k])
    kv_new = st("kv_new").astype(F32)
    heads = lambda z: z.reshape(depth, n_dec, s_new, N_KV_HEADS, HEAD_DIM)
    k_s, v_s = _shift_cache(cache_k, cache_v, heads(kv_new[..., :D_KV]), heads(kv_new[..., D_KV:]))
    conv_s = jnp.concatenate([state_conv[:, :, s_new:], st("u_new")], axis=2)
    return (y_p, y_s, st("kp"), st("vp"), st("cp"), st("rp"), st("ip"), k_s, v_s, conv_s, st("rs"), st("is"))
```

```python
import functools

import jax
import jax.numpy as jnp
from jax import lax
from jax.experimental import pallas as pl
from jax.experimental.pallas import tpu as pltpu

F32 = jnp.float32
BF16 = jnp.bfloat16
U32 = jnp.uint32
I32 = jnp.int32

D_MODEL = 1024
N_HEADS = 8
N_KV_HEADS = 2
HEAD_DIM = 64
WINDOW = 128
D_ATTN = N_HEADS * HEAD_DIM
D_KV = N_KV_HEADS * HEAD_DIM
D_CONV = 256
CONV_WIDTH = 31
D_SSM = 256
SSM_GROUPS = 16
SSM_GROUP_CH = 16
SSM_STATE = 64
N_STATE = SSM_GROUPS * SSM_STATE
D_IN = D_ATTN + 2 * D_KV + 2 * D_CONV + D_SSM
N_EXPERT_GROUPS = 4
EXPERTS_PER_GROUP = 8
N_EXPERTS = N_EXPERT_GROUPS * EXPERTS_PER_GROUP
D_EXPERT = 512
EPS = 1e-6
NEG_INF = -1e30
SCALE = HEAD_DIM ** -0.5

LANES = 128
SUBLANES = 8
HALF = LANES // 2
D_PACK = D_MODEL // 2

TM_IN = 512
TM_OUT = 256
TM_MOE = 512
CONV_T = 256
CONV_CHUNK = 64
CONV_HALO = 32
SCAN_T = 256
SCAN_PITCH = SCAN_T + SUBLANES
SAMPLE_BT = 16
ATTN_QB = 8
MERGE_SUB = 2
VMEM_LIMIT = 48 * 1024 * 1024

ROUTER_LANE0 = N_EXPERT_GROUPS
RUN = SUBLANES
N_LOCAL = 2 * TM_OUT + N_EXPERTS * RUN
TOK_W1, TOK_W2, TOK_P1, TOK_P2 = range(4)
TILE_NCHUNK, TILE_CHUNK_E, TILE_CHUNK_REL = range(3)
CHUNK_GROUP = 8
TRASH_ROWS = CHUNK_GROUP * RUN
GLOB_START, GLOB_ZERO, GLOB_TE, GLOB_TE_HI, GLOB_NUSED, GLOB_PTILES = range(6)


def _params(sem, vmem=VMEM_LIMIT):
    return pltpu.CompilerParams(dimension_semantics=sem, vmem_limit_bytes=vmem)


def _full(shape):
    zeros = (0,) * len(shape)
    return pl.BlockSpec(shape, lambda *_: zeros)


def _swap_halves(x):
    return jnp.concatenate([x[:, HALF:], x[:, :HALF]], axis=1)


def _rms(x, g):
    return x * lax.rsqrt(jnp.mean(x * x, axis=-1, keepdims=True) + EPS) * g


def _pack_pair(a, b):
    return pltpu.pack_elementwise([a, b], packed_dtype=BF16)


def _unpack_pair(p):
    return tuple(pltpu.unpack_elementwise(p, index=k, packed_dtype=BF16, unpacked_dtype=F32) for k in range(2))


def _pick(i, refs, starts):
    val = refs[0][...].astype(F32)
    for ref, start in zip(refs[1:], starts[1:]):
        val = jnp.where(i >= start, ref[...].astype(F32), val)
    return val


def _source_specs(sources, tm, width):
    specs, starts, start = [], [], 0
    for arr in sources:
        assert arr.shape[0] % tm == 0 and arr.shape[1] == width
        n_blk = arr.shape[0] // tm
        specs.append(pl.BlockSpec((tm, width), lambda i, s=start, nb=n_blk: (jnp.clip(i - s, 0, nb - 1), 0)))
        starts.append(start)
        start += n_blk
    return specs, tuple(starts), start


def _in_proj_kernel(*refs, starts):
    x_refs = refs[:len(starts)]
    g_ref, w_ref, q_ref, kv_ref, u_ref, su_ref, wbf_ref = refs[len(starts):]

    @pl.when(pl.program_id(0) == 0)
    def _():
        wbf_ref[...] = w_ref[...].astype(BF16)

    hn = _rms(_pick(pl.program_id(0), x_refs, starts), g_ref[...]).astype(BF16)
    z = jnp.dot(hn, wbf_ref[...], preferred_element_type=F32)
    q_ref[...] = z[:, :D_ATTN].astype(BF16)
    kv_ref[...] = z[:, D_ATTN:D_ATTN + 2 * D_KV].astype(BF16)
    c0 = D_ATTN + 2 * D_KV
    u_ref[...] = z[:, c0:c0 + D_CONV] * jax.nn.sigmoid(z[:, c0 + D_CONV:c0 + 2 * D_CONV])
    su_ref[...] = z[:, c0 + 2 * D_CONV:].astype(BF16)


def _layer_block(shape, layer):
    zeros = (0,) * len(shape)
    return pl.BlockSpec((None,) + tuple(shape), lambda *_: (layer,) + zeros)


def _in_proj(x_sources, g, w, layer):
    x_specs, starts, n_blk = _source_specs(x_sources, TM_IN, D_MODEL)
    n = n_blk * TM_IN
    row = lambda width: pl.BlockSpec((TM_IN, width), lambda i: (i, 0))
    return pl.pallas_call(
        functools.partial(_in_proj_kernel, starts=starts),
        grid=(n_blk,),
        in_specs=x_specs + [_full((1, D_MODEL)), _layer_block((D_MODEL, D_IN), layer)],
        out_specs=[row(D_ATTN), row(2 * D_KV), row(D_CONV), row(D_SSM)],
        out_shape=[jax.ShapeDtypeStruct((n, D_ATTN), BF16), jax.ShapeDtypeStruct((n, 2 * D_KV), BF16),
                   jax.ShapeDtypeStruct((n, D_CONV), F32), jax.ShapeDtypeStruct((n, D_SSM), BF16)],
        scratch_shapes=[pltpu.VMEM((D_MODEL, D_IN), BF16)],
        compiler_params=_params(("arbitrary",)),
        name="in_proj",
    )(*x_sources, g.reshape(1, D_MODEL), w)


def _softmax_pv(s, mask, sink, vmat):
    s = jnp.where(mask, s, NEG_INF)
    m = jnp.maximum(jnp.max(s, axis=-1, keepdims=True), sink)
    p = jnp.exp(s - m)
    denom = jnp.sum(p, axis=-1, keepdims=True) + jnp.exp(sink - m)
    return jnp.dot(p.astype(BF16), vmat, preferred_element_type=F32) / denom


def _attn_prompt_kernel(sink_ref, q_ref, kvc_ref, kvp_ref, o_ref):
    i = pl.program_id(1)
    lo = lax.broadcasted_iota(I32, (1, LANES), 1) < HALF
    a = lax.broadcasted_iota(I32, (WINDOW, 2 * WINDOW), 0)
    c = lax.broadcasted_iota(I32, (WINDOW, 2 * WINDOW), 1)
    diff = a + WINDOW - c
    band = (diff >= 0) & (diff < WINDOW)
    zero = jnp.zeros((WINDOW, LANES), BF16)
    for sub in range(ATTN_QB):
        rows = slice(sub * WINDOW, (sub + 1) * WINDOW)
        q = q_ref[rows, :] * jnp.asarray(SCALE, BF16)
        kvc = kvc_ref[rows, :]
        kvp = kvp_ref[...] if sub == 0 else kvc_ref[(sub - 1) * WINDOW:sub * WINDOW, :]
        mask = band & ((c >= WINDOW) | (i > 0)) if sub == 0 else band
        kk = jnp.concatenate([kvp[:, :LANES], kvc[:, :LANES]], axis=0)
        vv = jnp.concatenate([kvp[:, LANES:], kvc[:, LANES:]], axis=0)
        kk_sw = _swap_halves(kk)
        vv_sw = _swap_halves(vv)
        for j in range(D_ATTN // LANES):
            kvh = (2 * j) // (N_HEADS // N_KV_HEADS)
            qt = q[:, LANES * j:LANES * (j + 1)]
            mats = ((kk, vv), (kk_sw, vv_sw)) if kvh == 0 else ((kk_sw, vv_sw), (kk, vv))
            outs = []
            for par in range(2):
                kmat, vmat = mats[par]
                qm = jnp.where(lo if par == 0 else jnp.logical_not(lo), qt, zero)
                s = lax.dot_general(qm, kmat, (((1,), (1,)), ((), ())), preferred_element_type=F32)
                outs.append(_softmax_pv(s, mask, sink_ref[2 * j + par], vmat))
            o_ref[rows, LANES * j:LANES * (j + 1)] = jnp.where(lo, outs[0], outs[1]).astype(BF16)


def _attn_prompt(q, kv, sinks, n_seq, t):
    rows = ATTN_QB * WINDOW
    assert t % rows == 0
    nb = t // rows
    cur = lambda width: pl.BlockSpec((rows, width), lambda b, i: (b * nb + i, 0))
    prev = pl.BlockSpec((WINDOW, 2 * D_KV), lambda b, i: ((b * nb + i) * ATTN_QB - jnp.minimum(i, 1), 0))
    return pl.pallas_call(
        _attn_prompt_kernel,
        grid=(n_seq, nb),
        in_specs=[pl.BlockSpec(memory_space=pltpu.SMEM), cur(D_ATTN), cur(2 * D_KV), prev],
        out_specs=cur(D_ATTN),
        out_shape=jax.ShapeDtypeStruct((n_seq * t, D_ATTN), BF16),
        compiler_params=_params(("arbitrary", "arbitrary")),
        name="attn_prompt",
    )(sinks, q, kv, kv)


def _attn_sample_kernel(sink_ref, q_ref, kv_ref, ck_ref, cv_ref, o_ref, qf_ref, kvf_ref, *, s_new):
    rows = SUBLANES
    n_pair = q_ref.shape[0] // rows
    per = rows // s_new
    qf_ref[...] = q_ref[...].astype(F32) * SCALE
    kvf_ref[...] = kv_ref[...].astype(F32)
    lane = lax.broadcasted_iota(I32, (1, LANES), 1)
    lo = lane < HALF
    rid = lax.broadcasted_iota(I32, (N_HEADS * rows, 1), 0)
    head = rid // rows
    seq = (rid % rows) // s_new
    tok = rid % s_new
    sink = jnp.zeros((N_HEADS * rows, 1), F32)
    for h in range(N_HEADS):
        sink = jnp.where(head == h, sink_ref[h], sink)
    mask_c = lane > tok

    def pair(p, carry):
        r0 = pl.multiple_of(p * rows, rows)
        q8 = qf_ref[pl.ds(r0, rows), :]
        kv8 = kvf_ref[pl.ds(r0, rows), :]
        knew = kv8[:, :LANES]
        vnew = kv8[:, LANES:]
        pieces = []
        for h in range(N_HEADS):
            qt = q8[:, LANES * (h // 2):LANES * (h // 2 + 1)]
            tgt = h // (N_HEADS // N_KV_HEADS)
            if h % 2 != tgt:
                qt = _swap_halves(qt)
            pieces.append(jnp.where(lo if tgt == 0 else jnp.logical_not(lo), qt, 0.0))
        qm = jnp.concatenate(pieces, axis=0)
        qb = qm.astype(BF16)
        s_c = jnp.zeros((N_HEADS * rows, LANES), F32)
        for bb in range(per):
            kc = ck_ref[p * per + bb].astype(BF16)
            s_bb = lax.dot_general(qb, kc, (((1,), (1,)), ((), ())), preferred_element_type=F32)
            s_c = jnp.where(seq == bb, s_bb, s_c)
        s_c = jnp.where(mask_c, s_c, NEG_INF)
        m = jnp.maximum(jnp.max(s_c, axis=-1, keepdims=True), sink)
        s_n = []
        for k in range(rows):
            valid = (seq == k // s_new) & (tok >= k % s_new)
            sk = jnp.sum(qm * knew[k:k + 1, :], axis=-1, keepdims=True)
            sk = jnp.where(valid, sk, NEG_INF)
            s_n.append(sk)
            m = jnp.maximum(m, sk)
        p_c = jnp.exp(s_c - m)
        denom = jnp.sum(p_c, axis=-1, keepdims=True) + jnp.exp(sink - m)
        pb = p_c.astype(BF16)
        o = jnp.zeros((N_HEADS * rows, LANES), F32)
        for bb in range(per):
            vc = cv_ref[p * per + bb].astype(BF16)
            o = jnp.where(seq == bb, jnp.dot(pb, vc, preferred_element_type=F32), o)
        for k in range(rows):
            pk = jnp.exp(s_n[k] - m)
            denom = denom + pk
            o = o + pk.astype(BF16).astype(F32) * vnew[k:k + 1, :]
        o = o / denom
        for j in range(D_ATTN // LANES):
            kvh = (2 * j) // (N_HEADS // N_KV_HEADS)
            pe = o[rows * 2 * j:rows * (2 * j + 1), :]
            po = o[rows * (2 * j + 1):rows * (2 * j + 2), :]
            if kvh == 0:
                po = _swap_halves(po)
            else:
                pe = _swap_halves(pe)
            o_ref[pl.ds(r0, rows), LANES * j:LANES * (j + 1)] = jnp.where(lo, pe, po)
        return carry

    lax.fori_loop(0, n_pair, pair, 0)


def _attn_sample(q, kv, cache_k, cache_v, sinks, s_new, row0, n_seq, seq0):
    win = cache_k.shape[1]
    n_rows = n_seq * s_new
    rows_blk = SAMPLE_BT * s_new
    assert win == WINDOW and SUBLANES % s_new == 0 and n_seq % SAMPLE_BT == 0 and row0 % rows_blk == 0
    assert seq0 % SAMPLE_BT == 0
    blk0 = row0 // rows_blk
    cblk0 = seq0 // SAMPLE_BT
    return pl.pallas_call(
        functools.partial(_attn_sample_kernel, s_new=s_new),
        grid=(n_seq // SAMPLE_BT,),
        in_specs=[pl.BlockSpec(memory_space=pltpu.SMEM),
                  pl.BlockSpec((rows_blk, D_ATTN), lambda i: (blk0 + i, 0)),
                  pl.BlockSpec((rows_blk, 2 * D_KV), lambda i: (blk0 + i, 0)),
                  pl.BlockSpec((SAMPLE_BT, win, D_KV), lambda i: (cblk0 + i, 0, 0)),
                  pl.BlockSpec((SAMPLE_BT, win, D_KV), lambda i: (cblk0 + i, 0, 0))],
        out_specs=pl.BlockSpec((rows_blk, D_ATTN), lambda i: (i, 0)),
        out_shape=jax.ShapeDtypeStruct((n_rows, D_ATTN), F32),
        scratch_shapes=[pltpu.VMEM((rows_blk, D_ATTN), F32), pltpu.VMEM((rows_blk, 2 * D_KV), F32)],
        compiler_params=_params(("arbitrary",)),
        name="attn_sample",
    )(sinks, q, kv, cache_k, cache_v)


def _ln_silu(y, lg, lb):
    mu = jnp.mean(y, axis=-1, keepdims=True)
    var = jnp.mean(jnp.square(y - mu), axis=-1, keepdims=True)
    yn = (y - mu) * lax.rsqrt(var + EPS) * lg + lb
    return yn * jax.nn.sigmoid(yn)


def _conv_prompt_kernel(u_ref, w_ref, b_ref, lg_ref, lb_ref, o_ref, ext_ref):
    i = pl.program_id(1)

    @pl.when(i == 0)
    def _():
        ext_ref[0:CONV_HALO, :] = jnp.zeros((CONV_HALO, D_CONV), F32)
        ext_ref[CONV_HALO + CONV_T:, :] = jnp.zeros((SUBLANES, D_CONV), F32)

    @pl.when(i > 0)
    def _():
        ext_ref[0:CONV_HALO, :] = ext_ref[CONV_T:CONV_T + CONV_HALO, :]

    ext_ref[CONV_HALO:CONV_HALO + CONV_T, :] = u_ref[...]
    shift = CONV_HALO - (CONV_WIDTH - 1)
    for cidx in range(CONV_T // CONV_CHUNK):
        r0 = cidx * CONV_CHUNK
        acc = jnp.zeros((CONV_CHUNK, D_CONV), F32)
        for rho in range(SUBLANES):
            part = jnp.zeros((CONV_CHUNK + SUBLANES, D_CONV), F32)
            for j in range(CONV_WIDTH):
                if (j + shift) % SUBLANES == rho:
                    base = r0 + j + shift - rho
                    part = part + w_ref[j:j + 1, :] * ext_ref[base:base + CONV_CHUNK + SUBLANES, :]
            acc = acc + part[rho:rho + CONV_CHUNK, :]
        o_ref[r0:r0 + CONV_CHUNK, :] = _ln_silu(acc + b_ref[...], lg_ref[...], lb_ref[...]).astype(BF16)


def _conv_prompt(u, w, b, lg, lb, n_seq, t):
    nt = t // CONV_T
    vec = _full((1, D_CONV))
    return pl.pallas_call(
        _conv_prompt_kernel,
        grid=(n_seq, nt),
        in_specs=[pl.BlockSpec((CONV_T, D_CONV), lambda s, i: (s * nt + i, 0)),
                  _full((CONV_WIDTH, D_CONV)), vec, vec, vec],
        out_specs=pl.BlockSpec((CONV_T, D_CONV), lambda s, i: (s * nt + i, 0)),
        out_shape=jax.ShapeDtypeStruct((n_seq * t, D_CONV), BF16),
        scratch_shapes=[pltpu.VMEM((CONV_T + CONV_HALO + SUBLANES, D_CONV), F32)],
        compiler_params=_params(("arbitrary", "arbitrary")),
        name="conv_prompt",
    )(u, w, b.reshape(1, D_CONV), lg.reshape(1, D_CONV), lb.reshape(1, D_CONV))


def _conv_sample_kernel(st_ref, u_ref, w_ref, b_ref, lg_ref, lb_ref, o_ref, *, s_new):
    hist = CONV_WIDTH - 1
    for t in range(s_new):
        acc = jnp.zeros((st_ref.shape[0], D_CONV), F32)
        for j in range(CONV_WIDTH):
            idx = t + j
            if idx < hist:
                piece = st_ref[:, idx * D_CONV:(idx + 1) * D_CONV]
            else:
                piece = u_ref[:, (idx - hist) * D_CONV:(idx - hist + 1) * D_CONV]
            acc = acc + w_ref[j:j + 1, :] * piece
        o_ref[:, t * D_CONV:(t + 1) * D_CONV] = _ln_silu(acc + b_ref[...], lg_ref[...], lb_ref[...])


def _conv_sample(state2d, u2d, w, b, lg, lb, s_new, layer):
    n_seq = u2d.shape[0]
    vec = _full((1, D_CONV))
    return pl.pallas_call(
        functools.partial(_conv_sample_kernel, s_new=s_new),
        grid=(1,),
        in_specs=[pl.BlockSpec((n_seq, state2d.shape[1]), lambda i: (layer, 0)), _full(u2d.shape),
                  _full((CONV_WIDTH, D_CONV)), vec, vec, vec],
        out_specs=_full((n_seq, s_new * D_CONV)),
        out_shape=jax.ShapeDtypeStruct((n_seq, s_new * D_CONV), F32),
        compiler_params=_params(("arbitrary",)),
        name="conv_sample",
    )(state2d, u2d, w, b.reshape(1, D_CONV), lg.reshape(1, D_CONV), lb.reshape(1, D_CONV))


def _s5_discretize(a_re, a_im, log_dt):
    dt = jnp.exp(log_dt)
    mag = jnp.exp(a_re * dt)
    ang = a_im * dt
    lr = mag * jnp.cos(ang)
    li = mag * jnp.sin(ang)
    den = a_re * a_re + a_im * a_im
    cr = ((lr - 1.0) * a_re + li * a_im) / den
    ci = (li * a_re - (lr - 1.0) * a_im) / den
    return lr, li, cr, ci


def _s5_bbar(arow_ref, bre_ref, bim_ref):
    _, _, cr, ci = _s5_discretize(arow_ref[0:1, :], arow_ref[1:2, :], arow_ref[2:3, :])
    bre = bre_ref[...]
    bim = bim_ref[...]
    return (cr * bre - ci * bim).astype(BF16), (cr * bim + ci * bre).astype(BF16)


def _s5_readout(h_re, h_im, u, cre_ref, cim_ref, d_ref, gw_ref, gb_ref):
    y = (jnp.dot(h_re.astype(BF16), cre_ref[...], preferred_element_type=F32)
         - jnp.dot(h_im.astype(BF16), cim_ref[...], preferred_element_type=F32)
         + d_ref[...] * u.astype(F32))
    z = jax.nn.gelu(y)
    gate = jnp.dot(z.astype(BF16), gw_ref[...], preferred_element_type=F32) + gb_ref[...]
    return (z * jax.nn.sigmoid(gate)).astype(BF16)


def _s5_prompt_kernel(su0_ref, su1_ref, arow_ref, atile_ref, bre_ref, bim_ref, cre_ref, cim_ref, d_ref,
                      gw_ref, gb_ref, o0_ref, o1_ref, hre_ref, him_ref,
                      bbr_ref, bbi_ref, lam_ref, car_ref, bur_ref, bui_ref, hbr_ref, hbi_ref):
    i = pl.program_id(0)
    n_slab = N_STATE // LANES
    su_refs = (su0_ref, su1_ref)
    o_refs = (o0_ref, o1_ref)

    @pl.when(i == 0)
    def _():
        bbr, bbi = _s5_bbar(arow_ref, bre_ref, bim_ref)
        bbr_ref[...] = bbr
        bbi_ref[...] = bbi
        lr, li, _, _ = _s5_discretize(atile_ref[0], atile_ref[1], atile_ref[2])
        lam_ref[0] = lr
        lam_ref[1] = li
        car_ref[...] = jnp.zeros(car_ref.shape, F32)

    for s in range(2):
        u = su_refs[s][...]
        br = jnp.dot(u, bbr_ref[...], preferred_element_type=F32)
        bi = jnp.dot(u, bbi_ref[...], preferred_element_type=F32)
        for j in range(n_slab):
            bur_ref[s, j * SCAN_PITCH:j * SCAN_PITCH + SCAN_T, :] = br[:, LANES * j:LANES * (j + 1)]
            bui_ref[s, j * SCAN_PITCH:j * SCAN_PITCH + SCAN_T, :] = bi[:, LANES * j:LANES * (j + 1)]

    lr = lam_ref[0]
    li = lam_ref[1]

    def step(t, carry):
        new = []
        for s in range(2):
            hr, hi = carry[2 * s], carry[2 * s + 1]
            rows = pl.ds(t, n_slab, stride=SCAN_PITCH)
            nr = lr * hr - li * hi + bur_ref.at[s][rows, :]
            ni = lr * hi + li * hr + bui_ref.at[s][rows, :]
            hbr_ref.at[s][rows, :] = nr
            hbi_ref.at[s][rows, :] = ni
            new += [nr, ni]
        return tuple(new)

    carry = lax.fori_loop(0, SCAN_T, step, tuple(car_ref[k] for k in range(4)), unroll=8)
    for k in range(4):
        car_ref[k] = carry[k]

    @pl.when(i == pl.num_programs(0) - 1)
    def _():
        for s in range(2):
            hre_ref[s] = carry[2 * s]
            him_ref[s] = carry[2 * s + 1]

    for s in range(2):
        h_re = jnp.concatenate([hbr_ref[s, j * SCAN_PITCH:j * SCAN_PITCH + SCAN_T, :] for j in range(n_slab)], axis=1)
        h_im = jnp.concatenate([hbi_ref[s, j * SCAN_PITCH:j * SCAN_PITCH + SCAN_T, :] for j in range(n_slab)], axis=1)
        o_refs[s][...] = _s5_readout(h_re, h_im, su_refs[s][...], cre_ref, cim_ref, d_ref, gw_ref, gb_ref)


def _s5_prompt(su, prm, t):
    nt = t // SCAN_T
    n_slab = N_STATE // LANES
    blk0 = pl.BlockSpec((SCAN_T, D_SSM), lambda i: (i, 0))
    blk1 = pl.BlockSpec((SCAN_T, D_SSM), lambda i: (nt + i, 0))
    oblk = pl.BlockSpec((SCAN_T, D_SSM), lambda i: (i, 0))
    state = pl.BlockSpec((2, SUBLANES, LANES), lambda i: (0, 0, 0))
    slabs = pltpu.VMEM((2, n_slab * SCAN_PITCH, LANES), F32)
    return pl.pallas_call(
        _s5_prompt_kernel,
        grid=(nt,),
        in_specs=[blk0, blk1, _full((3, N_STATE)), _full((3, SUBLANES, LANES)),
                  _full((D_SSM, N_STATE)), _full((D_SSM, N_STATE)), _full((N_STATE, D_SSM)), _full((N_STATE, D_SSM)),
                  _full((1, D_SSM)), _full((D_SSM, D_SSM)), _full((1, D_SSM))],
        out_specs=[oblk, oblk, state, state],
        out_shape=[jax.ShapeDtypeStruct((t, D_SSM), BF16), jax.ShapeDtypeStruct((t, D_SSM), BF16),
                   jax.ShapeDtypeStruct((2, SUBLANES, LANES), F32), jax.ShapeDtypeStruct((2, SUBLANES, LANES), F32)],
        scratch_shapes=[pltpu.VMEM((D_SSM, N_STATE), BF16), pltpu.VMEM((D_SSM, N_STATE), BF16),
                        pltpu.VMEM((2, SUBLANES, LANES), F32), pltpu.VMEM((4, SUBLANES, LANES), F32),
                        slabs, slabs, slabs, slabs],
        compiler_params=_params(("arbitrary",)),
        name="s5_prompt",
    )(su, su, prm["arow"], prm["atile"], prm["bre"], prm["bim"], prm["cre"], prm["cim"],
      prm["d"], prm["gw"], prm["gb"])


def _s5_sample_kernel(su_ref, h0r_ref, h0i_ref, arow_ref, bre_ref, bim_ref, cre_ref, cim_ref, d_ref, gw_ref, gb_ref,
                      o_ref, hr_ref, hi_ref, hbr_ref, hbi_ref, *, s_new):
    n_seq = h0r_ref.shape[0]
    lr, li, _, _ = _s5_discretize(arow_ref[0:1, :], arow_ref[1:2, :], arow_ref[2:3, :])
    bbr, bbi = _s5_bbar(arow_ref, bre_ref, bim_ref)
    u = su_ref[...]
    hbr_ref[...] = jnp.dot(u, bbr, preferred_element_type=F32)
    hbi_ref[...] = jnp.dot(u, bbi, preferred_element_type=F32)
    hr_ref[...] = h0r_ref[...]
    hi_ref[...] = h0i_ref[...]
    for t in range(s_new):
        rows = slice(t * n_seq, (t + 1) * n_seq)
        hr = hr_ref[...]
        hi = hi_ref[...]
        nr = lr * hr - li * hi + hbr_ref[rows, :]
        ni = lr * hi + li * hr + hbi_ref[rows, :]
        hbr_ref[rows, :] = nr
        hbi_ref[rows, :] = ni
        hr_ref[...] = nr
        hi_ref[...] = ni
    o_ref[...] = _s5_readout(hbr_ref[...], hbi_ref[...], u, cre_ref, cim_ref, d_ref, gw_ref, gb_ref)


def _s5_sample(su_tb, h0r, h0i, prm, s_new):
    n_rows = su_tb.shape[0]
    n_seq = n_rows // s_new
    return pl.pallas_call(
        functools.partial(_s5_sample_kernel, s_new=s_new),
        out_shape=[jax.ShapeDtypeStruct((n_rows, D_SSM), BF16),
                   jax.ShapeDtypeStruct((n_seq, N_STATE), F32), jax.ShapeDtypeStruct((n_seq, N_STATE), F32)],
        scratch_shapes=[pltpu.VMEM((n_rows, N_STATE), F32), pltpu.VMEM((n_rows, N_STATE), F32)],
        compiler_params=pltpu.CompilerParams(vmem_limit_bytes=VMEM_LIMIT),
        name="s5_sample",
    )(su_tb, h0r, h0i, prm["arow"], prm["bre"], prm["bim"], prm["cre"], prm["cim"], prm["d"], prm["gw"], prm["gb"])


def _s5_params(p, l):
    eye = jnp.eye(SSM_GROUPS, dtype=F32)

    def b_diag(b):
        return jnp.einsum("gnc,gh->gchn", b, eye).reshape(D_SSM, N_STATE)

    def c_diag(c):
        return jnp.einsum("gcn,gh->gnhc", c, eye).reshape(N_STATE, D_SSM)

    ldt = jnp.broadcast_to(p["ssm_log_dt"][l][:, None], (SSM_GROUPS, SSM_STATE))
    a3 = jnp.stack([p["ssm_a_re"][l], p["ssm_a_im"][l], ldt])
    return {
        "arow": a3.reshape(3, N_STATE),
        "atile": a3.reshape(3, SUBLANES, LANES),
        "bre": b_diag(p["ssm_b_re"][l]), "bim": b_diag(p["ssm_b_im"][l]),
        "cre": c_diag(p["ssm_c_re"][l]).astype(BF16), "cim": c_diag(p["ssm_c_im"][l]).astype(BF16),
        "d": p["ssm_d"][l].reshape(1, D_SSM),
        "gw": p["ssm_glu_w"][l].astype(BF16), "gb": p["ssm_glu_b"][l].reshape(1, D_SSM),
    }


def _merge_out_kernel(*refs, starts):
    it = iter(refs)
    oa_refs, oc_refs, os_refs, x_refs = ([next(it) for _ in s] for s in starts)
    gn_ref, wo_ref, fg_ref, wrh_ref, wrl_ref, br_ref = (next(it) for _ in range(6))
    x1_ref, xn_ref, tokc_ref, tokl_ref, tile_ref, glob_ref = (next(it) for _ in range(6))
    carry_ref, wobf_ref = next(it), next(it)
    i = pl.program_id(0)
    tm = TM_OUT

    @pl.when(i == 0)
    def _():
        carry_ref[...] = jnp.zeros(carry_ref.shape, F32)
        wobf_ref[...] = wo_ref[...].astype(BF16)

    gn = gn_ref[...]
    lane = lax.broadcasted_iota(I32, (1, LANES), 1).astype(F32)
    far = float(LANES)
    is_group = lane < N_EXPERT_GROUPS
    r_i = lax.broadcasted_iota(I32, (tm, tm), 0)
    c_i = lax.broadcasted_iota(I32, (tm, tm), 1)
    lower = jnp.where(c_i < r_i, 1.0, 0.0).astype(BF16)
    lr = lax.broadcasted_iota(I32, (LANES, LANES), 0)
    lc = lax.broadcasted_iota(I32, (LANES, LANES), 1)
    before = jnp.where(lr < lc, 1.0, 0.0).astype(BF16)
    is_e_col = (lr >= ROUTER_LANE0) & (lr < ROUTER_LANE0 + N_EXPERTS)
    zrow = jnp.zeros((1, LANES), F32)
    row0 = lane * RUN

    def first_max(v):
        top = jnp.max(v, axis=-1, keepdims=True)
        return top, jnp.min(jnp.where(v == top, lane, far), axis=-1, keepdims=True)

    def at(sel, v):
        return jnp.sum(jnp.where(sel, v, 0.0), axis=-1, keepdims=True)

    oa_all = _pick(i, oa_refs, starts[0])
    oc_all = _pick(i, oc_refs, starts[1])
    os_all = _pick(i, os_refs, starts[2])
    x_all = _pick(i, x_refs, starts[3])
    carry = carry_ref[...]

    for sub in range(MERGE_SUB):
        rows_t = slice(sub * tm, (sub + 1) * tm)
        rows_8 = slice(sub * SUBLANES, (sub + 1) * SUBLANES)
        mix = jnp.concatenate([
            _rms(oa_all[rows_t], gn[:, :D_ATTN]),
            _rms(oc_all[rows_t], gn[:, D_ATTN:D_ATTN + D_CONV]),
            _rms(os_all[rows_t], gn[:, D_ATTN + D_CONV:]),
        ], axis=1).astype(BF16)
        x1 = x_all[rows_t] + jnp.dot(mix, wobf_ref[...], preferred_element_type=F32)
        x1_ref[rows_t, :] = x1
        xn = _rms(x1, fg_ref[...])
        xn_ref[rows_t, :] = _pack_pair(xn[:, :D_PACK], xn[:, D_PACK:])

        xh = xn.astype(BF16)
        xl = (xn - xh.astype(F32)).astype(BF16)
        wh = wrh_ref[...]
        logits = (jnp.dot(xh, wh, preferred_element_type=F32) + jnp.dot(xl, wh, preferred_element_type=F32)
                  + jnp.dot(xh, wrl_ref[...], preferred_element_type=F32) + br_ref[...])

        g_top, g_idx = first_max(jnp.where(is_group, logits, -jnp.inf))
        g_w = 1.0 / jnp.sum(jnp.where(is_group, jnp.exp(logits - g_top), 0.0), axis=-1, keepdims=True)
        e_lo = ROUTER_LANE0 + EXPERTS_PER_GROUP * g_idx
        el = jnp.where((lane >= e_lo) & (lane < e_lo + EXPERTS_PER_GROUP), logits, -jnp.inf)
        v1, i1 = first_max(el)
        v2, i2 = first_max(jnp.where(lane == i1, -jnp.inf, el))
        t2 = jnp.exp(v2 - v1)
        w1 = g_w / (1.0 + t2)
        w2 = g_w * t2 / (1.0 + t2)

        sel1 = lane == i1
        sel2 = lane == i2
        onehot = jnp.where(sel1 | sel2, 1.0, 0.0)
        rloc = jnp.dot(lower, onehot.astype(BF16), preferred_element_type=F32)
        cnt = jnp.sum(onehot, axis=0, keepdims=True)
        cnt_pad = jnp.floor((cnt + (RUN - 1.0)) * (1.0 / RUN)) * RUN
        stacked = jnp.concatenate([cnt_pad] + [zrow] * (SUBLANES - 1), axis=0).astype(BF16)
        lstart = jnp.dot(stacked, before, preferred_element_type=F32)[0:1, :]

        pos1, pos2 = at(sel1, lstart + rloc), at(sel2, lstart + rloc)
        tok = jnp.zeros((tm, LANES), F32)
        for col, val in ((TOK_W1, w1), (TOK_W2, w2), (TOK_P1, pos1), (TOK_P2, pos2)):
            tok = jnp.where(lane == col, val, tok)
        tokc_ref[rows_t, :] = tok[:, :tokc_ref.shape[1]]
        tokl_ref[rows_8, :] = jnp.transpose(tok)[:SUBLANES, :]

        per_expert = jnp.concatenate([lstart, lstart + cnt_pad, carry - lstart,
                                      jnp.zeros((LANES - 3, LANES), F32)], axis=0)
        cols = jnp.transpose(per_expert)
        own = is_e_col & (cols[:, 0:1] <= row0) & (row0 < cols[:, 1:2])
        chunk_e = jnp.sum(jnp.where(own, lr.astype(F32), 0.0), axis=0, keepdims=True)
        chunk_rel = row0 + jnp.sum(jnp.where(own, cols[:, 2:3], 0.0), axis=0, keepdims=True)
        n_chunk = jnp.sum(cnt_pad, axis=-1, keepdims=True) * (1.0 / RUN)
        rows = [zrow] * SUBLANES
        rows[TILE_NCHUNK] = jnp.broadcast_to(n_chunk, (1, LANES))
        rows[TILE_CHUNK_E], rows[TILE_CHUNK_REL] = chunk_e, chunk_rel
        tile_ref[rows_8, :] = jnp.concatenate(rows, axis=0).astype(I32)
        carry = carry + cnt_pad

    total = carry
    carry_ref[...] = total

    @pl.when(i == pl.num_programs(0) - 1)
    def _():
        is_e = (lane >= ROUTER_LANE0) & (lane < ROUTER_LANE0 + N_EXPERTS)
        ptiles = jnp.floor((total + (TM_MOE - 1.0)) * (1.0 / TM_MOE))
        upto = jnp.where(lr <= lc, 1.0, 0.0).astype(BF16)
        pt8 = jnp.concatenate([ptiles, jnp.zeros((SUBLANES - 1, LANES), F32)], axis=0).astype(BF16)
        tend = jnp.dot(pt8, upto, preferred_element_type=F32)[0:1, :]
        n_used = jnp.max(tend, axis=-1, keepdims=True)
        e_last = jnp.max(jnp.where(ptiles > 0.0, lane - ROUTER_LANE0, -1.0), axis=-1, keepdims=True)
        tend_col = jnp.transpose(jnp.broadcast_to(tend, (LANES, LANES)))

        def tile_expert(first_tile):
            hit = is_e_col & (tend_col <= lane + first_tile)
            return jnp.minimum(jnp.sum(jnp.where(hit, 1.0, 0.0), axis=0, keepdims=True), e_last)

        rows = [zrow] * SUBLANES
        rows[GLOB_START] = (tend - ptiles) * TM_MOE
        rows[GLOB_ZERO] = jnp.where(is_e & (total > 0.0), tend - 1.0, -1.0)
        rows[GLOB_TE] = tile_expert(0.0)
        rows[GLOB_TE_HI] = tile_expert(float(LANES))
        rows[GLOB_NUSED] = jnp.broadcast_to(n_used, (1, LANES))
        rows[GLOB_PTILES] = ptiles
        glob_ref[...] = jnp.concatenate(rows, axis=0).astype(I32)


def _merge_out(oa_src, oc_src, os_src, x_src, gn, wo, layer, fg, wr_hi, wr_lo, br):
    specs, starts, n_blk = [], [], None
    for src, width in ((oa_src, D_ATTN), (oc_src, D_CONV), (os_src, D_SSM), (x_src, D_MODEL)):
        sp, st, nb = _source_specs(src, MERGE_SUB * TM_OUT, width)
        assert n_blk in (None, nb)
        n_blk = nb
        specs += sp
        starts.append(st)
    n = n_blk * MERGE_SUB * TM_OUT
    row = lambda width: pl.BlockSpec((MERGE_SUB * TM_OUT, width), lambda i: (i, 0))
    tbl = lambda width: pl.BlockSpec((MERGE_SUB * SUBLANES, width), lambda i: (i, 0))
    return pl.pallas_call(
        functools.partial(_merge_out_kernel, starts=tuple(starts)),
        grid=(n_blk,),
        in_specs=specs + [_full((1, D_MODEL)), _layer_block((D_MODEL, D_MODEL), layer), _full((1, D_MODEL)),
                          _full((D_MODEL, LANES)), _full((D_MODEL, LANES)), _full((1, LANES))],
        out_specs=[row(D_MODEL), row(D_PACK), row(4), tbl(TM_OUT), tbl(LANES), _full((SUBLANES, LANES))],
        out_shape=[jax.ShapeDtypeStruct((n, D_MODEL), F32), jax.ShapeDtypeStruct((n, D_PACK), U32),
                   jax.ShapeDtypeStruct((n, 4), F32),
                   jax.ShapeDtypeStruct((n_blk * MERGE_SUB * SUBLANES, TM_OUT), F32),
                   jax.ShapeDtypeStruct((n_blk * MERGE_SUB * SUBLANES, LANES), I32),
                   jax.ShapeDtypeStruct((SUBLANES, LANES), I32)],
        scratch_shapes=[pltpu.VMEM((1, LANES), F32), pltpu.VMEM((D_MODEL, D_MODEL), BF16)],
        compiler_params=_params(("arbitrary",)),
        name="merge_out",
    )(*oa_src, *oc_src, *os_src, *x_src, gn.reshape(1, D_MODEL), wo, fg.reshape(1, D_MODEL), wr_hi, wr_lo, br)


def _chunk_groups(tile_ref):
    n_chunk = tile_ref[TILE_NCHUNK, 0]
    return lax.shift_right_logical(n_chunk + (CHUNK_GROUP - 1), CHUNK_GROUP.bit_length() - 1)


def _for_each_chunk(tile_ref, glob_ref, fn):
    n_chunk = tile_ref[TILE_NCHUNK, 0]
    n_group = _chunk_groups(tile_ref)

    def group(g, carry):
        for k in range(CHUNK_GROUP):
            c = g * CHUNK_GROUP + k
            seg = glob_ref[GLOB_START * LANES + tile_ref[TILE_CHUNK_E, c]]
            fn(pl.multiple_of(c * RUN, RUN), c < n_chunk, seg + tile_ref[TILE_CHUNK_REL, c], k)
        return carry

    lax.fori_loop(0, n_group, group, 0)
    return n_group * CHUNK_GROUP


def _wait_chunks(n, wait_one):
    def group(g, carry):
        for _ in range(CHUNK_GROUP):
            wait_one()
        return carry

    lax.fori_loop(0, lax.shift_right_logical(n, CHUNK_GROUP.bit_length() - 1), group, 0)


def _dispatch_kernel(tile_ref, glob_ref, tokl_ref, xn_ref, xs_ref, sbuf_ref, zbuf_ref, inflight_ref, sem_z, sem_r):
    i = pl.program_id(0)
    last = pl.num_programs(0) - 1
    slot = lax.rem(i, 2)
    trash0 = xs_ref.shape[0] - TRASH_ROWS

    def zero_copy(t):
        return pltpu.make_async_copy(zbuf_ref, xs_ref.at[pl.ds(pl.multiple_of(t * TM_MOE, TM_MOE), TM_MOE)], sem_z)

    def for_zero_tiles(fn):
        def seg_last(k, c):
            t = glob_ref[GLOB_ZERO * LANES + k]

            @pl.when(t >= 0)
            def _():
                fn(t)
            return c

        def unused(t, c):
            fn(t)
            return c

        lax.fori_loop(0, LANES, seg_last, 0)
        lax.fori_loop(glob_ref[GLOB_NUSED * LANES], trash0 // TM_MOE, unused, 0)

    @pl.when(i == 0)
    def _():
        zbuf_ref[...] = jnp.zeros(zbuf_ref.shape, U32)
        trash = pltpu.make_async_copy(zbuf_ref.at[pl.ds(0, TRASH_ROWS)], xs_ref.at[pl.ds(trash0, TRASH_ROWS)], sem_z)
        trash.start()
        for_zero_tiles(lambda t: zero_copy(t).start())
        for_zero_tiles(lambda t: zero_copy(t).wait())
        trash.wait()
        inflight_ref[0] = 0

    q = lax.broadcasted_iota(I32, (N_LOCAL, 1), 0).astype(F32)
    hit = (q == tokl_ref[TOK_P1:TOK_P1 + 1, :]) | (q == tokl_ref[TOK_P2:TOK_P2 + 1, :])
    sel = jnp.where(hit, 1.0, 0.0).astype(BF16)
    a, b = _unpack_pair(xn_ref[...])
    sa = jnp.dot(sel, a.astype(BF16), preferred_element_type=F32)
    sb = jnp.dot(sel, b.astype(BF16), preferred_element_type=F32)
    sbuf_ref[slot] = _pack_pair(sa, sb)

    def start_chunk(local_row, real, sorted_row, k):
        dst = pl.multiple_of(jnp.where(real, sorted_row, trash0 + k * RUN), RUN)
        pltpu.make_async_copy(sbuf_ref.at[slot, pl.ds(local_row, RUN)], xs_ref.at[pl.ds(dst, RUN)],
                              sem_r).start(priority=k % 2)

    def drain(n):
        _wait_chunks(n, lambda: pltpu.make_async_copy(sbuf_ref.at[0, pl.ds(0, RUN)], xs_ref.at[pl.ds(0, RUN)],
                                                      sem_r).wait())

    drain(inflight_ref[0])
    n_issued = _for_each_chunk(tile_ref, glob_ref, start_chunk)
    inflight_ref[0] = n_issued

    @pl.when(i == last)
    def _():
        drain(n_issued)


def _dispatch(tile_tbl, glob_flat, tokl, xn, n_rows_sorted):
    n = xn.shape[0]
    return pl.pallas_call(
        _dispatch_kernel,
        grid=(n // TM_OUT,),
        in_specs=[pl.BlockSpec((SUBLANES, LANES), lambda i: (i, 0), memory_space=pltpu.SMEM),
                  pl.BlockSpec(memory_space=pltpu.SMEM),
                  pl.BlockSpec((SUBLANES, TM_OUT), lambda i: (i, 0)),
                  pl.BlockSpec((TM_OUT, D_PACK), lambda i: (i, 0))],
        out_specs=pl.BlockSpec(memory_space=pl.ANY),
        out_shape=jax.ShapeDtypeStruct((n_rows_sorted, D_PACK), U32),
        scratch_shapes=[pltpu.VMEM((2, N_LOCAL, D_PACK), U32), pltpu.VMEM((TM_MOE, D_PACK), U32),
                        pltpu.SMEM((1,), I32), pltpu.SemaphoreType.DMA(()), pltpu.SemaphoreType.DMA(())],
        compiler_params=_params(("arbitrary",)),
        name="moe_dispatch",
    )(tile_tbl, glob_flat, tokl, xn)


def _tile_expert(glob_ref, i):
    return glob_ref[GLOB_TE * LANES + i]


def _moe_kernel(glob_ref, xs_ref, wg_hbm, wu_hbm, wd_hbm, ys_ref, wgb_ref, wub_ref, wdb_ref,
                sg_ref, su_ref, sd_ref, ord_ref, sem, *, layer):
    i = pl.program_id(0)
    used = i < glob_ref[GLOB_NUSED * LANES]
    expert = _tile_expert(glob_ref, i)
    new_expert = (i == 0) | (expert != _tile_expert(glob_ref, jnp.maximum(i - 1, 0)))

    def weight_copies(e, slot):
        return [pltpu.make_async_copy(src.at[layer, e], dst.at[slot], sem.at[slot])
                for src, dst in ((wg_hbm, sg_ref), (wu_hbm, su_ref), (wd_hbm, sd_ref))]

    @pl.when(i == 0)
    def _():
        ord_ref[0] = 0
        for cp in weight_copies(expert, 0):
            cp.start()

    @pl.when(used & new_expert)
    def _():
        slot = lax.rem(ord_ref[0], 2)
        for cp in weight_copies(expert, slot):
            cp.wait()
        nxt = lax.while_loop(
            lambda k: (k < N_EXPERTS) & (glob_ref[GLOB_PTILES * LANES + ROUTER_LANE0 + jnp.minimum(k, N_EXPERTS - 1)] == 0),
            lambda k: k + 1, expert + 1)

        @pl.when(nxt < N_EXPERTS)
        def _():
            for cp in weight_copies(nxt, 1 - slot):
                cp.start()

        wgb_ref[...] = sg_ref[slot].astype(BF16)
        wub_ref[...] = su_ref[slot].astype(BF16)
        wdb_ref[...] = sd_ref[slot].astype(BF16)
        ord_ref[0] = ord_ref[0] + 1

    @pl.when(used)
    def _():
        a, b = _unpack_pair(xs_ref[...])
        x = jnp.concatenate([a, b], axis=1).astype(BF16)
        gate = jnp.dot(x, wgb_ref[...], preferred_element_type=F32)
        up = jnp.dot(x, wub_ref[...], preferred_element_type=F32)
        h = (gate * jax.nn.sigmoid(gate) * up).astype(BF16)
        y = jnp.dot(h, wdb_ref[...], preferred_element_type=F32)
        ys_ref[...] = _pack_pair(y[:, :D_PACK], y[:, D_PACK:])

    @pl.when(jnp.logical_not(used))
    def _():
        zero = jnp.zeros(ys_ref.shape, F32)
        ys_ref[...] = _pack_pair(zero, zero)


def _moe(glob_flat, xs, wg, wu, wd, layer):
    n_tiles = (xs.shape[0] - TRASH_ROWS) // TM_MOE
    assert n_tiles <= 2 * LANES
    hbm = pl.BlockSpec(memory_space=pl.ANY)
    grid_spec = pltpu.PrefetchScalarGridSpec(
        num_scalar_prefetch=1,
        grid=(n_tiles,),
        in_specs=[pl.BlockSpec((TM_MOE, D_PACK), lambda i, g: (jnp.minimum(i, g[GLOB_NUSED * LANES] - 1), 0)),
                  hbm, hbm, hbm],
        out_specs=pl.BlockSpec((TM_MOE, D_PACK), lambda i, g: (i, 0)),
        scratch_shapes=[pltpu.VMEM((D_MODEL, D_EXPERT), BF16), pltpu.VMEM((D_MODEL, D_EXPERT), BF16),
                        pltpu.VMEM((D_EXPERT, D_MODEL), BF16),
                        pltpu.VMEM((2, D_MODEL, D_EXPERT), F32), pltpu.VMEM((2, D_MODEL, D_EXPERT), F32),
                        pltpu.VMEM((2, D_EXPERT, D_MODEL), F32), pltpu.SMEM((1,), I32),
                        pltpu.SemaphoreType.DMA((2,))],
    )
    return pl.pallas_call(
        functools.partial(_moe_kernel, layer=layer),
        grid_spec=grid_spec,
        out_shape=jax.ShapeDtypeStruct((n_tiles * TM_MOE, D_PACK), U32),
        compiler_params=_params(("arbitrary",)),
        name="moe_experts",
    )(glob_flat, xs, wg, wu, wd)


def _combine_kernel(tile_ref, next_ref, glob_ref, tokc_ref, x1_ref, ys_ref, fg_ref, *rest, n_first, final):
    n_out = 2 if final else 1
    out_refs = rest[:n_out]
    lbuf_ref, inflight_ref, sem = rest[n_out:]
    i = pl.program_id(0)
    last = pl.num_programs(0) - 1
    slot = lax.rem(i, 2)

    def gather(tbl_ref, dst_slot):
        def start(local_row, real, sorted_row, k):
            src = pl.multiple_of(jnp.where(real, sorted_row, 0), RUN)
            pltpu.make_async_copy(ys_ref.at[pl.ds(src, RUN)], lbuf_ref.at[dst_slot, pl.ds(local_row, RUN)],
                                  sem.at[dst_slot]).start(priority=k % 2)

        inflight_ref[dst_slot] = _for_each_chunk(tbl_ref, glob_ref, start)

    @pl.when(i == 0)
    def _():
        lbuf_ref[...] = jnp.zeros(lbuf_ref.shape, U32)
        gather(tile_ref, 0)

    @pl.when(i < last)
    def _():
        gather(next_ref, 1 - slot)

    _wait_chunks(inflight_ref[slot], lambda: pltpu.make_async_copy(
        ys_ref.at[pl.ds(0, RUN)], lbuf_ref.at[slot, pl.ds(0, RUN)], sem.at[slot]).wait())

    tokc = tokc_ref[...]
    w1 = tokc[:, TOK_W1:TOK_W1 + 1]
    w2 = tokc[:, TOK_W2:TOK_W2 + 1]
    col = lax.broadcasted_iota(I32, (1, N_LOCAL), 1).astype(F32)
    sel1 = jnp.where(col == tokc[:, TOK_P1:TOK_P1 + 1], 1.0, 0.0).astype(BF16)
    sel2 = jnp.where(col == tokc[:, TOK_P2:TOK_P2 + 1], 1.0, 0.0).astype(BF16)
    halves = []
    for part in _unpack_pair(lbuf_ref[slot]):
        rows = part.astype(BF16)
        halves.append(w1 * jnp.dot(sel1, rows, preferred_element_type=F32)
                      + w2 * jnp.dot(sel2, rows, preferred_element_type=F32))
    x2 = x1_ref[...] + jnp.concatenate(halves, axis=1)
    if not final:
        out_refs[0][...] = x2
    else:
        y = _rms(x2, fg_ref[...])

        @pl.when(i < n_first)
        def _():
            out_refs[0][...] = y

        @pl.when(i >= n_first)
        def _():
            out_refs[1][...] = y


def _combine(tile_tbl, glob_flat, tokc, x1, ys, fg, n_first_rows, final):
    n = x1.shape[0]
    n_blk = n // TM_OUT
    n_first = n_first_rows // TM_OUT
    assert n_first_rows % TM_OUT == 0
    row = lambda width: pl.BlockSpec((TM_OUT, width), lambda i: (i, 0))
    if final:
        out_specs = [pl.BlockSpec((TM_OUT, D_MODEL), lambda i: (jnp.minimum(i, n_first - 1), 0)),
                     pl.BlockSpec((TM_OUT, D_MODEL), lambda i: (jnp.maximum(i - n_first, 0), 0))]
        out_shape = [jax.ShapeDtypeStruct((n_first_rows, D_MODEL), F32),
                     jax.ShapeDtypeStruct((n - n_first_rows, D_MODEL), F32)]
    else:
        out_specs = [row(D_MODEL)]
        out_shape = [jax.ShapeDtypeStruct((n, D_MODEL), F32)]
    return pl.pallas_call(
        functools.partial(_combine_kernel, n_first=n_first, final=final),
        grid=(n_blk,),
        in_specs=[pl.BlockSpec((SUBLANES, LANES), lambda i: (i, 0), memory_space=pltpu.SMEM),
                  pl.BlockSpec((SUBLANES, LANES), lambda i: (jnp.minimum(i + 1, n_blk - 1), 0), memory_space=pltpu.SMEM),
                  pl.BlockSpec(memory_space=pltpu.SMEM),
                  row(4), row(D_MODEL), pl.BlockSpec(memory_space=pl.ANY), _full((1, D_MODEL))],
        out_specs=out_specs,
        out_shape=out_shape,
        scratch_shapes=[pltpu.VMEM((2, N_LOCAL, D_PACK), U32), pltpu.SMEM((2,), I32),
                        pltpu.SemaphoreType.DMA((2,))],
        compiler_params=_params(("arbitrary",)),
        name="moe_combine",
    )(tile_tbl, tile_tbl, glob_flat, tokc, x1, ys, fg.reshape(1, D_MODEL))


def _n_moe_tiles(n_tokens):
    n_runs = (n_tokens // TM_OUT) * N_EXPERTS
    return (2 * n_tokens + n_runs * (RUN - 1) + N_EXPERTS * (TM_MOE - 1)) // TM_MOE + 1


def kernel(x_prompt, x_sample, cache_k, cache_v, state_conv, state_ssm_re, state_ssm_im, attn_norm_g, w_in, attn_sinks, conv_w, conv_b, conv_ln_g, conv_ln_b, ssm_a_re, ssm_a_im, ssm_log_dt, ssm_b_re, ssm_b_im, ssm_c_re, ssm_c_im, ssm_d, ssm_glu_w, ssm_glu_b, grp_norm_g, w_out, ffn_norm_g, w_group_router, b_group_router, w_expert_router, b_expert_router, w_gate, w_up, w_down, final_norm_g):
    p = dict(ssm_a_re=ssm_a_re, ssm_a_im=ssm_a_im, ssm_log_dt=ssm_log_dt, ssm_b_re=ssm_b_re, ssm_b_im=ssm_b_im,
             ssm_c_re=ssm_c_re, ssm_c_im=ssm_c_im, ssm_d=ssm_d, ssm_glu_w=ssm_glu_w, ssm_glu_b=ssm_glu_b)
    depth = w_in.shape[0]
    n_seq, t, _ = x_prompt.shape
    n_dec, s_new, _ = x_sample.shape
    win = cache_k.shape[2]
    n_p = n_seq * t
    n_s = n_dec * s_new
    n = n_p + n_s
    assert n_seq == 2 and t % SCAN_T == 0 and t % CONV_T == 0 and n_p % TM_IN == 0 and n_s % TM_IN == 0
    hist = CONV_WIDTH - 1
    n_tiles = _n_moe_tiles(n)

    x_src = [x_prompt.reshape(n_p, D_MODEL), x_sample.reshape(n_s, D_MODEL)]
    outs = {k: [] for k in ("kp", "vp", "cp", "rp", "ip", "kv_new", "u_new", "rs", "is")}
    ck_all = cache_k.reshape(depth * n_dec, win, D_KV)
    cv_all = cache_v.reshape(depth * n_dec, win, D_KV)
    conv_all = state_conv.reshape(depth * n_dec, hist * D_CONV)
    for l in range(depth):
        q, kv, u, su = _in_proj(x_src, attn_norm_g[l], w_in, l)

        oa_p = _attn_prompt(q, kv, attn_sinks[l], n_seq, t)
        oa_s = _attn_sample(q, kv, ck_all, cv_all, attn_sinks[l], s_new, n_p, n_dec, l * n_dec)
        kv_p = kv[:n_p].reshape(n_seq, t, 2 * D_KV)[:, t - win:].astype(F32)
        outs["kp"].append(kv_p[..., :D_KV].reshape(n_seq, win, N_KV_HEADS, HEAD_DIM))
        outs["vp"].append(kv_p[..., D_KV:].reshape(n_seq, win, N_KV_HEADS, HEAD_DIM))
        outs["kv_new"].append(kv[n_p:].reshape(n_dec, s_new, 2 * D_KV))

        oc_p = _conv_prompt(u, conv_w[l], conv_b[l], conv_ln_g[l], conv_ln_b[l], n_seq, t)
        u_s = u[n_p:].reshape(n_dec, s_new, D_CONV)
        oc_s = _conv_sample(conv_all, u_s.reshape(n_dec, s_new * D_CONV),
                            conv_w[l], conv_b[l], conv_ln_g[l], conv_ln_b[l], s_new, l)
        outs["cp"].append(u[:n_p].reshape(n_seq, t, D_CONV)[:, t - hist:])
        outs["u_new"].append(u_s)

        sp = _s5_params(p, l)
        os0, os1, hre, him = _s5_prompt(su, sp, t)
        su_tb = su[n_p:].reshape(n_dec, s_new, D_SSM).transpose(1, 0, 2).reshape(n_s, D_SSM)
        os_tb, hr_s, hi_s = _s5_sample(su_tb, state_ssm_re[l].reshape(n_dec, N_STATE),
                                       state_ssm_im[l].reshape(n_dec, N_STATE), sp, s_new)
        os_s = os_tb.reshape(s_new, n_dec, D_SSM).transpose(1, 0, 2).reshape(n_s, D_SSM)
        outs["rp"].append(hre.reshape(n_seq, SSM_GROUPS, SSM_STATE))
        outs["ip"].append(him.reshape(n_seq, SSM_GROUPS, SSM_STATE))
        outs["rs"].append(hr_s.reshape(n_dec, SSM_GROUPS, SSM_STATE))
        outs["is"].append(hi_s.reshape(n_dec, SSM_GROUPS, SSM_STATE))

        unused = LANES - N_EXPERT_GROUPS - N_EXPERTS
        wr = jnp.pad(jnp.concatenate([w_group_router[l], w_expert_router[l]], axis=1), ((0, 0), (0, unused)))
        wr_hi = wr.astype(BF16)
        wr_lo = (wr - wr_hi.astype(F32)).astype(BF16)
        br = jnp.pad(jnp.concatenate([b_group_router[l], b_expert_router[l]]), (0, unused)).reshape(1, LANES)
        x1, xn, tokc, tokl, tile_tbl, glob = _merge_out(
            [oa_p, oa_s], [oc_p, oc_s.reshape(n_s, D_CONV)], [os0, os1, os_s], x_src,
            grp_norm_g[l], w_out, l, ffn_norm_g[l], wr_hi, wr_lo, br)

        glob_flat = glob.reshape(SUBLANES * LANES)
        xs = _dispatch(tile_tbl, glob_flat, tokl, xn, n_tiles * TM_MOE + TRASH_ROWS)
        ys = _moe(glob_flat, xs, w_gate, w_up, w_down, l)
        res = _combine(tile_tbl, glob_flat, tokc, x1, ys, final_norm_g, n_p, final=(l == depth - 1))
        x_src = [res[0]]

    y_p = res[0].reshape(n_seq, t, D_MODEL)
    y_s = res[1].reshape(n_dec, s_new, D_MODEL)
    st = lambda k: jnp.stack(outs[k])
    kv_new = st("kv_new").astype(F32)
    heads = lambda z: z.reshape(depth, n_dec, s_new, N_KV_HEADS, HEAD_DIM)
    k_s = jnp.concatenate([cache_k[:, :, s_new:], heads(kv_new[..., :D_KV])], axis=2)
    v_s = jnp.concatenate([cache_v[:, :, s_new:], heads(kv_new[..., D_KV:])], axis=2)
    conv_s = jnp.concatenate([state_conv[:, :, s_new:], st("u_new")], axis=2)
    return (y_p, y_s, st("kp"), st("vp"), st("cp"), st("rp"), st("ip"), k_s, v_s, conv_s, st("rs"), st("is"))
```

```python
import functools

import jax
import jax.numpy as jnp
from jax import lax
from jax.experimental import pallas as pl
from jax.experimental.pallas import tpu as pltpu

F32 = jnp.float32
BF16 = jnp.bfloat16
U32 = jnp.uint32
I32 = jnp.int32

D_MODEL = 1024
N_HEADS = 8
N_KV_HEADS = 2
HEAD_DIM = 64
WINDOW = 128
D_ATTN = N_HEADS * HEAD_DIM
D_KV = N_KV_HEADS * HEAD_DIM
D_CONV = 256
CONV_WIDTH = 31
D_SSM = 256
SSM_GROUPS = 16
SSM_GROUP_CH = 16
SSM_STATE = 64
N_STATE = SSM_GROUPS * SSM_STATE
D_IN = D_ATTN + 2 * D_KV + 2 * D_CONV + D_SSM
N_EXPERT_GROUPS = 4
EXPERTS_PER_GROUP = 8
N_EXPERTS = N_EXPERT_GROUPS * EXPERTS_PER_GROUP
D_EXPERT = 512
EPS = 1e-6
NEG_INF = -1e30
SCALE = HEAD_DIM ** -0.5

LANES = 128
SUBLANES = 8
HALF = LANES // 2
D_PACK = D_MODEL // 2

TM_IN = 512
TM_OUT = 256
TM_MOE = 512
CONV_T = 256
CONV_CHUNK = 64
CONV_HALO = 32
SCAN_T = 256
SCAN_PITCH = SCAN_T + SUBLANES
SAMPLE_BT = 16
ATTN_QB = 8
MERGE_SUB = 2
VMEM_LIMIT = 48 * 1024 * 1024

ROUTER_LANE0 = N_EXPERT_GROUPS
RUN = SUBLANES
N_LOCAL = 2 * TM_OUT + N_EXPERTS * RUN
TOK_W1, TOK_W2, TOK_P1, TOK_P2 = range(4)
TILE_NCHUNK, TILE_CHUNK_E, TILE_CHUNK_REL = range(3)
N_CHUNK = N_LOCAL // RUN
TRASH_ROWS = N_LOCAL
GLOB_START, GLOB_ZERO, GLOB_TE, GLOB_TE_HI, GLOB_NUSED, GLOB_PTILES = range(6)


def _params(sem, vmem=VMEM_LIMIT):
    return pltpu.CompilerParams(dimension_semantics=sem, vmem_limit_bytes=vmem)


def _full(shape):
    zeros = (0,) * len(shape)
    return pl.BlockSpec(shape, lambda *_: zeros)


def _swap_halves(x):
    return jnp.concatenate([x[:, HALF:], x[:, :HALF]], axis=1)


def _rms(x, g):
    return x * lax.rsqrt(jnp.mean(x * x, axis=-1, keepdims=True) + EPS) * g


def _pack_pair(a, b):
    return pltpu.pack_elementwise([a, b], packed_dtype=BF16)


def _unpack_pair(p):
    return tuple(pltpu.unpack_elementwise(p, index=k, packed_dtype=BF16, unpacked_dtype=F32) for k in range(2))


def _pick(i, refs, starts):
    val = refs[0][...].astype(F32)
    for ref, start in zip(refs[1:], starts[1:]):
        val = jnp.where(i >= start, ref[...].astype(F32), val)
    return val


def _source_specs(sources, tm, width):
    specs, starts, start = [], [], 0
    for arr in sources:
        assert arr.shape[0] % tm == 0 and arr.shape[1] == width
        n_blk = arr.shape[0] // tm
        specs.append(pl.BlockSpec((tm, width), lambda i, s=start, nb=n_blk: (jnp.clip(i - s, 0, nb - 1), 0)))
        starts.append(start)
        start += n_blk
    return specs, tuple(starts), start


def _in_proj_kernel(*refs, starts):
    x_refs = refs[:len(starts)]
    g_ref, w_ref, q_ref, kv_ref, u_ref, su_ref, wbf_ref = refs[len(starts):]

    @pl.when(pl.program_id(0) == 0)
    def _():
        wbf_ref[...] = w_ref[...].astype(BF16)

    hn = _rms(_pick(pl.program_id(0), x_refs, starts), g_ref[...]).astype(BF16)
    z = jnp.dot(hn, wbf_ref[...], preferred_element_type=F32)
    q_ref[...] = z[:, :D_ATTN].astype(BF16)
    kv_ref[...] = z[:, D_ATTN:D_ATTN + 2 * D_KV].astype(BF16)
    c0 = D_ATTN + 2 * D_KV
    u_ref[...] = z[:, c0:c0 + D_CONV] * jax.nn.sigmoid(z[:, c0 + D_CONV:c0 + 2 * D_CONV])
    su_ref[...] = z[:, c0 + 2 * D_CONV:].astype(BF16)


def _layer_block(shape, layer):
    zeros = (0,) * len(shape)
    return pl.BlockSpec((None,) + tuple(shape), lambda *_: (layer,) + zeros)


def _in_proj(x_sources, g, w, layer):
    x_specs, starts, n_blk = _source_specs(x_sources, TM_IN, D_MODEL)
    n = n_blk * TM_IN
    row = lambda width: pl.BlockSpec((TM_IN, width), lambda i: (i, 0))
    return pl.pallas_call(
        functools.partial(_in_proj_kernel, starts=starts),
        grid=(n_blk,),
        in_specs=x_specs + [_full((1, D_MODEL)), _layer_block((D_MODEL, D_IN), layer)],
        out_specs=[row(D_ATTN), row(2 * D_KV), row(D_CONV), row(D_SSM)],
        out_shape=[jax.ShapeDtypeStruct((n, D_ATTN), BF16), jax.ShapeDtypeStruct((n, 2 * D_KV), BF16),
                   jax.ShapeDtypeStruct((n, D_CONV), F32), jax.ShapeDtypeStruct((n, D_SSM), BF16)],
        scratch_shapes=[pltpu.VMEM((D_MODEL, D_IN), BF16)],
        compiler_params=_params(("arbitrary",)),
        name="in_proj",
    )(*x_sources, g.reshape(1, D_MODEL), w)


def _softmax_pv(s, mask, sink, vmat):
    s = jnp.where(mask, s, NEG_INF)
    m = jnp.maximum(jnp.max(s, axis=-1, keepdims=True), sink)
    p = jnp.exp(s - m)
    denom = jnp.sum(p, axis=-1, keepdims=True) + jnp.exp(sink - m)
    return jnp.dot(p.astype(BF16), vmat, preferred_element_type=F32) / denom


def _attn_prompt_kernel(sink_ref, q_ref, kvc_ref, kvp_ref, o_ref):
    i = pl.program_id(1)
    lo = lax.broadcasted_iota(I32, (1, LANES), 1) < HALF
    a = lax.broadcasted_iota(I32, (WINDOW, 2 * WINDOW), 0)
    c = lax.broadcasted_iota(I32, (WINDOW, 2 * WINDOW), 1)
    diff = a + WINDOW - c
    band = (diff >= 0) & (diff < WINDOW)
    zero = jnp.zeros((WINDOW, LANES), BF16)
    for sub in range(ATTN_QB):
        rows = slice(sub * WINDOW, (sub + 1) * WINDOW)
        q = q_ref[rows, :] * jnp.asarray(SCALE, BF16)
        kvc = kvc_ref[rows, :]
        kvp = kvp_ref[...] if sub == 0 else kvc_ref[(sub - 1) * WINDOW:sub * WINDOW, :]
        mask = band & ((c >= WINDOW) | (i > 0)) if sub == 0 else band
        kk = jnp.concatenate([kvp[:, :LANES], kvc[:, :LANES]], axis=0)
        vv = jnp.concatenate([kvp[:, LANES:], kvc[:, LANES:]], axis=0)
        kk_sw = _swap_halves(kk)
        vv_sw = _swap_halves(vv)
        for j in range(D_ATTN // LANES):
            kvh = (2 * j) // (N_HEADS // N_KV_HEADS)
            qt = q[:, LANES * j:LANES * (j + 1)]
            mats = ((kk, vv), (kk_sw, vv_sw)) if kvh == 0 else ((kk_sw, vv_sw), (kk, vv))
            outs = []
            for par in range(2):
                kmat, vmat = mats[par]
                qm = jnp.where(lo if par == 0 else jnp.logical_not(lo), qt, zero)
                s = lax.dot_general(qm, kmat, (((1,), (1,)), ((), ())), preferred_element_type=F32)
                outs.append(_softmax_pv(s, mask, sink_ref[2 * j + par], vmat))
            o_ref[rows, LANES * j:LANES * (j + 1)] = jnp.where(lo, outs[0], outs[1]).astype(BF16)


def _attn_prompt(q, kv, sinks, n_seq, t):
    rows = ATTN_QB * WINDOW
    assert t % rows == 0
    nb = t // rows
    cur = lambda width: pl.BlockSpec((rows, width), lambda b, i: (b * nb + i, 0))
    prev = pl.BlockSpec((WINDOW, 2 * D_KV), lambda b, i: ((b * nb + i) * ATTN_QB - jnp.minimum(i, 1), 0))
    return pl.pallas_call(
        _attn_prompt_kernel,
        grid=(n_seq, nb),
        in_specs=[pl.BlockSpec(memory_space=pltpu.SMEM), cur(D_ATTN), cur(2 * D_KV), prev],
        out_specs=cur(D_ATTN),
        out_shape=jax.ShapeDtypeStruct((n_seq * t, D_ATTN), BF16),
        compiler_params=_params(("arbitrary", "arbitrary")),
        name="attn_prompt",
    )(sinks, q, kv, kv)


def _attn_sample_kernel(sink_ref, q_ref, kv_ref, ck_ref, cv_ref, o_ref, qf_ref, kvf_ref, *, s_new):
    rows = SUBLANES
    n_pair = q_ref.shape[0] // rows
    per = rows // s_new
    qf_ref[...] = q_ref[...].astype(F32) * SCALE
    kvf_ref[...] = kv_ref[...].astype(F32)
    lane = lax.broadcasted_iota(I32, (1, LANES), 1)
    lo = lane < HALF
    rid = lax.broadcasted_iota(I32, (N_HEADS * rows, 1), 0)
    head = rid // rows
    seq = (rid % rows) // s_new
    tok = rid % s_new
    sink = jnp.zeros((N_HEADS * rows, 1), F32)
    for h in range(N_HEADS):
        sink = jnp.where(head == h, sink_ref[h], sink)
    mask_c = lane > tok

    def pair(p, carry):
        r0 = pl.multiple_of(p * rows, rows)
        q8 = qf_ref[pl.ds(r0, rows), :]
        kv8 = kvf_ref[pl.ds(r0, rows), :]
        knew = kv8[:, :LANES]
        vnew = kv8[:, LANES:]
        pieces = []
        for h in range(N_HEADS):
            qt = q8[:, LANES * (h // 2):LANES * (h // 2 + 1)]
            tgt = h // (N_HEADS // N_KV_HEADS)
            if h % 2 != tgt:
                qt = _swap_halves(qt)
            pieces.append(jnp.where(lo if tgt == 0 else jnp.logical_not(lo), qt, 0.0))
        qm = jnp.concatenate(pieces, axis=0)
        qb = qm.astype(BF16)
        s_c = jnp.zeros((N_HEADS * rows, LANES), F32)
        for bb in range(per):
            kc = ck_ref[p * per + bb].astype(BF16)
            s_bb = lax.dot_general(qb, kc, (((1,), (1,)), ((), ())), preferred_element_type=F32)
            s_c = jnp.where(seq == bb, s_bb, s_c)
        s_c = jnp.where(mask_c, s_c, NEG_INF)
        m = jnp.maximum(jnp.max(s_c, axis=-1, keepdims=True), sink)
        s_n = []
        for k in range(rows):
            valid = (seq == k // s_new) & (tok >= k % s_new)
            sk = jnp.sum(qm * knew[k:k + 1, :], axis=-1, keepdims=True)
            sk = jnp.where(valid, sk, NEG_INF)
            s_n.append(sk)
            m = jnp.maximum(m, sk)
        p_c = jnp.exp(s_c - m)
        denom = jnp.sum(p_c, axis=-1, keepdims=True) + jnp.exp(sink - m)
        pb = p_c.astype(BF16)
        o = jnp.zeros((N_HEADS * rows, LANES), F32)
        for bb in range(per):
            vc = cv_ref[p * per + bb].astype(BF16)
            o = jnp.where(seq == bb, jnp.dot(pb, vc, preferred_element_type=F32), o)
        for k in range(rows):
            pk = jnp.exp(s_n[k] - m)
            denom = denom + pk
            o = o + pk.astype(BF16).astype(F32) * vnew[k:k + 1, :]
        o = o / denom
        for j in range(D_ATTN // LANES):
            kvh = (2 * j) // (N_HEADS // N_KV_HEADS)
            pe = o[rows * 2 * j:rows * (2 * j + 1), :]
            po = o[rows * (2 * j + 1):rows * (2 * j + 2), :]
            if kvh == 0:
                po = _swap_halves(po)
            else:
                pe = _swap_halves(pe)
            o_ref[pl.ds(r0, rows), LANES * j:LANES * (j + 1)] = jnp.where(lo, pe, po)
        return carry

    lax.fori_loop(0, n_pair, pair, 0)


def _attn_sample(q, kv, cache_k, cache_v, sinks, s_new, row0, n_seq, seq0):
    win = cache_k.shape[1]
    n_rows = n_seq * s_new
    rows_blk = SAMPLE_BT * s_new
    assert win == WINDOW and SUBLANES % s_new == 0 and n_seq % SAMPLE_BT == 0 and row0 % rows_blk == 0
    assert seq0 % SAMPLE_BT == 0
    blk0 = row0 // rows_blk
    cblk0 = seq0 // SAMPLE_BT
    return pl.pallas_call(
        functools.partial(_attn_sample_kernel, s_new=s_new),
        grid=(n_seq // SAMPLE_BT,),
        in_specs=[pl.BlockSpec(memory_space=pltpu.SMEM),
                  pl.BlockSpec((rows_blk, D_ATTN), lambda i: (blk0 + i, 0)),
                  pl.BlockSpec((rows_blk, 2 * D_KV), lambda i: (blk0 + i, 0)),
                  pl.BlockSpec((SAMPLE_BT, win, D_KV), lambda i: (cblk0 + i, 0, 0)),
                  pl.BlockSpec((SAMPLE_BT, win, D_KV), lambda i: (cblk0 + i, 0, 0))],
        out_specs=pl.BlockSpec((rows_blk, D_ATTN), lambda i: (i, 0)),
        out_shape=jax.ShapeDtypeStruct((n_rows, D_ATTN), F32),
        scratch_shapes=[pltpu.VMEM((rows_blk, D_ATTN), F32), pltpu.VMEM((rows_blk, 2 * D_KV), F32)],
        compiler_params=_params(("arbitrary",)),
        name="attn_sample",
    )(sinks, q, kv, cache_k, cache_v)


def _ln_silu(y, lg, lb):
    mu = jnp.mean(y, axis=-1, keepdims=True)
    var = jnp.mean(jnp.square(y - mu), axis=-1, keepdims=True)
    yn = (y - mu) * lax.rsqrt(var + EPS) * lg + lb
    return yn * jax.nn.sigmoid(yn)


def _conv_prompt_kernel(u_ref, w_ref, b_ref, lg_ref, lb_ref, o_ref, ext_ref):
    i = pl.program_id(1)

    @pl.when(i == 0)
    def _():
        ext_ref[0:CONV_HALO, :] = jnp.zeros((CONV_HALO, D_CONV), F32)
        ext_ref[CONV_HALO + CONV_T:, :] = jnp.zeros((SUBLANES, D_CONV), F32)

    @pl.when(i > 0)
    def _():
        ext_ref[0:CONV_HALO, :] = ext_ref[CONV_T:CONV_T + CONV_HALO, :]

    ext_ref[CONV_HALO:CONV_HALO + CONV_T, :] = u_ref[...]
    shift = CONV_HALO - (CONV_WIDTH - 1)
    for cidx in range(CONV_T // CONV_CHUNK):
        r0 = cidx * CONV_CHUNK
        acc = jnp.zeros((CONV_CHUNK, D_CONV), F32)
        for rho in range(SUBLANES):
            part = jnp.zeros((CONV_CHUNK + SUBLANES, D_CONV), F32)
            for j in range(CONV_WIDTH):
                if (j + shift) % SUBLANES == rho:
                    base = r0 + j + shift - rho
                    part = part + w_ref[j:j + 1, :] * ext_ref[base:base + CONV_CHUNK + SUBLANES, :]
            acc = acc + part[rho:rho + CONV_CHUNK, :]
        o_ref[r0:r0 + CONV_CHUNK, :] = _ln_silu(acc + b_ref[...], lg_ref[...], lb_ref[...]).astype(BF16)


def _conv_prompt(u, w, b, lg, lb, n_seq, t):
    nt = t // CONV_T
    vec = _full((1, D_CONV))
    return pl.pallas_call(
        _conv_prompt_kernel,
        grid=(n_seq, nt),
        in_specs=[pl.BlockSpec((CONV_T, D_CONV), lambda s, i: (s * nt + i, 0)),
                  _full((CONV_WIDTH, D_CONV)), vec, vec, vec],
        out_specs=pl.BlockSpec((CONV_T, D_CONV), lambda s, i: (s * nt + i, 0)),
        out_shape=jax.ShapeDtypeStruct((n_seq * t, D_CONV), BF16),
        scratch_shapes=[pltpu.VMEM((CONV_T + CONV_HALO + SUBLANES, D_CONV), F32)],
        compiler_params=_params(("arbitrary", "arbitrary")),
        name="conv_prompt",
    )(u, w, b.reshape(1, D_CONV), lg.reshape(1, D_CONV), lb.reshape(1, D_CONV))


def _conv_sample_kernel(st_ref, u_ref, w_ref, b_ref, lg_ref, lb_ref, o_ref, *, s_new):
    hist = CONV_WIDTH - 1
    for t in range(s_new):
        acc = jnp.zeros((st_ref.shape[0], D_CONV), F32)
        for j in range(CONV_WIDTH):
            idx = t + j
            if idx < hist:
                piece = st_ref[:, idx * D_CONV:(idx + 1) * D_CONV]
            else:
                piece = u_ref[:, (idx - hist) * D_CONV:(idx - hist + 1) * D_CONV]
            acc = acc + w_ref[j:j + 1, :] * piece
        o_ref[:, t * D_CONV:(t + 1) * D_CONV] = _ln_silu(acc + b_ref[...], lg_ref[...], lb_ref[...])


def _conv_sample(state2d, u2d, w, b, lg, lb, s_new, layer):
    n_seq = u2d.shape[0]
    vec = _full((1, D_CONV))
    return pl.pallas_call(
        functools.partial(_conv_sample_kernel, s_new=s_new),
        grid=(1,),
        in_specs=[pl.BlockSpec((n_seq, state2d.shape[1]), lambda i: (layer, 0)), _full(u2d.shape),
                  _full((CONV_WIDTH, D_CONV)), vec, vec, vec],
        out_specs=_full((n_seq, s_new * D_CONV)),
        out_shape=jax.ShapeDtypeStruct((n_seq, s_new * D_CONV), F32),
        compiler_params=_params(("arbitrary",)),
        name="conv_sample",
    )(state2d, u2d, w, b.reshape(1, D_CONV), lg.reshape(1, D_CONV), lb.reshape(1, D_CONV))


def _s5_discretize(a_re, a_im, log_dt):
    dt = jnp.exp(log_dt)
    mag = jnp.exp(a_re * dt)
    ang = a_im * dt
    lr = mag * jnp.cos(ang)
    li = mag * jnp.sin(ang)
    den = a_re * a_re + a_im * a_im
    cr = ((lr - 1.0) * a_re + li * a_im) / den
    ci = (li * a_re - (lr - 1.0) * a_im) / den
    return lr, li, cr, ci


def _s5_bbar(arow_ref, bre_ref, bim_ref):
    _, _, cr, ci = _s5_discretize(arow_ref[0:1, :], arow_ref[1:2, :], arow_ref[2:3, :])
    bre = bre_ref[...]
    bim = bim_ref[...]
    return (cr * bre - ci * bim).astype(BF16), (cr * bim + ci * bre).astype(BF16)


def _s5_readout(h_re, h_im, u, cre_ref, cim_ref, d_ref, gw_ref, gb_ref):
    y = (jnp.dot(h_re.astype(BF16), cre_ref[...], preferred_element_type=F32)
         - jnp.dot(h_im.astype(BF16), cim_ref[...], preferred_element_type=F32)
         + d_ref[...] * u.astype(F32))
    z = jax.nn.gelu(y)
    gate = jnp.dot(z.astype(BF16), gw_ref[...], preferred_element_type=F32) + gb_ref[...]
    return (z * jax.nn.sigmoid(gate)).astype(BF16)


def _s5_prompt_kernel(su0_ref, su1_ref, arow_ref, atile_ref, bre_ref, bim_ref, cre_ref, cim_ref, d_ref,
                      gw_ref, gb_ref, o0_ref, o1_ref, hre_ref, him_ref,
                      bbr_ref, bbi_ref, lam_ref, car_ref, bur_ref, bui_ref, hbr_ref, hbi_ref):
    i = pl.program_id(0)
    n_slab = N_STATE // LANES
    su_refs = (su0_ref, su1_ref)
    o_refs = (o0_ref, o1_ref)

    @pl.when(i == 0)
    def _():
        bbr, bbi = _s5_bbar(arow_ref, bre_ref, bim_ref)
        bbr_ref[...] = bbr
        bbi_ref[...] = bbi
        lr, li, _, _ = _s5_discretize(atile_ref[0], atile_ref[1], atile_ref[2])
        lam_ref[0] = lr
        lam_ref[1] = li
        car_ref[...] = jnp.zeros(car_ref.shape, F32)

    for s in range(2):
        u = su_refs[s][...]
        br = jnp.dot(u, bbr_ref[...], preferred_element_type=F32)
        bi = jnp.dot(u, bbi_ref[...], preferred_element_type=F32)
        for j in range(n_slab):
            bur_ref[s, j * SCAN_PITCH:j * SCAN_PITCH + SCAN_T, :] = br[:, LANES * j:LANES * (j + 1)]
            bui_ref[s, j * SCAN_PITCH:j * SCAN_PITCH + SCAN_T, :] = bi[:, LANES * j:LANES * (j + 1)]

    lr = lam_ref[0]
    li = lam_ref[1]

    def step(t, carry):
        new = []
        for s in range(2):
            hr, hi = carry[2 * s], carry[2 * s + 1]
            rows = pl.ds(t, n_slab, stride=SCAN_PITCH)
            nr = lr * hr - li * hi + bur_ref.at[s][rows, :]
            ni = lr * hi + li * hr + bui_ref.at[s][rows, :]
            hbr_ref.at[s][rows, :] = nr
            hbi_ref.at[s][rows, :] = ni
            new += [nr, ni]
        return tuple(new)

    carry = lax.fori_loop(0, SCAN_T, step, tuple(car_ref[k] for k in range(4)), unroll=8)
    for k in range(4):
        car_ref[k] = carry[k]

    @pl.when(i == pl.num_programs(0) - 1)
    def _():
        for s in range(2):
            hre_ref[s] = carry[2 * s]
            him_ref[s] = carry[2 * s + 1]

    for s in range(2):
        h_re = jnp.concatenate([hbr_ref[s, j * SCAN_PITCH:j * SCAN_PITCH + SCAN_T, :] for j in range(n_slab)], axis=1)
        h_im = jnp.concatenate([hbi_ref[s, j * SCAN_PITCH:j * SCAN_PITCH + SCAN_T, :] for j in range(n_slab)], axis=1)
        o_refs[s][...] = _s5_readout(h_re, h_im, su_refs[s][...], cre_ref, cim_ref, d_ref, gw_ref, gb_ref)


def _s5_prompt(su, prm, t):
    nt = t // SCAN_T
    n_slab = N_STATE // LANES
    blk0 = pl.BlockSpec((SCAN_T, D_SSM), lambda i: (i, 0))
    blk1 = pl.BlockSpec((SCAN_T, D_SSM), lambda i: (nt + i, 0))
    oblk = pl.BlockSpec((SCAN_T, D_SSM), lambda i: (i, 0))
    state = pl.BlockSpec((2, SUBLANES, LANES), lambda i: (0, 0, 0))
    slabs = pltpu.VMEM((2, n_slab * SCAN_PITCH, LANES), F32)
    return pl.pallas_call(
        _s5_prompt_kernel,
        grid=(nt,),
        in_specs=[blk0, blk1, _full((3, N_STATE)), _full((3, SUBLANES, LANES)),
                  _full((D_SSM, N_STATE)), _full((D_SSM, N_STATE)), _full((N_STATE, D_SSM)), _full((N_STATE, D_SSM)),
                  _full((1, D_SSM)), _full((D_SSM, D_SSM)), _full((1, D_SSM))],
        out_specs=[oblk, oblk, state, state],
        out_shape=[jax.ShapeDtypeStruct((t, D_SSM), BF16), jax.ShapeDtypeStruct((t, D_SSM), BF16),
                   jax.ShapeDtypeStruct((2, SUBLANES, LANES), F32), jax.ShapeDtypeStruct((2, SUBLANES, LANES), F32)],
        scratch_shapes=[pltpu.VMEM((D_SSM, N_STATE), BF16), pltpu.VMEM((D_SSM, N_STATE), BF16),
                        pltpu.VMEM((2, SUBLANES, LANES), F32), pltpu.VMEM((4, SUBLANES, LANES), F32),
                        slabs, slabs, slabs, slabs],
        compiler_params=_params(("arbitrary",)),
        name="s5_prompt",
    )(su, su, prm["arow"], prm["atile"], prm["bre"], prm["bim"], prm["cre"], prm["cim"],
      prm["d"], prm["gw"], prm["gb"])


def _s5_sample_kernel(su_ref, h0r_ref, h0i_ref, arow_ref, bre_ref, bim_ref, cre_ref, cim_ref, d_ref, gw_ref, gb_ref,
                      o_ref, hr_ref, hi_ref, hbr_ref, hbi_ref, *, s_new):
    n_seq = h0r_ref.shape[0]
    lr, li, _, _ = _s5_discretize(arow_ref[0:1, :], arow_ref[1:2, :], arow_ref[2:3, :])
    bbr, bbi = _s5_bbar(arow_ref, bre_ref, bim_ref)
    u = su_ref[...]
    hbr_ref[...] = jnp.dot(u, bbr, preferred_element_type=F32)
    hbi_ref[...] = jnp.dot(u, bbi, preferred_element_type=F32)
    hr_ref[...] = h0r_ref[...]
    hi_ref[...] = h0i_ref[...]
    for t in range(s_new):
        rows = slice(t * n_seq, (t + 1) * n_seq)
        hr = hr_ref[...]
        hi = hi_ref[...]
        nr = lr * hr - li * hi + hbr_ref[rows, :]
        ni = lr * hi + li * hr + hbi_ref[rows, :]
        hbr_ref[rows, :] = nr
        hbi_ref[rows, :] = ni
        hr_ref[...] = nr
        hi_ref[...] = ni
    o_ref[...] = _s5_readout(hbr_ref[...], hbi_ref[...], u, cre_ref, cim_ref, d_ref, gw_ref, gb_ref)


def _s5_sample(su_tb, h0r, h0i, prm, s_new):
    n_rows = su_tb.shape[0]
    n_seq = n_rows // s_new
    return pl.pallas_call(
        functools.partial(_s5_sample_kernel, s_new=s_new),
        out_shape=[jax.ShapeDtypeStruct((n_rows, D_SSM), BF16),
                   jax.ShapeDtypeStruct((n_seq, N_STATE), F32), jax.ShapeDtypeStruct((n_seq, N_STATE), F32)],
        scratch_shapes=[pltpu.VMEM((n_rows, N_STATE), F32), pltpu.VMEM((n_rows, N_STATE), F32)],
        compiler_params=pltpu.CompilerParams(vmem_limit_bytes=VMEM_LIMIT),
        name="s5_sample",
    )(su_tb, h0r, h0i, prm["arow"], prm["bre"], prm["bim"], prm["cre"], prm["cim"], prm["d"], prm["gw"], prm["gb"])


def _s5_params(p, l):
    eye = jnp.eye(SSM_GROUPS, dtype=F32)

    def b_diag(b):
        return jnp.einsum("gnc,gh->gchn", b, eye).reshape(D_SSM, N_STATE)

    def c_diag(c):
        return jnp.einsum("gcn,gh->gnhc", c, eye).reshape(N_STATE, D_SSM)

    ldt = jnp.broadcast_to(p["ssm_log_dt"][l][:, None], (SSM_GROUPS, SSM_STATE))
    a3 = jnp.stack([p["ssm_a_re"][l], p["ssm_a_im"][l], ldt])
    return {
        "arow": a3.reshape(3, N_STATE),
        "atile": a3.reshape(3, SUBLANES, LANES),
        "bre": b_diag(p["ssm_b_re"][l]), "bim": b_diag(p["ssm_b_im"][l]),
        "cre": c_diag(p["ssm_c_re"][l]).astype(BF16), "cim": c_diag(p["ssm_c_im"][l]).astype(BF16),
        "d": p["ssm_d"][l].reshape(1, D_SSM),
        "gw": p["ssm_glu_w"][l].astype(BF16), "gb": p["ssm_glu_b"][l].reshape(1, D_SSM),
    }


def _merge_out_kernel(*refs, starts):
    it = iter(refs)
    oa_refs, oc_refs, os_refs, x_refs = ([next(it) for _ in s] for s in starts)
    gn_ref, wo_ref, fg_ref, wrh_ref, wrl_ref, br_ref = (next(it) for _ in range(6))
    x1_ref, xn_ref, tokc_ref, tokl_ref, tile_ref, glob_ref = (next(it) for _ in range(6))
    carry_ref, wobf_ref = next(it), next(it)
    i = pl.program_id(0)
    tm = TM_OUT

    @pl.when(i == 0)
    def _():
        carry_ref[...] = jnp.zeros(carry_ref.shape, F32)
        wobf_ref[...] = wo_ref[...].astype(BF16)

    gn = gn_ref[...]
    lane = lax.broadcasted_iota(I32, (1, LANES), 1).astype(F32)
    far = float(LANES)
    is_group = lane < N_EXPERT_GROUPS
    r_i = lax.broadcasted_iota(I32, (tm, tm), 0)
    c_i = lax.broadcasted_iota(I32, (tm, tm), 1)
    lower = jnp.where(c_i < r_i, 1.0, 0.0).astype(BF16)
    lr = lax.broadcasted_iota(I32, (LANES, LANES), 0)
    lc = lax.broadcasted_iota(I32, (LANES, LANES), 1)
    before = jnp.where(lr < lc, 1.0, 0.0).astype(BF16)
    is_e_col = (lr >= ROUTER_LANE0) & (lr < ROUTER_LANE0 + N_EXPERTS)
    zrow = jnp.zeros((1, LANES), F32)
    row0 = lane * RUN

    def first_max(v):
        top = jnp.max(v, axis=-1, keepdims=True)
        return top, jnp.min(jnp.where(v == top, lane, far), axis=-1, keepdims=True)

    def at(sel, v):
        return jnp.sum(jnp.where(sel, v, 0.0), axis=-1, keepdims=True)

    oa_all = _pick(i, oa_refs, starts[0])
    oc_all = _pick(i, oc_refs, starts[1])
    os_all = _pick(i, os_refs, starts[2])
    x_all = _pick(i, x_refs, starts[3])
    carry = carry_ref[...]

    for sub in range(MERGE_SUB):
        rows_t = slice(sub * tm, (sub + 1) * tm)
        rows_8 = slice(sub * SUBLANES, (sub + 1) * SUBLANES)
        mix = jnp.concatenate([
            _rms(oa_all[rows_t], gn[:, :D_ATTN]),
            _rms(oc_all[rows_t], gn[:, D_ATTN:D_ATTN + D_CONV]),
            _rms(os_all[rows_t], gn[:, D_ATTN + D_CONV:]),
        ], axis=1).astype(BF16)
        x1 = x_all[rows_t] + jnp.dot(mix, wobf_ref[...], preferred_element_type=F32)
        x1_ref[rows_t, :] = x1
        xn = _rms(x1, fg_ref[...])
        xn_ref[rows_t, :] = _pack_pair(xn[:, :D_PACK], xn[:, D_PACK:])

        xh = xn.astype(BF16)
        xl = (xn - xh.astype(F32)).astype(BF16)
        wh = wrh_ref[...]
        logits = (jnp.dot(xh, wh, preferred_element_type=F32) + jnp.dot(xl, wh, preferred_element_type=F32)
                  + jnp.dot(xh, wrl_ref[...], preferred_element_type=F32) + br_ref[...])

        g_top, g_idx = first_max(jnp.where(is_group, logits, -jnp.inf))
        g_w = 1.0 / jnp.sum(jnp.where(is_group, jnp.exp(logits - g_top), 0.0), axis=-1, keepdims=True)
        e_lo = ROUTER_LANE0 + EXPERTS_PER_GROUP * g_idx
        el = jnp.where((lane >= e_lo) & (lane < e_lo + EXPERTS_PER_GROUP), logits, -jnp.inf)
        v1, i1 = first_max(el)
        v2, i2 = first_max(jnp.where(lane == i1, -jnp.inf, el))
        t2 = jnp.exp(v2 - v1)
        w1 = g_w / (1.0 + t2)
        w2 = g_w * t2 / (1.0 + t2)

        sel1 = lane == i1
        sel2 = lane == i2
        onehot = jnp.where(sel1 | sel2, 1.0, 0.0)
        rloc = jnp.dot(lower, onehot.astype(BF16), preferred_element_type=F32)
        cnt = jnp.sum(onehot, axis=0, keepdims=True)
        cnt_pad = jnp.floor((cnt + (RUN - 1.0)) * (1.0 / RUN)) * RUN
        stacked = jnp.concatenate([cnt_pad] + [zrow] * (SUBLANES - 1), axis=0).astype(BF16)
        lstart = jnp.dot(stacked, before, preferred_element_type=F32)[0:1, :]

        pos1, pos2 = at(sel1, lstart + rloc), at(sel2, lstart + rloc)
        tok = jnp.zeros((tm, LANES), F32)
        for col, val in ((TOK_W1, w1), (TOK_W2, w2), (TOK_P1, pos1), (TOK_P2, pos2)):
            tok = jnp.where(lane == col, val, tok)
        tokc_ref[rows_t, :] = tok[:, :tokc_ref.shape[1]]
        tokl_ref[rows_8, :] = jnp.transpose(tok)[:SUBLANES, :]

        per_expert = jnp.concatenate([lstart, lstart + cnt_pad, carry - lstart,
                                      jnp.zeros((LANES - 3, LANES), F32)], axis=0)
        cols = jnp.transpose(per_expert)
        own = is_e_col & (cols[:, 0:1] <= row0) & (row0 < cols[:, 1:2])
        chunk_e = jnp.sum(jnp.where(own, lr.astype(F32), 0.0), axis=0, keepdims=True)
        chunk_rel = row0 + jnp.sum(jnp.where(own, cols[:, 2:3], 0.0), axis=0, keepdims=True)
        n_chunk = jnp.sum(cnt_pad, axis=-1, keepdims=True) * (1.0 / RUN)
        rows = [zrow] * SUBLANES
        rows[TILE_NCHUNK] = jnp.broadcast_to(n_chunk, (1, LANES))
        rows[TILE_CHUNK_E], rows[TILE_CHUNK_REL] = chunk_e, chunk_rel
        tile_ref[rows_8, :] = jnp.concatenate(rows, axis=0).astype(I32)
        carry = carry + cnt_pad

    total = carry
    carry_ref[...] = total

    @pl.when(i == pl.num_programs(0) - 1)
    def _():
        is_e = (lane >= ROUTER_LANE0) & (lane < ROUTER_LANE0 + N_EXPERTS)
        ptiles = jnp.floor((total + (TM_MOE - 1.0)) * (1.0 / TM_MOE))
        upto = jnp.where(lr <= lc, 1.0, 0.0).astype(BF16)
        pt8 = jnp.concatenate([ptiles, jnp.zeros((SUBLANES - 1, LANES), F32)], axis=0).astype(BF16)
        tend = jnp.dot(pt8, upto, preferred_element_type=F32)[0:1, :]
        n_used = jnp.max(tend, axis=-1, keepdims=True)
        e_last = jnp.max(jnp.where(ptiles > 0.0, lane - ROUTER_LANE0, -1.0), axis=-1, keepdims=True)
        tend_col = jnp.transpose(jnp.broadcast_to(tend, (LANES, LANES)))

        def tile_expert(first_tile):
            hit = is_e_col & (tend_col <= lane + first_tile)
            return jnp.minimum(jnp.sum(jnp.where(hit, 1.0, 0.0), axis=0, keepdims=True), e_last)

        rows = [zrow] * SUBLANES
        rows[GLOB_START] = (tend - ptiles) * TM_MOE
        rows[GLOB_ZERO] = jnp.where(is_e & (total > 0.0), tend - 1.0, -1.0)
        rows[GLOB_TE] = tile_expert(0.0)
        rows[GLOB_TE_HI] = tile_expert(float(LANES))
        rows[GLOB_NUSED] = jnp.broadcast_to(n_used, (1, LANES))
        rows[GLOB_PTILES] = ptiles
        glob_ref[...] = jnp.concatenate(rows, axis=0).astype(I32)


def _merge_out(oa_src, oc_src, os_src, x_src, gn, wo, layer, fg, wr_hi, wr_lo, br):
    specs, starts, n_blk = [], [], None
    for src, width in ((oa_src, D_ATTN), (oc_src, D_CONV), (os_src, D_SSM), (x_src, D_MODEL)):
        sp, st, nb = _source_specs(src, MERGE_SUB * TM_OUT, width)
        assert n_blk in (None, nb)
        n_blk = nb
        specs += sp
        starts.append(st)
    n = n_blk * MERGE_SUB * TM_OUT
    row = lambda width: pl.BlockSpec((MERGE_SUB * TM_OUT, width), lambda i: (i, 0))
    tbl = lambda width: pl.BlockSpec((MERGE_SUB * SUBLANES, width), lambda i: (i, 0))
    return pl.pallas_call(
        functools.partial(_merge_out_kernel, starts=tuple(starts)),
        grid=(n_blk,),
        in_specs=specs + [_full((1, D_MODEL)), _layer_block((D_MODEL, D_MODEL), layer), _full((1, D_MODEL)),
                          _full((D_MODEL, LANES)), _full((D_MODEL, LANES)), _full((1, LANES))],
        out_specs=[row(D_MODEL), row(D_PACK), row(4), tbl(TM_OUT), tbl(LANES), _full((SUBLANES, LANES))],
        out_shape=[jax.ShapeDtypeStruct((n, D_MODEL), F32), jax.ShapeDtypeStruct((n, D_PACK), U32),
                   jax.ShapeDtypeStruct((n, 4), F32),
                   jax.ShapeDtypeStruct((n_blk * MERGE_SUB * SUBLANES, TM_OUT), F32),
                   jax.ShapeDtypeStruct((n_blk * MERGE_SUB * SUBLANES, LANES), I32),
                   jax.ShapeDtypeStruct((SUBLANES, LANES), I32)],
        scratch_shapes=[pltpu.VMEM((1, LANES), F32), pltpu.VMEM((D_MODEL, D_MODEL), BF16)],
        compiler_params=_params(("arbitrary",)),
        name="merge_out",
    )(*oa_src, *oc_src, *os_src, *x_src, gn.reshape(1, D_MODEL), wo, fg.reshape(1, D_MODEL), wr_hi, wr_lo, br)


def _for_each_chunk(tile_ref, glob_ref, fn):
    n_chunk = tile_ref[TILE_NCHUNK, 0]
    for c in range(N_CHUNK):
        seg = glob_ref[GLOB_START * LANES + tile_ref[TILE_CHUNK_E, c]]
        fn(c * RUN, c < n_chunk, seg + tile_ref[TILE_CHUNK_REL, c], c)


def _wait_chunks(wait_one):
    for _ in range(N_CHUNK):
        wait_one()


def _dispatch_kernel(prev_ref, tile_ref, glob_ref, tokl_ref, xn_ref, xs_ref, sbuf_ref, zbuf_ref, sem_z, sem_r):
    i = pl.program_id(0)
    last = pl.num_programs(0) - 1
    slot = lax.rem(i, 2)
    trash0 = xs_ref.shape[0] - TRASH_ROWS

    def zero_copy(t):
        return pltpu.make_async_copy(zbuf_ref, xs_ref.at[pl.ds(pl.multiple_of(t * TM_MOE, TM_MOE), TM_MOE)], sem_z)

    def for_zero_tiles(fn):
        def seg_last(k, c):
            t = glob_ref[GLOB_ZERO * LANES + k]

            @pl.when(t >= 0)
            def _():
                fn(t)
            return c

        def unused(t, c):
            fn(t)
            return c

        lax.fori_loop(0, LANES, seg_last, 0)
        lax.fori_loop(glob_ref[GLOB_NUSED * LANES], trash0 // TM_MOE, unused, 0)

    def issue(tbl_ref, src_slot, enabled):
        def start(local_row, real, sorted_row, c):
            ok = real if enabled is True else real & enabled
            dst = pl.multiple_of(jnp.where(ok, sorted_row, trash0 + c * RUN), RUN)
            pltpu.make_async_copy(sbuf_ref.at[src_slot, pl.ds(local_row, RUN)], xs_ref.at[pl.ds(dst, RUN)],
                                  sem_r).start(priority=c % 2)

        _for_each_chunk(tbl_ref, glob_ref, start)

    def drain():
        _wait_chunks(lambda: pltpu.make_async_copy(sbuf_ref.at[0, pl.ds(0, RUN)], xs_ref.at[pl.ds(0, RUN)],
                                                   sem_r).wait())

    @pl.when(i == 0)
    def _():
        zbuf_ref[...] = jnp.zeros(zbuf_ref.shape, U32)
        sbuf_ref[...] = jnp.zeros(sbuf_ref.shape, U32)
        trash = [pltpu.make_async_copy(zbuf_ref.at[pl.ds(0, n)], xs_ref.at[pl.ds(trash0 + r0, n)], sem_z)
                 for r0, n in ((0, TM_MOE), (TM_MOE, TRASH_ROWS - TM_MOE))]
        for cp in trash:
            cp.start()
        for_zero_tiles(lambda t: zero_copy(t).start())
        for_zero_tiles(lambda t: zero_copy(t).wait())
        for cp in trash:
            cp.wait()

    @pl.when(i > 0)
    def _():
        drain()

    issue(prev_ref, 1 - slot, i > 0)

    q = lax.broadcasted_iota(I32, (N_LOCAL, 1), 0).astype(F32)
    hit = (q == tokl_ref[TOK_P1:TOK_P1 + 1, :]) | (q == tokl_ref[TOK_P2:TOK_P2 + 1, :])
    sel = jnp.where(hit, 1.0, 0.0).astype(BF16)
    a, b = _unpack_pair(xn_ref[...])
    sa = jnp.dot(sel, a.astype(BF16), preferred_element_type=F32)
    sb = jnp.dot(sel, b.astype(BF16), preferred_element_type=F32)
    sbuf_ref[slot] = _pack_pair(sa, sb)

    @pl.when(i == last)
    def _():
        drain()
        issue(tile_ref, slot, True)
        drain()


def _dispatch(tile_tbl, glob_flat, tokl, xn, n_rows_sorted):
    n = xn.shape[0]
    assert TM_MOE < TRASH_ROWS <= 2 * TM_MOE
    tbl = lambda f: pl.BlockSpec((SUBLANES, LANES), f, memory_space=pltpu.SMEM)
    return pl.pallas_call(
        _dispatch_kernel,
        grid=(n // TM_OUT,),
        in_specs=[tbl(lambda i: (jnp.maximum(i - 1, 0), 0)), tbl(lambda i: (i, 0)),
                  pl.BlockSpec(memory_space=pltpu.SMEM),
                  pl.BlockSpec((SUBLANES, TM_OUT), lambda i: (i, 0)),
                  pl.BlockSpec((TM_OUT, D_PACK), lambda i: (i, 0))],
        out_specs=pl.BlockSpec(memory_space=pl.ANY),
        out_shape=jax.ShapeDtypeStruct((n_rows_sorted, D_PACK), U32),
        scratch_shapes=[pltpu.VMEM((2, N_LOCAL, D_PACK), U32), pltpu.VMEM((TM_MOE, D_PACK), U32),
                        pltpu.SemaphoreType.DMA(()), pltpu.SemaphoreType.DMA(())],
        compiler_params=_params(("arbitrary",)),
        name="moe_dispatch",
    )(tile_tbl, tile_tbl, glob_flat, tokl, xn)


def _tile_expert(glob_ref, i):
    return glob_ref[GLOB_TE * LANES + i]


def _moe_kernel(glob_ref, xs_ref, wg_hbm, wu_hbm, wd_hbm, ys_ref, wgb_ref, wub_ref, wdb_ref,
                sg_ref, su_ref, sd_ref, ord_ref, sem, *, layer):
    i = pl.program_id(0)
    used = i < glob_ref[GLOB_NUSED * LANES]
    expert = _tile_expert(glob_ref, i)
    new_expert = (i == 0) | (expert != _tile_expert(glob_ref, jnp.maximum(i - 1, 0)))

    def weight_copies(e, slot):
        return [pltpu.make_async_copy(src.at[layer, e], dst.at[slot], sem.at[slot])
                for src, dst in ((wg_hbm, sg_ref), (wu_hbm, su_ref), (wd_hbm, sd_ref))]

    @pl.when(i == 0)
    def _():
        ord_ref[0] = 0
        for cp in weight_copies(expert, 0):
            cp.start()

    @pl.when(used & new_expert)
    def _():
        slot = lax.rem(ord_ref[0], 2)
        for cp in weight_copies(expert, slot):
            cp.wait()
        nxt = lax.while_loop(
            lambda k: (k < N_EXPERTS) & (glob_ref[GLOB_PTILES * LANES + ROUTER_LANE0 + jnp.minimum(k, N_EXPERTS - 1)] == 0),
            lambda k: k + 1, expert + 1)

        @pl.when(nxt < N_EXPERTS)
        def _():
            for cp in weight_copies(nxt, 1 - slot):
                cp.start()

        wgb_ref[...] = sg_ref[slot].astype(BF16)
        wub_ref[...] = su_ref[slot].astype(BF16)
        wdb_ref[...] = sd_ref[slot].astype(BF16)
        ord_ref[0] = ord_ref[0] + 1

    @pl.when(used)
    def _():
        a, b = _unpack_pair(xs_ref[...])
        x = jnp.concatenate([a, b], axis=1).astype(BF16)
        gate = jnp.dot(x, wgb_ref[...], preferred_element_type=F32)
        up = jnp.dot(x, wub_ref[...], preferred_element_type=F32)
        h = (gate * jax.nn.sigmoid(gate) * up).astype(BF16)
        y = jnp.dot(h, wdb_ref[...], preferred_element_type=F32)
        ys_ref[...] = _pack_pair(y[:, :D_PACK], y[:, D_PACK:])

    @pl.when(jnp.logical_not(used))
    def _():
        zero = jnp.zeros(ys_ref.shape, F32)
        ys_ref[...] = _pack_pair(zero, zero)


def _moe(glob_flat, xs, wg, wu, wd, layer):
    n_tiles = (xs.shape[0] - TRASH_ROWS) // TM_MOE
    assert n_tiles <= 2 * LANES
    hbm = pl.BlockSpec(memory_space=pl.ANY)
    grid_spec = pltpu.PrefetchScalarGridSpec(
        num_scalar_prefetch=1,
        grid=(n_tiles,),
        in_specs=[pl.BlockSpec((TM_MOE, D_PACK), lambda i, g: (jnp.minimum(i, g[GLOB_NUSED * LANES] - 1), 0)),
                  hbm, hbm, hbm],
        out_specs=pl.BlockSpec((TM_MOE, D_PACK), lambda i, g: (i, 0)),
        scratch_shapes=[pltpu.VMEM((D_MODEL, D_EXPERT), BF16), pltpu.VMEM((D_MODEL, D_EXPERT), BF16),
                        pltpu.VMEM((D_EXPERT, D_MODEL), BF16),
                        pltpu.VMEM((2, D_MODEL, D_EXPERT), F32), pltpu.VMEM((2, D_MODEL, D_EXPERT), F32),
                        pltpu.VMEM((2, D_EXPERT, D_MODEL), F32), pltpu.SMEM((1,), I32),
                        pltpu.SemaphoreType.DMA((2,))],
    )
    return pl.pallas_call(
        functools.partial(_moe_kernel, layer=layer),
        grid_spec=grid_spec,
        out_shape=jax.ShapeDtypeStruct((n_tiles * TM_MOE, D_PACK), U32),
        compiler_params=_params(("arbitrary",)),
        name="moe_experts",
    )(glob_flat, xs, wg, wu, wd)


def _combine_kernel(tile_ref, next_ref, glob_ref, tokc_ref, x1_ref, ys_ref, fg_ref, *rest, n_first, final):
    n_out = 2 if final else 1
    out_refs = rest[:n_out]
    lbuf_ref, sem = rest[n_out:]
    i = pl.program_id(0)
    last = pl.num_programs(0) - 1
    slot = lax.rem(i, 2)

    def gather(tbl_ref, dst_slot):
        def start(local_row, real, sorted_row, c):
            src = pl.multiple_of(jnp.where(real, sorted_row, 0), RUN)
            pltpu.make_async_copy(ys_ref.at[pl.ds(src, RUN)], lbuf_ref.at[dst_slot, pl.ds(local_row, RUN)],
                                  sem.at[dst_slot]).start(priority=c % 2)

        _for_each_chunk(tbl_ref, glob_ref, start)

    def wait(which):
        _wait_chunks(lambda: pltpu.make_async_copy(ys_ref.at[pl.ds(0, RUN)], lbuf_ref.at[which, pl.ds(0, RUN)],
                                                   sem.at[which]).wait())

    @pl.when(i == 0)
    def _():
        gather(tile_ref, 0)

    wait(slot)
    gather(next_ref, 1 - slot)

    tokc = tokc_ref[...]
    w1 = tokc[:, TOK_W1:TOK_W1 + 1]
    w2 = tokc[:, TOK_W2:TOK_W2 + 1]
    col = lax.broadcasted_iota(I32, (1, N_LOCAL), 1).astype(F32)
    sel1 = jnp.where(col == tokc[:, TOK_P1:TOK_P1 + 1], 1.0, 0.0).astype(BF16)
    sel2 = jnp.where(col == tokc[:, TOK_P2:TOK_P2 + 1], 1.0, 0.0).astype(BF16)
    halves = []
    for part in _unpack_pair(lbuf_ref[slot]):
        rows = part.astype(BF16)
        halves.append(w1 * jnp.dot(sel1, rows, preferred_element_type=F32)
                      + w2 * jnp.dot(sel2, rows, preferred_element_type=F32))
    x2 = x1_ref[...] + jnp.concatenate(halves, axis=1)
    if not final:
        out_refs[0][...] = x2
    else:
        y = _rms(x2, fg_ref[...])

        @pl.when(i < n_first)
        def _():
            out_refs[0][...] = y

        @pl.when(i >= n_first)
        def _():
            out_refs[1][...] = y

    @pl.when(i == last)
    def _():
        wait(1 - slot)


def _combine(tile_tbl, glob_flat, tokc, x1, ys, fg, n_first_rows, final):
    n = x1.shape[0]
    n_blk = n // TM_OUT
    n_first = n_first_rows // TM_OUT
    assert n_first_rows % TM_OUT == 0
    row = lambda width: pl.BlockSpec((TM_OUT, width), lambda i: (i, 0))
    if final:
        out_specs = [pl.BlockSpec((TM_OUT, D_MODEL), lambda i: (jnp.minimum(i, n_first - 1), 0)),
                     pl.BlockSpec((TM_OUT, D_MODEL), lambda i: (jnp.maximum(i - n_first, 0), 0))]
        out_shape = [jax.ShapeDtypeStruct((n_first_rows, D_MODEL), F32),
                     jax.ShapeDtypeStruct((n - n_first_rows, D_MODEL), F32)]
    else:
        out_specs = [row(D_MODEL)]
        out_shape = [jax.ShapeDtypeStruct((n, D_MODEL), F32)]
    return pl.pallas_call(
        functools.partial(_combine_kernel, n_first=n_first, final=final),
        grid=(n_blk,),
        in_specs=[pl.BlockSpec((SUBLANES, LANES), lambda i: (i, 0), memory_space=pltpu.SMEM),
                  pl.BlockSpec((SUBLANES, LANES), lambda i: (jnp.minimum(i + 1, n_blk - 1), 0), memory_space=pltpu.SMEM),
                  pl.BlockSpec(memory_space=pltpu.SMEM),
                  row(4), row(D_MODEL), pl.BlockSpec(memory_space=pl.ANY), _full((1, D_MODEL))],
        out_specs=out_specs,
        out_shape=out_shape,
        scratch_shapes=[pltpu.VMEM((2, N_LOCAL, D_PACK), U32), pltpu.SemaphoreType.DMA((2,))],
        compiler_params=_params(("arbitrary",)),
        name="moe_combine",
    )(tile_tbl, tile_tbl, glob_flat, tokc, x1, ys, fg.reshape(1, D_MODEL))


def _n_moe_tiles(n_tokens):
    n_runs = (n_tokens // TM_OUT) * N_EXPERTS
    return (2 * n_tokens + n_runs * (RUN - 1) + N_EXPERTS * (TM_MOE - 1)) // TM_MOE + 1


def kernel(x_prompt, x_sample, cache_k, cache_v, state_conv, state_ssm_re, state_ssm_im, attn_norm_g, w_in, attn_sinks, conv_w, conv_b, conv_ln_g, conv_ln_b, ssm_a_re, ssm_a_im, ssm_log_dt, ssm_b_re, ssm_b_im, ssm_c_re, ssm_c_im, ssm_d, ssm_glu_w, ssm_glu_b, grp_norm_g, w_out, ffn_norm_g, w_group_router, b_group_router, w_expert_router, b_expert_router, w_gate, w_up, w_down, final_norm_g):
    p = dict(ssm_a_re=ssm_a_re, ssm_a_im=ssm_a_im, ssm_log_dt=ssm_log_dt, ssm_b_re=ssm_b_re, ssm_b_im=ssm_b_im,
             ssm_c_re=ssm_c_re, ssm_c_im=ssm_c_im, ssm_d=ssm_d, ssm_glu_w=ssm_glu_w, ssm_glu_b=ssm_glu_b)
    depth = w_in.shape[0]
    n_seq, t, _ = x_prompt.shape
    n_dec, s_new, _ = x_sample.shape
    win = cache_k.shape[2]
    n_p = n_seq * t
    n_s = n_dec * s_new
    n = n_p + n_s
    assert n_seq == 2 and t % SCAN_T == 0 and t % CONV_T == 0 and n_p % TM_IN == 0 and n_s % TM_IN == 0
    hist = CONV_WIDTH - 1
    n_tiles = _n_moe_tiles(n)

    x_src = [x_prompt.reshape(n_p, D_MODEL), x_sample.reshape(n_s, D_MODEL)]
    outs = {k: [] for k in ("kp", "vp", "cp", "rp", "ip", "kv_new", "u_new", "rs", "is")}
    ck_all = cache_k.reshape(depth * n_dec, win, D_KV)
    cv_all = cache_v.reshape(depth * n_dec, win, D_KV)
    conv_all = state_conv.reshape(depth * n_dec, hist * D_CONV)
    for l in range(depth):
        q, kv, u, su = _in_proj(x_src, attn_norm_g[l], w_in, l)

        oa_p = _attn_prompt(q, kv, attn_sinks[l], n_seq, t)
        oa_s = _attn_sample(q, kv, ck_all, cv_all, attn_sinks[l], s_new, n_p, n_dec, l * n_dec)
        kv_p = kv[:n_p].reshape(n_seq, t, 2 * D_KV)[:, t - win:].astype(F32)
        outs["kp"].append(kv_p[..., :D_KV].reshape(n_seq, win, N_KV_HEADS, HEAD_DIM))
        outs["vp"].append(kv_p[..., D_KV:].reshape(n_seq, win, N_KV_HEADS, HEAD_DIM))
        outs["kv_new"].append(kv[n_p:].reshape(n_dec, s_new, 2 * D_KV))

        oc_p = _conv_prompt(u, conv_w[l], conv_b[l], conv_ln_g[l], conv_ln_b[l], n_seq, t)
        u_s = u[n_p:].reshape(n_dec, s_new, D_CONV)
        oc_s = _conv_sample(conv_all, u_s.reshape(n_dec, s_new * D_CONV),
                            conv_w[l], conv_b[l], conv_ln_g[l], conv_ln_b[l], s_new, l)
        outs["cp"].append(u[:n_p].reshape(n_seq, t, D_CONV)[:, t - hist:])
        outs["u_new"].append(u_s)

        sp = _s5_params(p, l)
        os0, os1, hre, him = _s5_prompt(su, sp, t)
        su_tb = su[n_p:].reshape(n_dec, s_new, D_SSM).transpose(1, 0, 2).reshape(n_s, D_SSM)
        os_tb, hr_s, hi_s = _s5_sample(su_tb, state_ssm_re[l].reshape(n_dec, N_STATE),
                                       state_ssm_im[l].reshape(n_dec, N_STATE), sp, s_new)
        os_s = os_tb.reshape(s_new, n_dec, D_SSM).transpose(1, 0, 2).reshape(n_s, D_SSM)
        outs["rp"].append(hre.reshape(n_seq, SSM_GROUPS, SSM_STATE))
        outs["ip"].append(him.reshape(n_seq, SSM_GROUPS, SSM_STATE))
        outs["rs"].append(hr_s.reshape(n_dec, SSM_GROUPS, SSM_STATE))
        outs["is"].append(hi_s.reshape(n_dec, SSM_GROUPS, SSM_STATE))

        unused = LANES - N_EXPERT_GROUPS - N_EXPERTS
        wr = jnp.pad(jnp.concatenate([w_group_router[l], w_expert_router[l]], axis=1), ((0, 0), (0, unused)))
        wr_hi = wr.astype(BF16)
        wr_lo = (wr - wr_hi.astype(F32)).astype(BF16)
        br = jnp.pad(jnp.concatenate([b_group_router[l], b_expert_router[l]]), (0, unused)).reshape(1, LANES)
        x1, xn, tokc, tokl, tile_tbl, glob = _merge_out(
            [oa_p, oa_s], [oc_p, oc_s.reshape(n_s, D_CONV)], [os0, os1, os_s], x_src,
            grp_norm_g[l], w_out, l, ffn_norm_g[l], wr_hi, wr_lo, br)

        glob_flat = glob.reshape(SUBLANES * LANES)
        xs = _dispatch(tile_tbl, glob_flat, tokl, xn, n_tiles * TM_MOE + TRASH_ROWS)
        ys = _moe(glob_flat, xs, w_gate, w_up, w_down, l)
        res = _combine(tile_tbl, glob_flat, tokc, x1, ys, final_norm_g, n_p, final=(l == depth - 1))
        x_src = [res[0]]

    y_p = res[0].reshape(n_seq, t, D_MODEL)
    y_s = res[1].reshape(n_dec, s_new, D_MODEL)
    st = lambda k: jnp.stack(outs[k])
    kv_new = st("kv_new").astype(F32)
    heads = lambda z: z.reshape(depth, n_dec, s_new, N_KV_HEADS, HEAD_DIM)
    k_s = jnp.concatenate([cache_k[:, :, s_new:], heads(kv_new[..., :D_KV])], axis=2)
    v_s = jnp.concatenate([cache_v[:, :, s_new:], heads(kv_new[..., D_KV:])], axis=2)
    conv_s = jnp.concatenate([state_conv[:, :, s_new:], st("u_new")], axis=2)
    return (y_p, y_s, st("kp"), st("vp"), st("cp"), st("rp"), st("ip"), k_s, v_s, conv_s, st("rs"), st("is"))
```

```python
import functools

import jax
import jax.numpy as jnp
from jax import lax
from jax.experimental import pallas as pl
from jax.experimental.pallas import tpu as pltpu

F32 = jnp.float32
BF16 = jnp.bfloat16
U32 = jnp.uint32
I32 = jnp.int32

D_MODEL = 1024
N_HEADS = 8
N_KV_HEADS = 2
HEAD_DIM = 64
WINDOW = 128
D_ATTN = N_HEADS * HEAD_DIM
D_KV = N_KV_HEADS * HEAD_DIM
D_CONV = 256
CONV_WIDTH = 31
D_SSM = 256
SSM_GROUPS = 16
SSM_GROUP_CH = 16
SSM_STATE = 64
N_STATE = SSM_GROUPS * SSM_STATE
D_IN = D_ATTN + 2 * D_KV + 2 * D_CONV + D_SSM
N_EXPERT_GROUPS = 4
EXPERTS_PER_GROUP = 8
N_EXPERTS = N_EXPERT_GROUPS * EXPERTS_PER_GROUP
D_EXPERT = 512
EPS = 1e-6
NEG_INF = -1e30
SCALE = HEAD_DIM ** -0.5

LANES = 128
SUBLANES = 8
HALF = LANES // 2
D_PACK = D_MODEL // 2

TM_IN = 512
TM_OUT = 256
TM_MOE = 512
CONV_T = 512
CONV_CHUNK = 64
CONV_HALO = 32
SCAN_T = 512
SCAN_PITCH = SCAN_T + SUBLANES
SAMPLE_BT = 16
ATTN_QB = 8
MERGE_SUB = 2
VMEM_LIMIT = 48 * 1024 * 1024

ROUTER_LANE0 = N_EXPERT_GROUPS
RUN = SUBLANES
N_LOCAL = 2 * TM_OUT + N_EXPERTS * RUN
TOK_W1, TOK_W2, TOK_P1, TOK_P2 = range(4)
TILE_NCHUNK, TILE_CHUNK_E, TILE_CHUNK_REL = range(3)
CHUNK_GROUP = 8
TRASH_ROWS = CHUNK_GROUP * RUN
GLOB_START, GLOB_ZERO, GLOB_TE, GLOB_TE_HI, GLOB_NUSED, GLOB_PTILES = range(6)


def _params(sem, vmem=VMEM_LIMIT):
    return pltpu.CompilerParams(dimension_semantics=sem, vmem_limit_bytes=vmem)


def _full(shape):
    zeros = (0,) * len(shape)
    return pl.BlockSpec(shape, lambda *_: zeros)


def _swap_halves(x):
    return jnp.concatenate([x[:, HALF:], x[:, :HALF]], axis=1)


def _rms(x, g):
    return x * lax.rsqrt(jnp.mean(x * x, axis=-1, keepdims=True) + EPS) * g


def _pack_pair(a, b):
    return pltpu.pack_elementwise([a, b], packed_dtype=BF16)


def _unpack_pair(p):
    return tuple(pltpu.unpack_elementwise(p, index=k, packed_dtype=BF16, unpacked_dtype=F32) for k in range(2))


def _pick(i, refs, starts):
    val = refs[0][...].astype(F32)
    for ref, start in zip(refs[1:], starts[1:]):
        val = jnp.where(i >= start, ref[...].astype(F32), val)
    return val


def _source_specs(sources, tm, width):
    specs, starts, start = [], [], 0
    for arr in sources:
        assert arr.shape[0] % tm == 0 and arr.shape[1] == width
        n_blk = arr.shape[0] // tm
        specs.append(pl.BlockSpec((tm, width), lambda i, s=start, nb=n_blk: (jnp.clip(i - s, 0, nb - 1), 0)))
        starts.append(start)
        start += n_blk
    return specs, tuple(starts), start


def _in_proj_kernel(*refs, starts):
    x_refs = refs[:len(starts)]
    g_ref, w_ref, q_ref, kv_ref, u_ref, su_ref, wbf_ref = refs[len(starts):]

    @pl.when(pl.program_id(0) == 0)
    def _():
        wbf_ref[...] = w_ref[...].astype(BF16)

    hn = _rms(_pick(pl.program_id(0), x_refs, starts), g_ref[...]).astype(BF16)
    z = jnp.dot(hn, wbf_ref[...], preferred_element_type=F32)
    q_ref[...] = z[:, :D_ATTN].astype(BF16)
    kv_ref[...] = z[:, D_ATTN:D_ATTN + 2 * D_KV].astype(BF16)
    c0 = D_ATTN + 2 * D_KV
    u_ref[...] = z[:, c0:c0 + D_CONV] * jax.nn.sigmoid(z[:, c0 + D_CONV:c0 + 2 * D_CONV])
    su_ref[...] = z[:, c0 + 2 * D_CONV:].astype(BF16)


def _layer_block(shape, layer):
    zeros = (0,) * len(shape)
    return pl.BlockSpec((None,) + tuple(shape), lambda *_: (layer,) + zeros)


def _in_proj(x_sources, g, w, layer):
    x_specs, starts, n_blk = _source_specs(x_sources, TM_IN, D_MODEL)
    n = n_blk * TM_IN
    row = lambda width: pl.BlockSpec((TM_IN, width), lambda i: (i, 0))
    return pl.pallas_call(
        functools.partial(_in_proj_kernel, starts=starts),
        grid=(n_blk,),
        in_specs=x_specs + [_full((1, D_MODEL)), _layer_block((D_MODEL, D_IN), layer)],
        out_specs=[row(D_ATTN), row(2 * D_KV), row(D_CONV), row(D_SSM)],
        out_shape=[jax.ShapeDtypeStruct((n, D_ATTN), BF16), jax.ShapeDtypeStruct((n, 2 * D_KV), BF16),
                   jax.ShapeDtypeStruct((n, D_CONV), F32), jax.ShapeDtypeStruct((n, D_SSM), BF16)],
        scratch_shapes=[pltpu.VMEM((D_MODEL, D_IN), BF16)],
        compiler_params=_params(("arbitrary",)),
        name="in_proj",
    )(*x_sources, g.reshape(1, D_MODEL), w)


def _softmax_pv(s, mask, sink, vmat):
    s = jnp.where(mask, s, NEG_INF)
    m = jnp.maximum(jnp.max(s, axis=-1, keepdims=True), sink)
    p = jnp.exp(s - m)
    denom = jnp.sum(p, axis=-1, keepdims=True) + jnp.exp(sink - m)
    return jnp.dot(p.astype(BF16), vmat, preferred_element_type=F32) / denom


def _attn_prompt_kernel(sink_ref, q_ref, kvc_ref, kvp_ref, o_ref):
    i = pl.program_id(1)
    lo = lax.broadcasted_iota(I32, (1, LANES), 1) < HALF
    a = lax.broadcasted_iota(I32, (WINDOW, 2 * WINDOW), 0)
    c = lax.broadcasted_iota(I32, (WINDOW, 2 * WINDOW), 1)
    diff = a + WINDOW - c
    band = (diff >= 0) & (diff < WINDOW)
    zero = jnp.zeros((WINDOW, LANES), BF16)
    for sub in range(ATTN_QB):
        rows = slice(sub * WINDOW, (sub + 1) * WINDOW)
        q = q_ref[rows, :] * jnp.asarray(SCALE, BF16)
        kvc = kvc_ref[rows, :]
        kvp = kvp_ref[...] if sub == 0 else kvc_ref[(sub - 1) * WINDOW:sub * WINDOW, :]
        mask = band & ((c >= WINDOW) | (i > 0)) if sub == 0 else band
        kk = jnp.concatenate([kvp[:, :LANES], kvc[:, :LANES]], axis=0)
        vv = jnp.concatenate([kvp[:, LANES:], kvc[:, LANES:]], axis=0)
        kk_sw = _swap_halves(kk)
        vv_sw = _swap_halves(vv)
        for j in range(D_ATTN // LANES):
            kvh = (2 * j) // (N_HEADS // N_KV_HEADS)
            qt = q[:, LANES * j:LANES * (j + 1)]
            mats = ((kk, vv), (kk_sw, vv_sw)) if kvh == 0 else ((kk_sw, vv_sw), (kk, vv))
            outs = []
            for par in range(2):
                kmat, vmat = mats[par]
                qm = jnp.where(lo if par == 0 else jnp.logical_not(lo), qt, zero)
                s = lax.dot_general(qm, kmat, (((1,), (1,)), ((), ())), preferred_element_type=F32)
                outs.append(_softmax_pv(s, mask, sink_ref[2 * j + par], vmat))
            o_ref[rows, LANES * j:LANES * (j + 1)] = jnp.where(lo, outs[0], outs[1]).astype(BF16)


def _attn_prompt(q, kv, sinks, n_seq, t):
    rows = ATTN_QB * WINDOW
    assert t % rows == 0
    nb = t // rows
    cur = lambda width: pl.BlockSpec((rows, width), lambda b, i: (b * nb + i, 0))
    prev = pl.BlockSpec((WINDOW, 2 * D_KV), lambda b, i: ((b * nb + i) * ATTN_QB - jnp.minimum(i, 1), 0))
    return pl.pallas_call(
        _attn_prompt_kernel,
        grid=(n_seq, nb),
        in_specs=[pl.BlockSpec(memory_space=pltpu.SMEM), cur(D_ATTN), cur(2 * D_KV), prev],
        out_specs=cur(D_ATTN),
        out_shape=jax.ShapeDtypeStruct((n_seq * t, D_ATTN), BF16),
        compiler_params=_params(("arbitrary", "arbitrary")),
        name="attn_prompt",
    )(sinks, q, kv, kv)


def _attn_sample_kernel(sink_ref, q_ref, kv_ref, ck_ref, cv_ref, o_ref, qf_ref, kvf_ref, *, s_new):
    rows = SUBLANES
    n_pair = q_ref.shape[0] // rows
    per = rows // s_new
    qf_ref[...] = q_ref[...].astype(F32) * SCALE
    kvf_ref[...] = kv_ref[...].astype(F32)
    lane = lax.broadcasted_iota(I32, (1, LANES), 1)
    lo = lane < HALF
    rid = lax.broadcasted_iota(I32, (N_HEADS * rows, 1), 0)
    head = rid // rows
    seq = (rid % rows) // s_new
    tok = rid % s_new
    sink = jnp.zeros((N_HEADS * rows, 1), F32)
    for h in range(N_HEADS):
        sink = jnp.where(head == h, sink_ref[h], sink)
    mask_c = lane > tok

    def pair(p, carry):
        r0 = pl.multiple_of(p * rows, rows)
        q8 = qf_ref[pl.ds(r0, rows), :]
        kv8 = kvf_ref[pl.ds(r0, rows), :]
        knew = kv8[:, :LANES]
        vnew = kv8[:, LANES:]
        pieces = []
        for h in range(N_HEADS):
            qt = q8[:, LANES * (h // 2):LANES * (h // 2 + 1)]
            tgt = h // (N_HEADS // N_KV_HEADS)
            if h % 2 != tgt:
                qt = _swap_halves(qt)
            pieces.append(jnp.where(lo if tgt == 0 else jnp.logical_not(lo), qt, 0.0))
        qm = jnp.concatenate(pieces, axis=0)
        qb = qm.astype(BF16)
        s_c = jnp.zeros((N_HEADS * rows, LANES), F32)
        for bb in range(per):
            kc = ck_ref[p * per + bb].astype(BF16)
            s_bb = lax.dot_general(qb, kc, (((1,), (1,)), ((), ())), preferred_element_type=F32)
            s_c = jnp.where(seq == bb, s_bb, s_c)
        s_c = jnp.where(mask_c, s_c, NEG_INF)
        m = jnp.maximum(jnp.max(s_c, axis=-1, keepdims=True), sink)
        s_n = []
        for k in range(rows):
            valid = (seq == k // s_new) & (tok >= k % s_new)
            sk = jnp.sum(qm * knew[k:k + 1, :], axis=-1, keepdims=True)
            sk = jnp.where(valid, sk, NEG_INF)
            s_n.append(sk)
            m = jnp.maximum(m, sk)
        p_c = jnp.exp(s_c - m)
        denom = jnp.sum(p_c, axis=-1, keepdims=True) + jnp.exp(sink - m)
        pb = p_c.astype(BF16)
        o = jnp.zeros((N_HEADS * rows, LANES), F32)
        for bb in range(per):
            vc = cv_ref[p * per + bb].astype(BF16)
            o = jnp.where(seq == bb, jnp.dot(pb, vc, preferred_element_type=F32), o)
        for k in range(rows):
            pk = jnp.exp(s_n[k] - m)
            denom = denom + pk
            o = o + pk.astype(BF16).astype(F32) * vnew[k:k + 1, :]
        o = o / denom
        for j in range(D_ATTN // LANES):
            kvh = (2 * j) // (N_HEADS // N_KV_HEADS)
            pe = o[rows * 2 * j:rows * (2 * j + 1), :]
            po = o[rows * (2 * j + 1):rows * (2 * j + 2), :]
            if kvh == 0:
                po = _swap_halves(po)
            else:
                pe = _swap_halves(pe)
            o_ref[pl.ds(r0, rows), LANES * j:LANES * (j + 1)] = jnp.where(lo, pe, po)
        return carry

    lax.fori_loop(0, n_pair, pair, 0)


def _attn_sample(q, kv, cache_k, cache_v, sinks, s_new, row0, n_seq, seq0):
    win = cache_k.shape[1]
    n_rows = n_seq * s_new
    rows_blk = SAMPLE_BT * s_new
    assert win == WINDOW and SUBLANES % s_new == 0 and n_seq % SAMPLE_BT == 0 and row0 % rows_blk == 0
    assert seq0 % SAMPLE_BT == 0
    blk0 = row0 // rows_blk
    cblk0 = seq0 // SAMPLE_BT
    return pl.pallas_call(
        functools.partial(_attn_sample_kernel, s_new=s_new),
        grid=(n_seq // SAMPLE_BT,),
        in_specs=[pl.BlockSpec(memory_space=pltpu.SMEM),
                  pl.BlockSpec((rows_blk, D_ATTN), lambda i: (blk0 + i, 0)),
                  pl.BlockSpec((rows_blk, 2 * D_KV), lambda i: (blk0 + i, 0)),
                  pl.BlockSpec((SAMPLE_BT, win, D_KV), lambda i: (cblk0 + i, 0, 0)),
                  pl.BlockSpec((SAMPLE_BT, win, D_KV), lambda i: (cblk0 + i, 0, 0))],
        out_specs=pl.BlockSpec((rows_blk, D_ATTN), lambda i: (i, 0)),
        out_shape=jax.ShapeDtypeStruct((n_rows, D_ATTN), F32),
        scratch_shapes=[pltpu.VMEM((rows_blk, D_ATTN), F32), pltpu.VMEM((rows_blk, 2 * D_KV), F32)],
        compiler_params=_params(("arbitrary",)),
        name="attn_sample",
    )(sinks, q, kv, cache_k, cache_v)


def _ln_silu(y, lg, lb):
    mu = jnp.mean(y, axis=-1, keepdims=True)
    var = jnp.mean(jnp.square(y - mu), axis=-1, keepdims=True)
    yn = (y - mu) * lax.rsqrt(var + EPS) * lg + lb
    return yn * jax.nn.sigmoid(yn)


def _conv_prompt_kernel(u_ref, w_ref, b_ref, lg_ref, lb_ref, o_ref, ext_ref):
    i = pl.program_id(1)

    @pl.when(i == 0)
    def _():
        ext_ref[0:CONV_HALO, :] = jnp.zeros((CONV_HALO, D_CONV), F32)
        ext_ref[CONV_HALO + CONV_T:, :] = jnp.zeros((SUBLANES, D_CONV), F32)

    @pl.when(i > 0)
    def _():
        ext_ref[0:CONV_HALO, :] = ext_ref[CONV_T:CONV_T + CONV_HALO, :]

    ext_ref[CONV_HALO:CONV_HALO + CONV_T, :] = u_ref[...]
    shift = CONV_HALO - (CONV_WIDTH - 1)
    for cidx in range(CONV_T // CONV_CHUNK):
        r0 = cidx * CONV_CHUNK
        acc = jnp.zeros((CONV_CHUNK, D_CONV), F32)
        for rho in range(SUBLANES):
            part = jnp.zeros((CONV_CHUNK + SUBLANES, D_CONV), F32)
            for j in range(CONV_WIDTH):
                if (j + shift) % SUBLANES == rho:
                    base = r0 + j + shift - rho
                    part = part + w_ref[j:j + 1, :] * ext_ref[base:base + CONV_CHUNK + SUBLANES, :]
            acc = acc + part[rho:rho + CONV_CHUNK, :]
        o_ref[r0:r0 + CONV_CHUNK, :] = _ln_silu(acc + b_ref[...], lg_ref[...], lb_ref[...]).astype(BF16)


def _conv_prompt(u, w, b, lg, lb, n_seq, t):
    nt = t // CONV_T
    vec = _full((1, D_CONV))
    return pl.pallas_call(
        _conv_prompt_kernel,
        grid=(n_seq, nt),
        in_specs=[pl.BlockSpec((CONV_T, D_CONV), lambda s, i: (s * nt + i, 0)),
                  _full((CONV_WIDTH, D_CONV)), vec, vec, vec],
        out_specs=pl.BlockSpec((CONV_T, D_CONV), lambda s, i: (s * nt + i, 0)),
        out_shape=jax.ShapeDtypeStruct((n_seq * t, D_CONV), BF16),
        scratch_shapes=[pltpu.VMEM((CONV_T + CONV_HALO + SUBLANES, D_CONV), F32)],
        compiler_params=_params(("arbitrary", "arbitrary")),
        name="conv_prompt",
    )(u, w, b.reshape(1, D_CONV), lg.reshape(1, D_CONV), lb.reshape(1, D_CONV))


def _conv_sample_kernel(st_ref, u_ref, w_ref, b_ref, lg_ref, lb_ref, o_ref, *, s_new):
    hist = CONV_WIDTH - 1
    for t in range(s_new):
        acc = jnp.zeros((st_ref.shape[0], D_CONV), F32)
        for j in range(CONV_WIDTH):
            idx = t + j
            if idx < hist:
                piece = st_ref[:, idx * D_CONV:(idx + 1) * D_CONV]
            else:
                piece = u_ref[:, (idx - hist) * D_CONV:(idx - hist + 1) * D_CONV]
            acc = acc + w_ref[j:j + 1, :] * piece
        o_ref[:, t * D_CONV:(t + 1) * D_CONV] = _ln_silu(acc + b_ref[...], lg_ref[...], lb_ref[...])


def _conv_sample(state2d, u2d, w, b, lg, lb, s_new, layer):
    n_seq = u2d.shape[0]
    vec = _full((1, D_CONV))
    return pl.pallas_call(
        functools.partial(_conv_sample_kernel, s_new=s_new),
        grid=(1,),
        in_specs=[pl.BlockSpec((n_seq, state2d.shape[1]), lambda i: (layer, 0)), _full(u2d.shape),
                  _full((CONV_WIDTH, D_CONV)), vec, vec, vec],
        out_specs=_full((n_seq, s_new * D_CONV)),
        out_shape=jax.ShapeDtypeStruct((n_seq, s_new * D_CONV), F32),
        compiler_params=_params(("arbitrary",)),
        name="conv_sample",
    )(state2d, u2d, w, b.reshape(1, D_CONV), lg.reshape(1, D_CONV), lb.reshape(1, D_CONV))


def _s5_discretize(a_re, a_im, log_dt):
    dt = jnp.exp(log_dt)
    mag = jnp.exp(a_re * dt)
    ang = a_im * dt
    lr = mag * jnp.cos(ang)
    li = mag * jnp.sin(ang)
    den = a_re * a_re + a_im * a_im
    cr = ((lr - 1.0) * a_re + li * a_im) / den
    ci = (li * a_re - (lr - 1.0) * a_im) / den
    return lr, li, cr, ci


def _s5_bbar(arow_ref, bre_ref, bim_ref):
    _, _, cr, ci = _s5_discretize(arow_ref[0:1, :], arow_ref[1:2, :], arow_ref[2:3, :])
    bre = bre_ref[...]
    bim = bim_ref[...]
    return (cr * bre - ci * bim).astype(BF16), (cr * bim + ci * bre).astype(BF16)


def _s5_readout(h_re, h_im, u, cre_ref, cim_ref, d_ref, gw_ref, gb_ref):
    y = (jnp.dot(h_re.astype(BF16), cre_ref[...], preferred_element_type=F32)
         - jnp.dot(h_im.astype(BF16), cim_ref[...], preferred_element_type=F32)
         + d_ref[...] * u.astype(F32))
    z = jax.nn.gelu(y)
    gate = jnp.dot(z.astype(BF16), gw_ref[...], preferred_element_type=F32) + gb_ref[...]
    return (z * jax.nn.sigmoid(gate)).astype(BF16)


def _s5_prompt_kernel(su0_ref, su1_ref, arow_ref, atile_ref, bre_ref, bim_ref, cre_ref, cim_ref, d_ref,
                      gw_ref, gb_ref, o0_ref, o1_ref, hre_ref, him_ref,
                      bbr_ref, bbi_ref, lam_ref, car_ref, bur_ref, bui_ref, hbr_ref, hbi_ref):
    i = pl.program_id(0)
    n_slab = N_STATE // LANES
    su_refs = (su0_ref, su1_ref)
    o_refs = (o0_ref, o1_ref)

    @pl.when(i == 0)
    def _():
        bbr, bbi = _s5_bbar(arow_ref, bre_ref, bim_ref)
        bbr_ref[...] = bbr
        bbi_ref[...] = bbi
        lr, li, _, _ = _s5_discretize(atile_ref[0], atile_ref[1], atile_ref[2])
        lam_ref[0] = lr
        lam_ref[1] = li
        car_ref[...] = jnp.zeros(car_ref.shape, F32)

    for s in range(2):
        u = su_refs[s][...]
        br = jnp.dot(u, bbr_ref[...], preferred_element_type=F32)
        bi = jnp.dot(u, bbi_ref[...], preferred_element_type=F32)
        for j in range(n_slab):
            bur_ref[s, j * SCAN_PITCH:j * SCAN_PITCH + SCAN_T, :] = br[:, LANES * j:LANES * (j + 1)]
            bui_ref[s, j * SCAN_PITCH:j * SCAN_PITCH + SCAN_T, :] = bi[:, LANES * j:LANES * (j + 1)]

    lr = lam_ref[0]
    li = lam_ref[1]

    def step(t, carry):
        new = []
        for s in range(2):
            hr, hi = carry[2 * s], carry[2 * s + 1]
            rows = pl.ds(t, n_slab, stride=SCAN_PITCH)
            nr = lr * hr - li * hi + bur_ref.at[s][rows, :]
            ni = lr * hi + li * hr + bui_ref.at[s][rows, :]
            hbr_ref.at[s][rows, :] = nr
            hbi_ref.at[s][rows, :] = ni
            new += [nr, ni]
        return tuple(new)

    carry = lax.fori_loop(0, SCAN_T, step, tuple(car_ref[k] for k in range(4)), unroll=8)
    for k in range(4):
        car_ref[k] = carry[k]

    @pl.when(i == pl.num_programs(0) - 1)
    def _():
        for s in range(2):
            hre_ref[s] = carry[2 * s]
            him_ref[s] = carry[2 * s + 1]

    for s in range(2):
        h_re = jnp.concatenate([hbr_ref[s, j * SCAN_PITCH:j * SCAN_PITCH + SCAN_T, :] for j in range(n_slab)], axis=1)
        h_im = jnp.concatenate([hbi_ref[s, j * SCAN_PITCH:j * SCAN_PITCH + SCAN_T, :] for j in range(n_slab)], axis=1)
        o_refs[s][...] = _s5_readout(h_re, h_im, su_refs[s][...], cre_ref, cim_ref, d_ref, gw_ref, gb_ref)


def _s5_prompt(su, prm, t):
    nt = t // SCAN_T
    n_slab = N_STATE // LANES
    blk0 = pl.BlockSpec((SCAN_T, D_SSM), lambda i: (i, 0))
    blk1 = pl.BlockSpec((SCAN_T, D_SSM), lambda i: (nt + i, 0))
    oblk = pl.BlockSpec((SCAN_T, D_SSM), lambda i: (i, 0))
    state = pl.BlockSpec((2, SUBLANES, LANES), lambda i: (0, 0, 0))
    slabs = pltpu.VMEM((2, n_slab * SCAN_PITCH, LANES), F32)
    return pl.pallas_call(
        _s5_prompt_kernel,
        grid=(nt,),
        in_specs=[blk0, blk1, _full((3, N_STATE)), _full((3, SUBLANES, LANES)),
                  _full((D_SSM, N_STATE)), _full((D_SSM, N_STATE)), _full((N_STATE, D_SSM)), _full((N_STATE, D_SSM)),
                  _full((1, D_SSM)), _full((D_SSM, D_SSM)), _full((1, D_SSM))],
        out_specs=[oblk, oblk, state, state],
        out_shape=[jax.ShapeDtypeStruct((t, D_SSM), BF16), jax.ShapeDtypeStruct((t, D_SSM), BF16),
                   jax.ShapeDtypeStruct((2, SUBLANES, LANES), F32), jax.ShapeDtypeStruct((2, SUBLANES, LANES), F32)],
        scratch_shapes=[pltpu.VMEM((D_SSM, N_STATE), BF16), pltpu.VMEM((D_SSM, N_STATE), BF16),
                        pltpu.VMEM((2, SUBLANES, LANES), F32), pltpu.VMEM((4, SUBLANES, LANES), F32),
                        slabs, slabs, slabs, slabs],
        compiler_params=_params(("arbitrary",)),
        name="s5_prompt",
    )(su, su, prm["arow"], prm["atile"], prm["bre"], prm["bim"], prm["cre"], prm["cim"],
      prm["d"], prm["gw"], prm["gb"])


def _s5_sample_kernel(su_ref, h0r_ref, h0i_ref, arow_ref, bre_ref, bim_ref, cre_ref, cim_ref, d_ref, gw_ref, gb_ref,
                      o_ref, hr_ref, hi_ref, hbr_ref, hbi_ref, *, s_new):
    n_seq = h0r_ref.shape[0]
    lr, li, _, _ = _s5_discretize(arow_ref[0:1, :], arow_ref[1:2, :], arow_ref[2:3, :])
    bbr, bbi = _s5_bbar(arow_ref, bre_ref, bim_ref)
    u = su_ref[...]
    hbr_ref[...] = jnp.dot(u, bbr, preferred_element_type=F32)
    hbi_ref[...] = jnp.dot(u, bbi, preferred_element_type=F32)
    hr_ref[...] = h0r_ref[...]
    hi_ref[...] = h0i_ref[...]
    for t in range(s_new):
        rows = slice(t * n_seq, (t + 1) * n_seq)
        hr = hr_ref[...]
        hi = hi_ref[...]
        nr = lr * hr - li * hi + hbr_ref[rows, :]
        ni = lr * hi + li * hr + hbi_ref[rows, :]
        hbr_ref[rows, :] = nr
        hbi_ref[rows, :] = ni
        hr_ref[...] = nr
        hi_ref[...] = ni
    o_ref[...] = _s5_readout(hbr_ref[...], hbi_ref[...], u, cre_ref, cim_ref, d_ref, gw_ref, gb_ref)


def _s5_sample(su_tb, h0r, h0i, prm, s_new):
    n_rows = su_tb.shape[0]
    n_seq = n_rows // s_new
    return pl.pallas_call(
        functools.partial(_s5_sample_kernel, s_new=s_new),
        out_shape=[jax.ShapeDtypeStruct((n_rows, D_SSM), BF16),
                   jax.ShapeDtypeStruct((n_seq, N_STATE), F32), jax.ShapeDtypeStruct((n_seq, N_STATE), F32)],
        scratch_shapes=[pltpu.VMEM((n_rows, N_STATE), F32), pltpu.VMEM((n_rows, N_STATE), F32)],
        compiler_params=pltpu.CompilerParams(vmem_limit_bytes=VMEM_LIMIT),
        name="s5_sample",
    )(su_tb, h0r, h0i, prm["arow"], prm["bre"], prm["bim"], prm["cre"], prm["cim"], prm["d"], prm["gw"], prm["gb"])


def _s5_params(p, l):
    eye = jnp.eye(SSM_GROUPS, dtype=F32)

    def b_diag(b):
        return jnp.einsum("gnc,gh->gchn", b, eye).reshape(D_SSM, N_STATE)

    def c_diag(c):
        return jnp.einsum("gcn,gh->gnhc", c, eye).reshape(N_STATE, D_SSM)

    ldt = jnp.broadcast_to(p["ssm_log_dt"][l][:, None], (SSM_GROUPS, SSM_STATE))
    a3 = jnp.stack([p["ssm_a_re"][l], p["ssm_a_im"][l], ldt])
    return {
        "arow": a3.reshape(3, N_STATE),
        "atile": a3.reshape(3, SUBLANES, LANES),
        "bre": b_diag(p["ssm_b_re"][l]), "bim": b_diag(p["ssm_b_im"][l]),
        "cre": c_diag(p["ssm_c_re"][l]).astype(BF16), "cim": c_diag(p["ssm_c_im"][l]).astype(BF16),
        "d": p["ssm_d"][l].reshape(1, D_SSM),
        "gw": p["ssm_glu_w"][l].astype(BF16), "gb": p["ssm_glu_b"][l].reshape(1, D_SSM),
    }


def _merge_out_kernel(*refs, starts):
    it = iter(refs)
    oa_refs, oc_refs, os_refs, x_refs = ([next(it) for _ in s] for s in starts)
    gn_ref, wo_ref, fg_ref, wrh_ref, wrl_ref, br_ref = (next(it) for _ in range(6))
    x1_ref, xn_ref, tokc_ref, tokl_ref, tile_ref, glob_ref = (next(it) for _ in range(6))
    carry_ref, wobf_ref = next(it), next(it)
    i = pl.program_id(0)
    tm = TM_OUT

    @pl.when(i == 0)
    def _():
        carry_ref[...] = jnp.zeros(carry_ref.shape, F32)
        wobf_ref[...] = wo_ref[...].astype(BF16)

    gn = gn_ref[...]
    lane = lax.broadcasted_iota(I32, (1, LANES), 1).astype(F32)
    far = float(LANES)
    is_group = lane < N_EXPERT_GROUPS
    r_i = lax.broadcasted_iota(I32, (tm, tm), 0)
    c_i = lax.broadcasted_iota(I32, (tm, tm), 1)
    lower = jnp.where(c_i < r_i, 1.0, 0.0).astype(BF16)
    lr = lax.broadcasted_iota(I32, (LANES, LANES), 0)
    lc = lax.broadcasted_iota(I32, (LANES, LANES), 1)
    before = jnp.where(lr < lc, 1.0, 0.0).astype(BF16)
    is_e_col = (lr >= ROUTER_LANE0) & (lr < ROUTER_LANE0 + N_EXPERTS)
    zrow = jnp.zeros((1, LANES), F32)
    row0 = lane * RUN

    def first_max(v):
        top = jnp.max(v, axis=-1, keepdims=True)
        return top, jnp.min(jnp.where(v == top, lane, far), axis=-1, keepdims=True)

    def at(sel, v):
        return jnp.sum(jnp.where(sel, v, 0.0), axis=-1, keepdims=True)

    oa_all = _pick(i, oa_refs, starts[0])
    oc_all = _pick(i, oc_refs, starts[1])
    os_all = _pick(i, os_refs, starts[2])
    x_all = _pick(i, x_refs, starts[3])
    carry = carry_ref[...]

    for sub in range(MERGE_SUB):
        rows_t = slice(sub * tm, (sub + 1) * tm)
        rows_8 = slice(sub * SUBLANES, (sub + 1) * SUBLANES)
        mix = jnp.concatenate([
            _rms(oa_all[rows_t], gn[:, :D_ATTN]),
            _rms(oc_all[rows_t], gn[:, D_ATTN:D_ATTN + D_CONV]),
            _rms(os_all[rows_t], gn[:, D_ATTN + D_CONV:]),
        ], axis=1).astype(BF16)
        x1 = x_all[rows_t] + jnp.dot(mix, wobf_ref[...], preferred_element_type=F32)
        x1_ref[rows_t, :] = x1
        xn = _rms(x1, fg_ref[...])
        xn_ref[rows_t, :] = _pack_pair(xn[:, :D_PACK], xn[:, D_PACK:])

        xh = xn.astype(BF16)
        xl = (xn - xh.astype(F32)).astype(BF16)
        wh = wrh_ref[...]
        logits = (jnp.dot(xh, wh, preferred_element_type=F32) + jnp.dot(xl, wh, preferred_element_type=F32)
                  + jnp.dot(xh, wrl_ref[...], preferred_element_type=F32) + br_ref[...])

        g_top, g_idx = first_max(jnp.where(is_group, logits, -jnp.inf))
        g_w = 1.0 / jnp.sum(jnp.where(is_group, jnp.exp(logits - g_top), 0.0), axis=-1, keepdims=True)
        e_lo = ROUTER_LANE0 + EXPERTS_PER_GROUP * g_idx
        el = jnp.where((lane >= e_lo) & (lane < e_lo + EXPERTS_PER_GROUP), logits, -jnp.inf)
        v1, i1 = first_max(el)
        v2, i2 = first_max(jnp.where(lane == i1, -jnp.inf, el))
        t2 = jnp.exp(v2 - v1)
        w1 = g_w / (1.0 + t2)
        w2 = g_w * t2 / (1.0 + t2)

        sel1 = lane == i1
        sel2 = lane == i2
        onehot = jnp.where(sel1 | sel2, 1.0, 0.0)
        rloc = jnp.dot(lower, onehot.astype(BF16), preferred_element_type=F32)
        cnt = jnp.sum(onehot, axis=0, keepdims=True)
        cnt_pad = jnp.floor((cnt + (RUN - 1.0)) * (1.0 / RUN)) * RUN
        stacked = jnp.concatenate([cnt_pad] + [zrow] * (SUBLANES - 1), axis=0).astype(BF16)
        lstart = jnp.dot(stacked, before, preferred_element_type=F32)[0:1, :]

        pos1, pos2 = at(sel1, lstart + rloc), at(sel2, lstart + rloc)
        tok = jnp.zeros((tm, LANES), F32)
        for col, val in ((TOK_W1, w1), (TOK_W2, w2), (TOK_P1, pos1), (TOK_P2, pos2)):
            tok = jnp.where(lane == col, val, tok)
        tokc_ref[rows_t, :] = tok[:, :tokc_ref.shape[1]]
        tokl_ref[rows_8, :] = jnp.transpose(tok)[:SUBLANES, :]

        per_expert = jnp.concatenate([lstart, lstart + cnt_pad, carry - lstart,
                                      jnp.zeros((LANES - 3, LANES), F32)], axis=0)
        cols = jnp.transpose(per_expert)
        own = is_e_col & (cols[:, 0:1] <= row0) & (row0 < cols[:, 1:2])
        chunk_e = jnp.sum(jnp.where(own, lr.astype(F32), 0.0), axis=0, keepdims=True)
        chunk_rel = row0 + jnp.sum(jnp.where(own, cols[:, 2:3], 0.0), axis=0, keepdims=True)
        n_chunk = jnp.sum(cnt_pad, axis=-1, keepdims=True) * (1.0 / RUN)
        rows = [zrow] * SUBLANES
        rows[TILE_NCHUNK] = jnp.broadcast_to(n_chunk, (1, LANES))
        rows[TILE_CHUNK_E], rows[TILE_CHUNK_REL] = chunk_e, chunk_rel
        tile_ref[rows_8, :] = jnp.concatenate(rows, axis=0).astype(I32)
        carry = carry + cnt_pad

    total = carry
    carry_ref[...] = total

    @pl.when(i == pl.num_programs(0) - 1)
    def _():
        is_e = (lane >= ROUTER_LANE0) & (lane < ROUTER_LANE0 + N_EXPERTS)
        ptiles = jnp.floor((total + (TM_MOE - 1.0)) * (1.0 / TM_MOE))
        upto = jnp.where(lr <= lc, 1.0, 0.0).astype(BF16)
        pt8 = jnp.concatenate([ptiles, jnp.zeros((SUBLANES - 1, LANES), F32)], axis=0).astype(BF16)
        tend = jnp.dot(pt8, upto, preferred_element_type=F32)[0:1, :]
        n_used = jnp.max(tend, axis=-1, keepdims=True)
        e_last = jnp.max(jnp.where(ptiles > 0.0, lane - ROUTER_LANE0, -1.0), axis=-1, keepdims=True)
        tend_col = jnp.transpose(jnp.broadcast_to(tend, (LANES, LANES)))

        def tile_expert(first_tile):
            hit = is_e_col & (tend_col <= lane + first_tile)
            return jnp.minimum(jnp.sum(jnp.where(hit, 1.0, 0.0), axis=0, keepdims=True), e_last)

        rows = [zrow] * SUBLANES
        rows[GLOB_START] = (tend - ptiles) * TM_MOE
        rows[GLOB_ZERO] = jnp.where(is_e & (total > 0.0), tend - 1.0, -1.0)
        rows[GLOB_TE] = tile_expert(0.0)
        rows[GLOB_TE_HI] = tile_expert(float(LANES))
        rows[GLOB_NUSED] = jnp.broadcast_to(n_used, (1, LANES))
        rows[GLOB_PTILES] = ptiles
        glob_ref[...] = jnp.concatenate(rows, axis=0).astype(I32)


def _merge_out(oa_src, oc_src, os_src, x_src, gn, wo, layer, fg, wr_hi, wr_lo, br):
    specs, starts, n_blk = [], [], None
    for src, width in ((oa_src, D_ATTN), (oc_src, D_CONV), (os_src, D_SSM), (x_src, D_MODEL)):
        sp, st, nb = _source_specs(src, MERGE_SUB * TM_OUT, width)
        assert n_blk in (None, nb)
        n_blk = nb
        specs += sp
        starts.append(st)
    n = n_blk * MERGE_SUB * TM_OUT
    row = lambda width: pl.BlockSpec((MERGE_SUB * TM_OUT, width), lambda i: (i, 0))
    tbl = lambda width: pl.BlockSpec((MERGE_SUB * SUBLANES, width), lambda i: (i, 0))
    return pl.pallas_call(
        functools.partial(_merge_out_kernel, starts=tuple(starts)),
        grid=(n_blk,),
        in_specs=specs + [_full((1, D_MODEL)), _layer_block((D_MODEL, D_MODEL), layer), _full((1, D_MODEL)),
                          _full((D_MODEL, LANES)), _full((D_MODEL, LANES)), _full((1, LANES))],
        out_specs=[row(D_MODEL), row(D_PACK), row(4), tbl(TM_OUT), tbl(LANES), _full((SUBLANES, LANES))],
        out_shape=[jax.ShapeDtypeStruct((n, D_MODEL), F32), jax.ShapeDtypeStruct((n, D_PACK), U32),
                   jax.ShapeDtypeStruct((n, 4), F32),
                   jax.ShapeDtypeStruct((n_blk * MERGE_SUB * SUBLANES, TM_OUT), F32),
                   jax.ShapeDtypeStruct((n_blk * MERGE_SUB * SUBLANES, LANES), I32),
                   jax.ShapeDtypeStruct((SUBLANES, LANES), I32)],
        scratch_shapes=[pltpu.VMEM((1, LANES), F32), pltpu.VMEM((D_MODEL, D_MODEL), BF16)],
        compiler_params=_params(("arbitrary",)),
        name="merge_out",
    )(*oa_src, *oc_src, *os_src, *x_src, gn.reshape(1, D_MODEL), wo, fg.reshape(1, D_MODEL), wr_hi, wr_lo, br)


def _chunk_groups(tile_ref):
    n_chunk = tile_ref[TILE_NCHUNK, 0]
    return lax.shift_right_logical(n_chunk + (CHUNK_GROUP - 1), CHUNK_GROUP.bit_length() - 1)


def _for_each_chunk(tile_ref, glob_ref, fn):
    n_chunk = tile_ref[TILE_NCHUNK, 0]
    n_group = _chunk_groups(tile_ref)

    def group(g, carry):
        for k in range(CHUNK_GROUP):
            c = g * CHUNK_GROUP + k
            seg = glob_ref[GLOB_START * LANES + tile_ref[TILE_CHUNK_E, c]]
            fn(pl.multiple_of(c * RUN, RUN), c < n_chunk, seg + tile_ref[TILE_CHUNK_REL, c], k)
        return carry

    lax.fori_loop(0, n_group, group, 0)
    return n_group * CHUNK_GROUP


def _wait_chunks(n, wait_one):
    def group(g, carry):
        for _ in range(CHUNK_GROUP):
            wait_one()
        return carry

    lax.fori_loop(0, lax.shift_right_logical(n, CHUNK_GROUP.bit_length() - 1), group, 0)


def _dispatch_kernel(tile_ref, glob_ref, tokl_ref, xn_ref, xs_ref, sbuf_ref, zbuf_ref, inflight_ref, sem_z, sem_r):
    i = pl.program_id(0)
    last = pl.num_programs(0) - 1
    slot = lax.rem(i, 2)
    trash0 = xs_ref.shape[0] - TRASH_ROWS

    def zero_copy(t):
        return pltpu.make_async_copy(zbuf_ref, xs_ref.at[pl.ds(pl.multiple_of(t * TM_MOE, TM_MOE), TM_MOE)], sem_z)

    def for_zero_tiles(fn):
        def seg_last(k, c):
            t = glob_ref[GLOB_ZERO * LANES + k]

            @pl.when(t >= 0)
            def _():
                fn(t)
            return c

        def unused(t, c):
            fn(t)
            return c

        lax.fori_loop(0, LANES, seg_last, 0)
        lax.fori_loop(glob_ref[GLOB_NUSED * LANES], trash0 // TM_MOE, unused, 0)

    @pl.when(i == 0)
    def _():
        zbuf_ref[...] = jnp.zeros(zbuf_ref.shape, U32)
        trash = pltpu.make_async_copy(zbuf_ref.at[pl.ds(0, TRASH_ROWS)], xs_ref.at[pl.ds(trash0, TRASH_ROWS)], sem_z)
        trash.start()
        for_zero_tiles(lambda t: zero_copy(t).start())
        for_zero_tiles(lambda t: zero_copy(t).wait())
        trash.wait()
        inflight_ref[0] = 0

    q = lax.broadcasted_iota(I32, (N_LOCAL, 1), 0).astype(F32)
    hit = (q == tokl_ref[TOK_P1:TOK_P1 + 1, :]) | (q == tokl_ref[TOK_P2:TOK_P2 + 1, :])
    sel = jnp.where(hit, 1.0, 0.0).astype(BF16)
    a, b = _unpack_pair(xn_ref[...])
    sa = jnp.dot(sel, a.astype(BF16), preferred_element_type=F32)
    sb = jnp.dot(sel, b.astype(BF16), preferred_element_type=F32)
    sbuf_ref[slot] = _pack_pair(sa, sb)

    def start_chunk(local_row, real, sorted_row, k):
        dst = pl.multiple_of(jnp.where(real, sorted_row, trash0 + k * RUN), RUN)
        pltpu.make_async_copy(sbuf_ref.at[slot, pl.ds(local_row, RUN)], xs_ref.at[pl.ds(dst, RUN)],
                              sem_r).start(priority=k % 2)

    def drain(n):
        _wait_chunks(n, lambda: pltpu.make_async_copy(sbuf_ref.at[0, pl.ds(0, RUN)], xs_ref.at[pl.ds(0, RUN)],
                                                      sem_r).wait())

    drain(inflight_ref[0])
    n_issued = _for_each_chunk(tile_ref, glob_ref, start_chunk)
    inflight_ref[0] = n_issued

    @pl.when(i == last)
    def _():
        drain(n_issued)


def _dispatch(tile_tbl, glob_flat, tokl, xn, n_rows_sorted):
    n = xn.shape[0]
    return pl.pallas_call(
        _dispatch_kernel,
        grid=(n // TM_OUT,),
        in_specs=[pl.BlockSpec((SUBLANES, LANES), lambda i: (i, 0), memory_space=pltpu.SMEM),
                  pl.BlockSpec(memory_space=pltpu.SMEM),
                  pl.BlockSpec((SUBLANES, TM_OUT), lambda i: (i, 0)),
                  pl.BlockSpec((TM_OUT, D_PACK), lambda i: (i, 0))],
        out_specs=pl.BlockSpec(memory_space=pl.ANY),
        out_shape=jax.ShapeDtypeStruct((n_rows_sorted, D_PACK), U32),
        scratch_shapes=[pltpu.VMEM((2, N_LOCAL, D_PACK), U32), pltpu.VMEM((TM_MOE, D_PACK), U32),
                        pltpu.SMEM((1,), I32), pltpu.SemaphoreType.DMA(()), pltpu.SemaphoreType.DMA(())],
        compiler_params=_params(("arbitrary",)),
        name="moe_dispatch",
    )(tile_tbl, glob_flat, tokl, xn)


def _tile_expert(glob_ref, i):
    return glob_ref[GLOB_TE * LANES + i]


def _moe_kernel(glob_ref, xs_ref, wg_hbm, wu_hbm, wd_hbm, ys_ref, wgb_ref, wub_ref, wdb_ref,
                sg_ref, su_ref, sd_ref, ord_ref, sem, *, layer):
    i = pl.program_id(0)
    used = i < glob_ref[GLOB_NUSED * LANES]
    expert = _tile_expert(glob_ref, i)
    new_expert = (i == 0) | (expert != _tile_expert(glob_ref, jnp.maximum(i - 1, 0)))

    def weight_copies(e, slot):
        return [pltpu.make_async_copy(src.at[layer, e], dst.at[slot], sem.at[slot])
                for src, dst in ((wg_hbm, sg_ref), (wu_hbm, su_ref), (wd_hbm, sd_ref))]

    @pl.when(i == 0)
    def _():
        ord_ref[0] = 0
        for cp in weight_copies(expert, 0):
            cp.start()

    @pl.when(used & new_expert)
    def _():
        slot = lax.rem(ord_ref[0], 2)
        for cp in weight_copies(expert, slot):
            cp.wait()
        nxt = lax.while_loop(
            lambda k: (k < N_EXPERTS) & (glob_ref[GLOB_PTILES * LANES + ROUTER_LANE0 + jnp.minimum(k, N_EXPERTS - 1)] == 0),
            lambda k: k + 1, expert + 1)

        @pl.when(nxt < N_EXPERTS)
        def _():
            for cp in weight_copies(nxt, 1 - slot):
                cp.start()

        wgb_ref[...] = sg_ref[slot].astype(BF16)
        wub_ref[...] = su_ref[slot].astype(BF16)
        wdb_ref[...] = sd_ref[slot].astype(BF16)
        ord_ref[0] = ord_ref[0] + 1

    @pl.when(used)
    def _():
        a, b = _unpack_pair(xs_ref[...])
        x = jnp.concatenate([a, b], axis=1).astype(BF16)
        gate = jnp.dot(x, wgb_ref[...], preferred_element_type=F32)
        up = jnp.dot(x, wub_ref[...], preferred_element_type=F32)
        h = (gate * jax.nn.sigmoid(gate) * up).astype(BF16)
        y = jnp.dot(h, wdb_ref[...], preferred_element_type=F32)
        ys_ref[...] = _pack_pair(y[:, :D_PACK], y[:, D_PACK:])

    @pl.when(jnp.logical_not(used))
    def _():
        zero = jnp.zeros(ys_ref.shape, F32)
        ys_ref[...] = _pack_pair(zero, zero)


def _moe(glob_flat, xs, wg, wu, wd, layer):
    n_tiles = (xs.shape[0] - TRASH_ROWS) // TM_MOE
    assert n_tiles <= 2 * LANES
    hbm = pl.BlockSpec(memory_space=pl.ANY)
    grid_spec = pltpu.PrefetchScalarGridSpec(
        num_scalar_prefetch=1,
        grid=(n_tiles,),
        in_specs=[pl.BlockSpec((TM_MOE, D_PACK), lambda i, g: (jnp.minimum(i, g[GLOB_NUSED * LANES] - 1), 0)),
                  hbm, hbm, hbm],
        out_specs=pl.BlockSpec((TM_MOE, D_PACK), lambda i, g: (i, 0)),
        scratch_shapes=[pltpu.VMEM((D_MODEL, D_EXPERT), BF16), pltpu.VMEM((D_MODEL, D_EXPERT), BF16),
                        pltpu.VMEM((D_EXPERT, D_MODEL), BF16),
                        pltpu.VMEM((2, D_MODEL, D_EXPERT), F32), pltpu.VMEM((2, D_MODEL, D_EXPERT), F32),
                        pltpu.VMEM((2, D_EXPERT, D_MODEL), F32), pltpu.SMEM((1,), I32),
                        pltpu.SemaphoreType.DMA((2,))],
    )
    return pl.pallas_call(
        functools.partial(_moe_kernel, layer=layer),
        grid_spec=grid_spec,
        out_shape=jax.ShapeDtypeStruct((n_tiles * TM_MOE, D_PACK), U32),
        compiler_params=_params(("arbitrary",)),
        name="moe_experts",
    )(glob_flat, xs, wg, wu, wd)


def _combine_kernel(tile_ref, next_ref, glob_ref, tokc_ref, x1_ref, ys_ref, fg_ref, *rest, n_first, final):
    n_out = 2 if final else 1
    out_refs = rest[:n_out]
    lbuf_ref, inflight_ref, sem = rest[n_out:]
    i = pl.program_id(0)
    last = pl.num_programs(0) - 1
    slot = lax.rem(i, 2)

    def gather(tbl_ref, dst_slot):
        def start(local_row, real, sorted_row, k):
            src = pl.multiple_of(jnp.where(real, sorted_row, 0), RUN)
            pltpu.make_async_copy(ys_ref.at[pl.ds(src, RUN)], lbuf_ref.at[dst_slot, pl.ds(local_row, RUN)],
                                  sem.at[dst_slot]).start(priority=k % 2)

        inflight_ref[dst_slot] = _for_each_chunk(tbl_ref, glob_ref, start)

    @pl.when(i == 0)
    def _():
        lbuf_ref[...] = jnp.zeros(lbuf_ref.shape, U32)
        gather(tile_ref, 0)

    @pl.when(i < last)
    def _():
        gather(next_ref, 1 - slot)

    _wait_chunks(inflight_ref[slot], lambda: pltpu.make_async_copy(
        ys_ref.at[pl.ds(0, RUN)], lbuf_ref.at[slot, pl.ds(0, RUN)], sem.at[slot]).wait())

    tokc = tokc_ref[...]
    w1 = tokc[:, TOK_W1:TOK_W1 + 1]
    w2 = tokc[:, TOK_W2:TOK_W2 + 1]
    col = lax.broadcasted_iota(I32, (1, N_LOCAL), 1).astype(F32)
    sel1 = jnp.where(col == tokc[:, TOK_P1:TOK_P1 + 1], 1.0, 0.0).astype(BF16)
    sel2 = jnp.where(col == tokc[:, TOK_P2:TOK_P2 + 1], 1.0, 0.0).astype(BF16)
    halves = []
    for part in _unpack_pair(lbuf_ref[slot]):
        rows = part.astype(BF16)
        halves.append(w1 * jnp.dot(sel1, rows, preferred_element_type=F32)
                      + w2 * jnp.dot(sel2, rows, preferred_element_type=F32))
    x2 = x1_ref[...] + jnp.concatenate(halves, axis=1)
    if not final:
        out_refs[0][...] = x2
    else:
        y = _rms(x2, fg_ref[...])

        @pl.when(i < n_first)
        def _():
            out_refs[0][...] = y

        @pl.when(i >= n_first)
        def _():
            out_refs[1][...] = y


def _combine(tile_tbl, glob_flat, tokc, x1, ys, fg, n_first_rows, final):
    n = x1.shape[0]
    n_blk = n // TM_OUT
    n_first = n_first_rows // TM_OUT
    assert n_first_rows % TM_OUT == 0
    row = lambda width: pl.BlockSpec((TM_OUT, width), lambda i: (i, 0))
    if final:
        out_specs = [pl.BlockSpec((TM_OUT, D_MODEL), lambda i: (jnp.minimum(i, n_first - 1), 0)),
                     pl.BlockSpec((TM_OUT, D_MODEL), lambda i: (jnp.maximum(i - n_first, 0), 0))]
        out_shape = [jax.ShapeDtypeStruct((n_first_rows, D_MODEL), F32),
                     jax.ShapeDtypeStruct((n - n_first_rows, D_MODEL), F32)]
    else:
        out_specs = [row(D_MODEL)]
        out_shape = [jax.ShapeDtypeStruct((n, D_MODEL), F32)]
    return pl.pallas_call(
        functools.partial(_combine_kernel, n_first=n_first, final=final),
        grid=(n_blk,),
        in_specs=[pl.BlockSpec((SUBLANES, LANES), lambda i: (i, 0), memory_space=pltpu.SMEM),
                  pl.BlockSpec((SUBLANES, LANES), lambda i: (jnp.minimum(i + 1, n_blk - 1), 0), memory_space=pltpu.SMEM),
                  pl.BlockSpec(memory_space=pltpu.SMEM),
                  row(4), row(D_MODEL), pl.BlockSpec(memory_space=pl.ANY), _full((1, D_MODEL))],
        out_specs=out_specs,
        out_shape=out_shape,
        scratch_shapes=[pltpu.VMEM((2, N_LOCAL, D_PACK), U32), pltpu.SMEM((2,), I32),
                        pltpu.SemaphoreType.DMA((2,))],
        compiler_params=_params(("arbitrary",)),
        name="moe_combine",
    )(tile_tbl, tile_tbl, glob_flat, tokc, x1, ys, fg.reshape(1, D_MODEL))


def _n_moe_tiles(n_tokens):
    n_runs = (n_tokens // TM_OUT) * N_EXPERTS
    return (2 * n_tokens + n_runs * (RUN - 1) + N_EXPERTS * (TM_MOE - 1)) // TM_MOE + 1


def kernel(x_prompt, x_sample, cache_k, cache_v, state_conv, state_ssm_re, state_ssm_im, attn_norm_g, w_in, attn_sinks, conv_w, conv_b, conv_ln_g, conv_ln_b, ssm_a_re, ssm_a_im, ssm_log_dt, ssm_b_re, ssm_b_im, ssm_c_re, ssm_c_im, ssm_d, ssm_glu_w, ssm_glu_b, grp_norm_g, w_out, ffn_norm_g, w_group_router, b_group_router, w_expert_router, b_expert_router, w_gate, w_up, w_down, final_norm_g):
    p = dict(ssm_a_re=ssm_a_re, ssm_a_im=ssm_a_im, ssm_log_dt=ssm_log_dt, ssm_b_re=ssm_b_re, ssm_b_im=ssm_b_im,
             ssm_c_re=ssm_c_re, ssm_c_im=ssm_c_im, ssm_d=ssm_d, ssm_glu_w=ssm_glu_w, ssm_glu_b=ssm_glu_b)
    depth = w_in.shape[0]
    n_seq, t, _ = x_prompt.shape
    n_dec, s_new, _ = x_sample.shape
    win = cache_k.shape[2]
    n_p = n_seq * t
    n_s = n_dec * s_new
    n = n_p + n_s
    assert n_seq == 2 and t % SCAN_T == 0 and t % CONV_T == 0 and n_p % TM_IN == 0 and n_s % TM_IN == 0
    hist = CONV_WIDTH - 1
    n_tiles = _n_moe_tiles(n)

    x_src = [x_prompt.reshape(n_p, D_MODEL), x_sample.reshape(n_s, D_MODEL)]
    outs = {k: [] for k in ("kp", "vp", "cp", "rp", "ip", "kv_new", "u_new", "rs", "is")}
    ck_all = cache_k.reshape(depth * n_dec, win, D_KV)
    cv_all = cache_v.reshape(depth * n_dec, win, D_KV)
    conv_all = state_conv.reshape(depth * n_dec, hist * D_CONV)
    for l in range(depth):
        q, kv, u, su = _in_proj(x_src, attn_norm_g[l], w_in, l)

        oa_p = _attn_prompt(q, kv, attn_sinks[l], n_seq, t)
        oa_s = _attn_sample(q, kv, ck_all, cv_all, attn_sinks[l], s_new, n_p, n_dec, l * n_dec)
        kv_p = kv[:n_p].reshape(n_seq, t, 2 * D_KV)[:, t - win:].astype(F32)
        outs["kp"].append(kv_p[..., :D_KV].reshape(n_seq, win, N_KV_HEADS, HEAD_DIM))
        outs["vp"].append(kv_p[..., D_KV:].reshape(n_seq, win, N_KV_HEADS, HEAD_DIM))
        outs["kv_new"].append(kv[n_p:].reshape(n_dec, s_new, 2 * D_KV))

        oc_p = _conv_prompt(u, conv_w[l], conv_b[l], conv_ln_g[l], conv_ln_b[l], n_seq, t)
        u_s = u[n_p:].reshape(n_dec, s_new, D_CONV)
        oc_s = _conv_sample(conv_all, u_s.reshape(n_dec, s_new * D_CONV),
                            conv_w[l], conv_b[l], conv_ln_g[l], conv_ln_b[l], s_new, l)
        outs["cp"].append(u[:n_p].reshape(n_seq, t, D_CONV)[:, t - hist:])
        outs["u_new"].append(u_s)

        sp = _s5_params(p, l)
        os0, os1, hre, him = _s5_prompt(su, sp, t)
        su_tb = su[n_p:].reshape(n_dec, s_new, D_SSM).transpose(1, 0, 2).reshape(n_s, D_SSM)
        os_tb, hr_s, hi_s = _s5_sample(su_tb, state_ssm_re[l].reshape(n_dec, N_STATE),
                                       state_ssm_im[l].reshape(n_dec, N_STATE), sp, s_new)
        os_s = os_tb.reshape(s_new, n_dec, D_SSM).transpose(1, 0, 2).reshape(n_s, D_SSM)
        outs["rp"].append(hre.reshape(n_seq, SSM_GROUPS, SSM_STATE))
        outs["ip"].append(him.reshape(n_seq, SSM_GROUPS, SSM_STATE))
        outs["rs"].append(hr_s.reshape(n_dec, SSM_GROUPS, SSM_STATE))
        outs["is"].append(hi_s.reshape(n_dec, SSM_GROUPS, SSM_STATE))

        unused = LANES - N_EXPERT_GROUPS - N_EXPERTS
        wr = jnp.pad(jnp.concatenate([w_group_router[l], w_expert_router[l]], axis=1), ((0, 0), (0, unused)))
        wr_hi = wr.astype(BF16)
        wr_lo = (wr - wr_hi.astype(F32)).astype(BF16)
        br = jnp.pad(jnp.concatenate([b_group_router[l], b_expert_router[l]]), (0, unused)).reshape(1, LANES)
        x1, xn, tokc, tokl, tile_tbl, glob = _merge_out(
            [oa_p, oa_s], [oc_p, oc_s.reshape(n_s, D_CONV)], [os0, os1, os_s], x_src,
            grp_norm_g[l], w_out, l, ffn_norm_g[l], wr_hi, wr_lo, br)

        glob_flat = glob.reshape(SUBLANES * LANES)
        xs = _dispatch(tile_tbl, glob_flat, tokl, xn, n_tiles * TM_MOE + TRASH_ROWS)
        ys = _moe(glob_flat, xs, w_gate, w_up, w_down, l)
        res = _combine(tile_tbl, glob_flat, tokc, x1, ys, final_norm_g, n_p, final=(l == depth - 1))
        x_src = [res[0]]

    y_p = res[0].reshape(n_seq, t, D_MODEL)
    y_s = res[1].reshape(n_dec, s_new, D_MODEL)
    st = lambda k: jnp.stack(outs[k])
    kv_new = st("kv_new").astype(F32)
    heads = lambda z: z.reshape(depth, n_dec, s_new, N_KV_HEADS, HEAD_DIM)
    k_s = jnp.concatenate([cache_k[:, :, s_new:], heads(kv_new[..., :D_KV])], axis=2)
    v_s = jnp.concatenate([cache_v[:, :, s_new:], heads(kv_new[..., D_KV:])], axis=2)
    conv_s = jnp.concatenate([state_conv[:, :, s_new:], st("u_new")], axis=2)
    return (y_p, y_s, st("kp"), st("vp"), st("cp"), st("rp"), st("ip"), k_s, v_s, conv_s, st("rs"), st("is"))
```

```python
import functools

import jax
import jax.numpy as jnp
from jax import lax
from jax.experimental import pallas as pl
from jax.experimental.pallas import tpu as pltpu

F32 = jnp.float32
BF16 = jnp.bfloat16
U32 = jnp.uint32
I32 = jnp.int32

D_MODEL = 1024
N_HEADS = 8
N_KV_HEADS = 2
HEAD_DIM = 64
WINDOW = 128
D_ATTN = N_HEADS * HEAD_DIM
D_KV = N_KV_HEADS * HEAD_DIM
D_CONV = 256
CONV_WIDTH = 31
D_SSM = 256
SSM_GROUPS = 16
SSM_GROUP_CH = 16
SSM_STATE = 64
N_STATE = SSM_GROUPS * SSM_STATE
D_IN = D_ATTN + 2 * D_KV + 2 * D_CONV + D_SSM
N_EXPERT_GROUPS = 4
EXPERTS_PER_GROUP = 8
N_EXPERTS = N_EXPERT_GROUPS * EXPERTS_PER_GROUP
D_EXPERT = 512
EPS = 1e-6
NEG_INF = -1e30
SCALE = HEAD_DIM ** -0.5

LANES = 128
SUBLANES = 8
HALF = LANES // 2
D_PACK = D_MODEL // 2

TM_IN = 512
TM_OUT = 256
TM_MOE = 512
CONV_T = 512
CONV_CHUNK = 64
CONV_HALO = 32
SCAN_T = 512
SCAN_PITCH = SCAN_T + SUBLANES
SAMPLE_BT = 16
ATTN_QB = 8
MERGE_SUB = 2
VMEM_LIMIT = 48 * 1024 * 1024

ROUTER_LANE0 = N_EXPERT_GROUPS
RUN = SUBLANES
N_LOCAL = 2 * TM_OUT + N_EXPERTS * RUN
TOK_W1, TOK_W2, TOK_P1, TOK_P2 = range(4)
TILE_NCHUNK, TILE_CHUNK_E, TILE_CHUNK_REL = range(3)
CHUNK_GROUP = 8
TRASH_ROWS = CHUNK_GROUP * RUN
GLOB_START, GLOB_ZERO, GLOB_TE, GLOB_TE_HI, GLOB_NUSED, GLOB_PTILES = range(6)


def _params(sem, vmem=VMEM_LIMIT):
    return pltpu.CompilerParams(dimension_semantics=sem, vmem_limit_bytes=vmem)


def _full(shape):
    zeros = (0,) * len(shape)
    return pl.BlockSpec(shape, lambda *_: zeros)


def _swap_halves(x):
    return jnp.concatenate([x[:, HALF:], x[:, :HALF]], axis=1)


def _rms(x, g):
    return x * lax.rsqrt(jnp.mean(x * x, axis=-1, keepdims=True) + EPS) * g


def _pack_pair(a, b):
    return pltpu.pack_elementwise([a, b], packed_dtype=BF16)


def _unpack_pair(p):
    return tuple(pltpu.unpack_elementwise(p, index=k, packed_dtype=BF16, unpacked_dtype=F32) for k in range(2))


def _pick(i, refs, starts):
    val = refs[0][...].astype(F32)
    for ref, start in zip(refs[1:], starts[1:]):
        val = jnp.where(i >= start, ref[...].astype(F32), val)
    return val


def _source_specs(sources, tm, width):
    specs, starts, start = [], [], 0
    for arr in sources:
        assert arr.shape[0] % tm == 0 and arr.shape[1] == width
        n_blk = arr.shape[0] // tm
        specs.append(pl.BlockSpec((tm, width), lambda i, s=start, nb=n_blk: (jnp.clip(i - s, 0, nb - 1), 0)))
        starts.append(start)
        start += n_blk
    return specs, tuple(starts), start


def _in_proj_kernel(*refs, starts):
    x_refs = refs[:len(starts)]
    g_ref, w_ref, q_ref, kv_ref, u_ref, su_ref, wbf_ref = refs[len(starts):]

    @pl.when(pl.program_id(0) == 0)
    def _():
        wbf_ref[...] = w_ref[...].astype(BF16)

    hn = _rms(_pick(pl.program_id(0), x_refs, starts), g_ref[...]).astype(BF16)
    z = jnp.dot(hn, wbf_ref[...], preferred_element_type=F32)
    q_ref[...] = z[:, :D_ATTN].astype(BF16)
    kv_ref[...] = z[:, D_ATTN:D_ATTN + 2 * D_KV].astype(BF16)
    c0 = D_ATTN + 2 * D_KV
    u_ref[...] = z[:, c0:c0 + D_CONV] * jax.nn.sigmoid(z[:, c0 + D_CONV:c0 + 2 * D_CONV])
    su_ref[...] = z[:, c0 + 2 * D_CONV:].astype(BF16)


def _layer_block(shape, layer):
    zeros = (0,) * len(shape)
    return pl.BlockSpec((None,) + tuple(shape), lambda *_: (layer,) + zeros)


def _in_proj(x_sources, g, w, layer):
    x_specs, starts, n_blk = _source_specs(x_sources, TM_IN, D_MODEL)
    n = n_blk * TM_IN
    row = lambda width: pl.BlockSpec((TM_IN, width), lambda i: (i, 0))
    return pl.pallas_call(
        functools.partial(_in_proj_kernel, starts=starts),
        grid=(n_blk,),
        in_specs=x_specs + [_full((1, D_MODEL)), _layer_block((D_MODEL, D_IN), layer)],
        out_specs=[row(D_ATTN), row(2 * D_KV), row(D_CONV), row(D_SSM)],
        out_shape=[jax.ShapeDtypeStruct((n, D_ATTN), BF16), jax.ShapeDtypeStruct((n, 2 * D_KV), BF16),
                   jax.ShapeDtypeStruct((n, D_CONV), F32), jax.ShapeDtypeStruct((n, D_SSM), BF16)],
        scratch_shapes=[pltpu.VMEM((D_MODEL, D_IN), BF16)],
        compiler_params=_params(("arbitrary",)),
        name="in_proj",
    )(*x_sources, g.reshape(1, D_MODEL), w)


def _softmax_pv(s, mask, sink, vmat):
    s = jnp.where(mask, s, NEG_INF)
    m = jnp.maximum(jnp.max(s, axis=-1, keepdims=True), sink)
    p = jnp.exp(s - m)
    denom = jnp.sum(p, axis=-1, keepdims=True) + jnp.exp(sink - m)
    return jnp.dot(p.astype(BF16), vmat, preferred_element_type=F32) / denom


def _attn_prompt_kernel(sink_ref, q_ref, kvc_ref, kvp_ref, o_ref):
    i = pl.program_id(1)
    lo = lax.broadcasted_iota(I32, (1, LANES), 1) < HALF
    a = lax.broadcasted_iota(I32, (WINDOW, 2 * WINDOW), 0)
    c = lax.broadcasted_iota(I32, (WINDOW, 2 * WINDOW), 1)
    diff = a + WINDOW - c
    band = (diff >= 0) & (diff < WINDOW)
    zero = jnp.zeros((WINDOW, LANES), BF16)
    for sub in range(ATTN_QB):
        rows = slice(sub * WINDOW, (sub + 1) * WINDOW)
        q = q_ref[rows, :] * jnp.asarray(SCALE, BF16)
        kvc = kvc_ref[rows, :]
        kvp = kvp_ref[...] if sub == 0 else kvc_ref[(sub - 1) * WINDOW:sub * WINDOW, :]
        mask = band & ((c >= WINDOW) | (i > 0)) if sub == 0 else band
        kk = jnp.concatenate([kvp[:, :LANES], kvc[:, :LANES]], axis=0)
        vv = jnp.concatenate([kvp[:, LANES:], kvc[:, LANES:]], axis=0)
        kk_sw = _swap_halves(kk)
        vv_sw = _swap_halves(vv)
        for j in range(D_ATTN // LANES):
            kvh = (2 * j) // (N_HEADS // N_KV_HEADS)
            qt = q[:, LANES * j:LANES * (j + 1)]
            mats = ((kk, vv), (kk_sw, vv_sw)) if kvh == 0 else ((kk_sw, vv_sw), (kk, vv))
            outs = []
            for par in range(2):
                kmat, vmat = mats[par]
                qm = jnp.where(lo if par == 0 else jnp.logical_not(lo), qt, zero)
                s = lax.dot_general(qm, kmat, (((1,), (1,)), ((), ())), preferred_element_type=F32)
                outs.append(_softmax_pv(s, mask, sink_ref[2 * j + par], vmat))
            o_ref[rows, LANES * j:LANES * (j + 1)] = jnp.where(lo, outs[0], outs[1]).astype(BF16)


def _attn_prompt(q, kv, sinks, n_seq, t):
    rows = ATTN_QB * WINDOW
    assert t % rows == 0
    nb = t // rows
    cur = lambda width: pl.BlockSpec((rows, width), lambda b, i: (b * nb + i, 0))
    prev = pl.BlockSpec((WINDOW, 2 * D_KV), lambda b, i: ((b * nb + i) * ATTN_QB - jnp.minimum(i, 1), 0))
    return pl.pallas_call(
        _attn_prompt_kernel,
        grid=(n_seq, nb),
        in_specs=[pl.BlockSpec(memory_space=pltpu.SMEM), cur(D_ATTN), cur(2 * D_KV), prev],
        out_specs=cur(D_ATTN),
        out_shape=jax.ShapeDtypeStruct((n_seq * t, D_ATTN), BF16),
        compiler_params=_params(("arbitrary", "arbitrary")),
        name="attn_prompt",
    )(sinks, q, kv, kv)


def _attn_sample_kernel(sink_ref, q_ref, kv_ref, ck_ref, cv_ref, o_ref, qf_ref, kvf_ref, *, s_new):
    rows = SUBLANES
    n_pair = q_ref.shape[0] // rows
    per = rows // s_new
    qf_ref[...] = q_ref[...].astype(F32) * SCALE
    kvf_ref[...] = kv_ref[...].astype(F32)
    lane = lax.broadcasted_iota(I32, (1, LANES), 1)
    lo = lane < HALF
    rid = lax.broadcasted_iota(I32, (N_HEADS * rows, 1), 0)
    head = rid // rows
    seq = (rid % rows) // s_new
    tok = rid % s_new
    sink = jnp.zeros((N_HEADS * rows, 1), F32)
    for h in range(N_HEADS):
        sink = jnp.where(head == h, sink_ref[h], sink)
    mask_c = lane > tok

    def pair(p, carry):
        r0 = pl.multiple_of(p * rows, rows)
        q8 = qf_ref[pl.ds(r0, rows), :]
        kv8 = kvf_ref[pl.ds(r0, rows), :]
        knew = kv8[:, :LANES]
        vnew = kv8[:, LANES:]
        pieces = []
        for h in range(N_HEADS):
            qt = q8[:, LANES * (h // 2):LANES * (h // 2 + 1)]
            tgt = h // (N_HEADS // N_KV_HEADS)
            if h % 2 != tgt:
                qt = _swap_halves(qt)
            pieces.append(jnp.where(lo if tgt == 0 else jnp.logical_not(lo), qt, 0.0))
        qm = jnp.concatenate(pieces, axis=0)
        qb = qm.astype(BF16)
        s_c = jnp.zeros((N_HEADS * rows, LANES), F32)
        for bb in range(per):
            kc = ck_ref[p * per + bb].astype(BF16)
            s_bb = lax.dot_general(qb, kc, (((1,), (1,)), ((), ())), preferred_element_type=F32)
            s_c = jnp.where(seq == bb, s_bb, s_c)
        s_c = jnp.where(mask_c, s_c, NEG_INF)
        m = jnp.maximum(jnp.max(s_c, axis=-1, keepdims=True), sink)
        s_n = []
        for k in range(rows):
            valid = (seq == k // s_new) & (tok >= k % s_new)
            sk = jnp.sum(qm * knew[k:k + 1, :], axis=-1, keepdims=True)
            sk = jnp.where(valid, sk, NEG_INF)
            s_n.append(sk)
            m = jnp.maximum(m, sk)
        p_c = jnp.exp(s_c - m)
        denom = jnp.sum(p_c, axis=-1, keepdims=True) + jnp.exp(sink - m)
        pb = p_c.astype(BF16)
        o = jnp.zeros((N_HEADS * rows, LANES), F32)
        for bb in range(per):
            vc = cv_ref[p * per + bb].astype(BF16)
            o = jnp.where(seq == bb, jnp.dot(pb, vc, preferred_element_type=F32), o)
        for k in range(rows):
            pk = jnp.exp(s_n[k] - m)
            denom = denom + pk
            o = o + pk.astype(BF16).astype(F32) * vnew[k:k + 1, :]
        o = o / denom
        for j in range(D_ATTN // LANES):
            kvh = (2 * j) // (N_HEADS // N_KV_HEADS)
            pe = o[rows * 2 * j:rows * (2 * j + 1), :]
            po = o[rows * (2 * j + 1):rows * (2 * j + 2), :]
            if kvh == 0:
                po = _swap_halves(po)
            else:
                pe = _swap_halves(pe)
            o_ref[pl.ds(r0, rows), LANES * j:LANES * (j + 1)] = jnp.where(lo, pe, po)
        return carry

    lax.fori_loop(0, n_pair, pair, 0)


def _attn_sample(q, kv, cache_k, cache_v, sinks, s_new, row0, n_seq, seq0):
    win = cache_k.shape[1]
    n_rows = n_seq * s_new
    rows_blk = SAMPLE_BT * s_new
    assert win == WINDOW and SUBLANES % s_new == 0 and n_seq % SAMPLE_BT == 0 and row0 % rows_blk == 0
    assert seq0 % SAMPLE_BT == 0
    blk0 = row0 // rows_blk
    cblk0 = seq0 // SAMPLE_BT
    return pl.pallas_call(
        functools.partial(_attn_sample_kernel, s_new=s_new),
        grid=(n_seq // SAMPLE_BT,),
        in_specs=[pl.BlockSpec(memory_space=pltpu.SMEM),
                  pl.BlockSpec((rows_blk, D_ATTN), lambda i: (blk0 + i, 0)),
                  pl.BlockSpec((rows_blk, 2 * D_KV), lambda i: (blk0 + i, 0)),
                  pl.BlockSpec((SAMPLE_BT, win, D_KV), lambda i: (cblk0 + i, 0, 0)),
                  pl.BlockSpec((SAMPLE_BT, win, D_KV), lambda i: (cblk0 + i, 0, 0))],
        out_specs=pl.BlockSpec((rows_blk, D_ATTN), lambda i: (i, 0)),
        out_shape=jax.ShapeDtypeStruct((n_rows, D_ATTN), F32),
        scratch_shapes=[pltpu.VMEM((rows_blk, D_ATTN), F32), pltpu.VMEM((rows_blk, 2 * D_KV), F32)],
        compiler_params=_params(("arbitrary",)),
        name="attn_sample",
    )(sinks, q, kv, cache_k, cache_v)


def _ln_silu(y, lg, lb):
    mu = jnp.mean(y, axis=-1, keepdims=True)
    var = jnp.mean(jnp.square(y - mu), axis=-1, keepdims=True)
    yn = (y - mu) * lax.rsqrt(var + EPS) * lg + lb
    return yn * jax.nn.sigmoid(yn)


def _conv_prompt_kernel(u_ref, w_ref, b_ref, lg_ref, lb_ref, o_ref, ext_ref):
    i = pl.program_id(1)

    @pl.when(i == 0)
    def _():
        ext_ref[0:CONV_HALO, :] = jnp.zeros((CONV_HALO, D_CONV), F32)
        ext_ref[CONV_HALO + CONV_T:, :] = jnp.zeros((SUBLANES, D_CONV), F32)

    @pl.when(i > 0)
    def _():
        ext_ref[0:CONV_HALO, :] = ext_ref[CONV_T:CONV_T + CONV_HALO, :]

    ext_ref[CONV_HALO:CONV_HALO + CONV_T, :] = u_ref[...]
    shift = CONV_HALO - (CONV_WIDTH - 1)
    for cidx in range(CONV_T // CONV_CHUNK):
        r0 = cidx * CONV_CHUNK
        acc = jnp.zeros((CONV_CHUNK, D_CONV), F32)
        for rho in range(SUBLANES):
            part = jnp.zeros((CONV_CHUNK + SUBLANES, D_CONV), F32)
            for j in range(CONV_WIDTH):
                if (j + shift) % SUBLANES == rho:
                    base = r0 + j + shift - rho
                    part = part + w_ref[j:j + 1, :] * ext_ref[base:base + CONV_CHUNK + SUBLANES, :]
            acc = acc + part[rho:rho + CONV_CHUNK, :]
        o_ref[r0:r0 + CONV_CHUNK, :] = _ln_silu(acc + b_ref[...], lg_ref[...], lb_ref[...]).astype(BF16)


def _conv_prompt(u, w, b, lg, lb, n_seq, t):
    nt = t // CONV_T
    vec = _full((1, D_CONV))
    return pl.pallas_call(
        _conv_prompt_kernel,
        grid=(n_seq, nt),
        in_specs=[pl.BlockSpec((CONV_T, D_CONV), lambda s, i: (s * nt + i, 0)),
                  _full((CONV_WIDTH, D_CONV)), vec, vec, vec],
        out_specs=pl.BlockSpec((CONV_T, D_CONV), lambda s, i: (s * nt + i, 0)),
        out_shape=jax.ShapeDtypeStruct((n_seq * t, D_CONV), BF16),
        scratch_shapes=[pltpu.VMEM((CONV_T + CONV_HALO + SUBLANES, D_CONV), F32)],
        compiler_params=_params(("arbitrary", "arbitrary")),
        name="conv_prompt",
    )(u, w, b.reshape(1, D_CONV), lg.reshape(1, D_CONV), lb.reshape(1, D_CONV))


def _conv_sample_kernel(st_ref, u_ref, w_ref, b_ref, lg_ref, lb_ref, o_ref, *, s_new):
    hist = CONV_WIDTH - 1
    for t in range(s_new):
        acc = jnp.zeros((st_ref.shape[0], D_CONV), F32)
        for j in range(CONV_WIDTH):
            idx = t + j
            if idx < hist:
                piece = st_ref[:, idx * D_CONV:(idx + 1) * D_CONV]
            else:
                piece = u_ref[:, (idx - hist) * D_CONV:(idx - hist + 1) * D_CONV]
            acc = acc + w_ref[j:j + 1, :] * piece
        o_ref[:, t * D_CONV:(t + 1) * D_CONV] = _ln_silu(acc + b_ref[...], lg_ref[...], lb_ref[...])


def _conv_sample(state2d, u2d, w, b, lg, lb, s_new, layer):
    n_seq = u2d.shape[0]
    vec = _full((1, D_CONV))
    return pl.pallas_call(
        functools.partial(_conv_sample_kernel, s_new=s_new),
        grid=(1,),
        in_specs=[pl.BlockSpec((n_seq, state2d.shape[1]), lambda i: (layer, 0)), _full(u2d.shape),
                  _full((CONV_WIDTH, D_CONV)), vec, vec, vec],
        out_specs=_full((n_seq, s_new * D_CONV)),
        out_shape=jax.ShapeDtypeStruct((n_seq, s_new * D_CONV), F32),
        compiler_params=_params(("arbitrary",)),
        name="conv_sample",
    )(state2d, u2d, w, b.reshape(1, D_CONV), lg.reshape(1, D_CONV), lb.reshape(1, D_CONV))


def _s5_discretize(a_re, a_im, log_dt):
    dt = jnp.exp(log_dt)
    mag = jnp.exp(a_re * dt)
    ang = a_im * dt
    lr = mag * jnp.cos(ang)
    li = mag * jnp.sin(ang)
    den = a_re * a_re + a_im * a_im
    cr = ((lr - 1.0) * a_re + li * a_im) / den
    ci = (li * a_re - (lr - 1.0) * a_im) / den
    return lr, li, cr, ci


def _s5_bbar(arow_ref, bre_ref, bim_ref):
    _, _, cr, ci = _s5_discretize(arow_ref[0:1, :], arow_ref[1:2, :], arow_ref[2:3, :])
    bre = bre_ref[...]
    bim = bim_ref[...]
    return (cr * bre - ci * bim).astype(BF16), (cr * bim + ci * bre).astype(BF16)


def _s5_readout(h_re, h_im, u, cre_ref, cim_ref, d_ref, gw_ref, gb_ref):
    y = (jnp.dot(h_re.astype(BF16), cre_ref[...], preferred_element_type=F32)
         - jnp.dot(h_im.astype(BF16), cim_ref[...], preferred_element_type=F32)
         + d_ref[...] * u.astype(F32))
    z = jax.nn.gelu(y)
    gate = jnp.dot(z.astype(BF16), gw_ref[...], preferred_element_type=F32) + gb_ref[...]
    return (z * jax.nn.sigmoid(gate)).astype(BF16)


def _s5_prompt_kernel(su0_ref, su1_ref, arow_ref, atile_ref, bre_ref, bim_ref, cre_ref, cim_ref, d_ref,
                      gw_ref, gb_ref, o0_ref, o1_ref, hre_ref, him_ref,
                      bbr_ref, bbi_ref, lam_ref, car_ref, bur_ref, bui_ref, hbr_ref, hbi_ref):
    i = pl.program_id(0)
    n_slab = N_STATE // LANES
    su_refs = (su0_ref, su1_ref)
    o_refs = (o0_ref, o1_ref)

    @pl.when(i == 0)
    def _():
        bbr, bbi = _s5_bbar(arow_ref, bre_ref, bim_ref)
        bbr_ref[...] = bbr
        bbi_ref[...] = bbi
        lr, li, _, _ = _s5_discretize(atile_ref[0], atile_ref[1], atile_ref[2])
        lam_ref[0] = lr
        lam_ref[1] = li
        car_ref[...] = jnp.zeros(car_ref.shape, F32)

    for s in range(2):
        u = su_refs[s][...]
        br = jnp.dot(u, bbr_ref[...], preferred_element_type=F32)
        bi = jnp.dot(u, bbi_ref[...], preferred_element_type=F32)
        for j in range(n_slab):
            bur_ref[s, j * SCAN_PITCH:j * SCAN_PITCH + SCAN_T, :] = br[:, LANES * j:LANES * (j + 1)]
            bui_ref[s, j * SCAN_PITCH:j * SCAN_PITCH + SCAN_T, :] = bi[:, LANES * j:LANES * (j + 1)]

    lr = lam_ref[0]
    li = lam_ref[1]

    def step(t, carry):
        new = []
        for s in range(2):
            hr, hi = carry[2 * s], carry[2 * s + 1]
            rows = pl.ds(t, n_slab, stride=SCAN_PITCH)
            nr = lr * hr - li * hi + bur_ref.at[s][rows, :]
            ni = lr * hi + li * hr + bui_ref.at[s][rows, :]
            hbr_ref.at[s][rows, :] = nr
            hbi_ref.at[s][rows, :] = ni
            new += [nr, ni]
        return tuple(new)

    carry = lax.fori_loop(0, SCAN_T, step, tuple(car_ref[k] for k in range(4)), unroll=8)
    for k in range(4):
        car_ref[k] = carry[k]

    @pl.when(i == pl.num_programs(0) - 1)
    def _():
        for s in range(2):
            hre_ref[s] = carry[2 * s]
            him_ref[s] = carry[2 * s + 1]

    for s in range(2):
        h_re = jnp.concatenate([hbr_ref[s, j * SCAN_PITCH:j * SCAN_PITCH + SCAN_T, :] for j in range(n_slab)], axis=1)
        h_im = jnp.concatenate([hbi_ref[s, j * SCAN_PITCH:j * SCAN_PITCH + SCAN_T, :] for j in range(n_slab)], axis=1)
        o_refs[s][...] = _s5_readout(h_re, h_im, su_refs[s][...], cre_ref, cim_ref, d_ref, gw_ref, gb_ref)


def _s5_prompt(su, prm, t):
    nt = t // SCAN_T
    n_slab = N_STATE // LANES
    blk0 = pl.BlockSpec((SCAN_T, D_SSM), lambda i: (i, 0))
    blk1 = pl.BlockSpec((SCAN_T, D_SSM), lambda i: (nt + i, 0))
    oblk = pl.BlockSpec((SCAN_T, D_SSM), lambda i: (i, 0))
    state = pl.BlockSpec((2, SUBLANES, LANES), lambda i: (0, 0, 0))
    slabs = pltpu.VMEM((2, n_slab * SCAN_PITCH, LANES), F32)
    return pl.pallas_call(
        _s5_prompt_kernel,
        grid=(nt,),
        in_specs=[blk0, blk1, _full((3, N_STATE)), _full((3, SUBLANES, LANES)),
                  _full((D_SSM, N_STATE)), _full((D_SSM, N_STATE)), _full((N_STATE, D_SSM)), _full((N_STATE, D_SSM)),
                  _full((1, D_SSM)), _full((D_SSM, D_SSM)), _full((1, D_SSM))],
        out_specs=[oblk, oblk, state, state],
        out_shape=[jax.ShapeDtypeStruct((t, D_SSM), BF16), jax.ShapeDtypeStruct((t, D_SSM), BF16),
                   jax.ShapeDtypeStruct((2, SUBLANES, LANES), F32), jax.ShapeDtypeStruct((2, SUBLANES, LANES), F32)],
        scratch_shapes=[pltpu.VMEM((D_SSM, N_STATE), BF16), pltpu.VMEM((D_SSM, N_STATE), BF16),
                        pltpu.VMEM((2, SUBLANES, LANES), F32), pltpu.VMEM((4, SUBLANES, LANES), F32),
                        slabs, slabs, slabs, slabs],
        compiler_params=_params(("arbitrary",)),
        name="s5_prompt",
    )(su, su, prm["arow"], prm["atile"], prm["bre"], prm["bim"], prm["cre"], prm["cim"],
      prm["d"], prm["gw"], prm["gb"])


def _s5_sample_kernel(su_ref, h0r_ref, h0i_ref, arow_ref, bre_ref, bim_ref, cre_ref, cim_ref, d_ref, gw_ref, gb_ref,
                      o_ref, hr_ref, hi_ref, hbr_ref, hbi_ref, *, s_new):
    n_seq = h0r_ref.shape[0]
    lr, li, _, _ = _s5_discretize(arow_ref[0:1, :], arow_ref[1:2, :], arow_ref[2:3, :])
    bbr, bbi = _s5_bbar(arow_ref, bre_ref, bim_ref)
    u = su_ref[...]
    hbr_ref[...] = jnp.dot(u, bbr, preferred_element_type=F32)
    hbi_ref[...] = jnp.dot(u, bbi, preferred_element_type=F32)
    hr_ref[...] = h0r_ref[...]
    hi_ref[...] = h0i_ref[...]
    for t in range(s_new):
        rows = slice(t * n_seq, (t + 1) * n_seq)
        hr = hr_ref[...]
        hi = hi_ref[...]
        nr = lr * hr - li * hi + hbr_ref[rows, :]
        ni = lr * hi + li * hr + hbi_ref[rows, :]
        hbr_ref[rows, :] = nr
        hbi_ref[rows, :] = ni
        hr_ref[...] = nr
        hi_ref[...] = ni
    o_ref[...] = _s5_readout(hbr_ref[...], hbi_ref[...], u, cre_ref, cim_ref, d_ref, gw_ref, gb_ref)


def _s5_sample(su_tb, h0r, h0i, prm, s_new):
    n_rows = su_tb.shape[0]
    n_seq = n_rows // s_new
    return pl.pallas_call(
        functools.partial(_s5_sample_kernel, s_new=s_new),
        out_shape=[jax.ShapeDtypeStruct((n_rows, D_SSM), BF16),
                   jax.ShapeDtypeStruct((n_seq, N_STATE), F32), jax.ShapeDtypeStruct((n_seq, N_STATE), F32)],
        scratch_shapes=[pltpu.VMEM((n_rows, N_STATE), F32), pltpu.VMEM((n_rows, N_STATE), F32)],
        compiler_params=pltpu.CompilerParams(vmem_limit_bytes=VMEM_LIMIT),
        name="s5_sample",
    )(su_tb, h0r, h0i, prm["arow"], prm["bre"], prm["bim"], prm["cre"], prm["cim"], prm["d"], prm["gw"], prm["gb"])


def _s5_params(p, l):
    eye = jnp.eye(SSM_GROUPS, dtype=F32)

    def b_diag(b):
        return jnp.einsum("gnc,gh->gchn", b, eye).reshape(D_SSM, N_STATE)

    def c_diag(c):
        return jnp.einsum("gcn,gh->gnhc", c, eye).reshape(N_STATE, D_SSM)

    ldt = jnp.broadcast_to(p["ssm_log_dt"][l][:, None], (SSM_GROUPS, SSM_STATE))
    a3 = jnp.stack([p["ssm_a_re"][l], p["ssm_a_im"][l], ldt])
    return {
        "arow": a3.reshape(3, N_STATE),
        "atile": a3.reshape(3, SUBLANES, LANES),
        "bre": b_diag(p["ssm_b_re"][l]), "bim": b_diag(p["ssm_b_im"][l]),
        "cre": c_diag(p["ssm_c_re"][l]).astype(BF16), "cim": c_diag(p["ssm_c_im"][l]).astype(BF16),
        "d": p["ssm_d"][l].reshape(1, D_SSM),
        "gw": p["ssm_glu_w"][l].astype(BF16), "gb": p["ssm_glu_b"][l].reshape(1, D_SSM),
    }


def _merge_out_kernel(*refs, starts):
    it = iter(refs)
    oa_refs, oc_refs, os_refs, x_refs = ([next(it) for _ in s] for s in starts)
    gn_ref, wo_ref, fg_ref, wrh_ref, wrl_ref, br_ref = (next(it) for _ in range(6))
    x1_ref, xn_ref, tokc_ref, tokl_ref, tile_ref, glob_ref = (next(it) for _ in range(6))
    carry_ref, wobf_ref = next(it), next(it)
    i = pl.program_id(0)
    tm = TM_OUT

    @pl.when(i == 0)
    def _():
        carry_ref[...] = jnp.zeros(carry_ref.shape, F32)
        wobf_ref[...] = wo_ref[...].astype(BF16)

    gn = gn_ref[...]
    lane = lax.broadcasted_iota(I32, (1, LANES), 1).astype(F32)
    far = float(LANES)
    is_group = lane < N_EXPERT_GROUPS
    r_i = lax.broadcasted_iota(I32, (tm, tm), 0)
    c_i = lax.broadcasted_iota(I32, (tm, tm), 1)
    lower = jnp.where(c_i < r_i, 1.0, 0.0).astype(BF16)
    lr = lax.broadcasted_iota(I32, (LANES, LANES), 0)
    lc = lax.broadcasted_iota(I32, (LANES, LANES), 1)
    before = jnp.where(lr < lc, 1.0, 0.0).astype(BF16)
    is_e_col = (lr >= ROUTER_LANE0) & (lr < ROUTER_LANE0 + N_EXPERTS)
    zrow = jnp.zeros((1, LANES), F32)
    row0 = lane * RUN

    def first_max(v):
        top = jnp.max(v, axis=-1, keepdims=True)
        return top, jnp.min(jnp.where(v == top, lane, far), axis=-1, keepdims=True)

    def at(sel, v):
        return jnp.sum(jnp.where(sel, v, 0.0), axis=-1, keepdims=True)

    oa_all = _pick(i, oa_refs, starts[0])
    oc_all = _pick(i, oc_refs, starts[1])
    os_all = _pick(i, os_refs, starts[2])
    x_all = _pick(i, x_refs, starts[3])
    carry = carry_ref[...]

    for sub in range(MERGE_SUB):
        rows_t = slice(sub * tm, (sub + 1) * tm)
        rows_8 = slice(sub * SUBLANES, (sub + 1) * SUBLANES)
        mix = jnp.concatenate([
            _rms(oa_all[rows_t], gn[:, :D_ATTN]),
            _rms(oc_all[rows_t], gn[:, D_ATTN:D_ATTN + D_CONV]),
            _rms(os_all[rows_t], gn[:, D_ATTN + D_CONV:]),
        ], axis=1).astype(BF16)
        x1 = x_all[rows_t] + jnp.dot(mix, wobf_ref[...], preferred_element_type=F32)
        x1_ref[rows_t, :] = x1
        xn = _rms(x1, fg_ref[...])
        xn_ref[rows_t, :] = _pack_pair(xn[:, :D_PACK], xn[:, D_PACK:])

        xh = xn.astype(BF16)
        xl = (xn - xh.astype(F32)).astype(BF16)
        wh = wrh_ref[...]
        logits = (jnp.dot(xh, wh, preferred_element_type=F32) + jnp.dot(xl, wh, preferred_element_type=F32)
                  + jnp.dot(xh, wrl_ref[...], preferred_element_type=F32) + br_ref[...])

        g_top, g_idx = first_max(jnp.where(is_group, logits, -jnp.inf))
        g_w = 1.0 / jnp.sum(jnp.where(is_group, jnp.exp(logits - g_top), 0.0), axis=-1, keepdims=True)
        e_lo = ROUTER_LANE0 + EXPERTS_PER_GROUP * g_idx
        el = jnp.where((lane >= e_lo) & (lane < e_lo + EXPERTS_PER_GROUP), logits, -jnp.inf)
        v1, i1 = first_max(el)
        v2, i2 = first_max(jnp.where(lane == i1, -jnp.inf, el))
        t2 = jnp.exp(v2 - v1)
        w1 = g_w / (1.0 + t2)
        w2 = g_w * t2 / (1.0 + t2)

        sel1 = lane == i1
        sel2 = lane == i2
        onehot = jnp.where(sel1 | sel2, 1.0, 0.0)
        rloc = jnp.dot(lower, onehot.astype(BF16), preferred_element_type=F32)
        cnt = jnp.sum(onehot, axis=0, keepdims=True)
        cnt_pad = jnp.floor((cnt + (RUN - 1.0)) * (1.0 / RUN)) * RUN
        stacked = jnp.concatenate([cnt_pad] + [zrow] * (SUBLANES - 1), axis=0).astype(BF16)
        lstart = jnp.dot(stacked, before, preferred_element_type=F32)[0:1, :]

        pos1, pos2 = at(sel1, lstart + rloc), at(sel2, lstart + rloc)
        tok = jnp.zeros((tm, LANES), F32)
        for col, val in ((TOK_W1, w1), (TOK_W2, w2), (TOK_P1, pos1), (TOK_P2, pos2)):
            tok = jnp.where(lane == col, val, tok)
        tokc_ref[rows_t, :] = tok[:, :tokc_ref.shape[1]]
        tokl_ref[rows_8, :] = jnp.transpose(tok)[:SUBLANES, :]

        per_expert = jnp.concatenate([lstart, lstart + cnt_pad, carry - lstart,
                                      jnp.zeros((LANES - 3, LANES), F32)], axis=0)
        cols = jnp.transpose(per_expert)
        own = is_e_col & (cols[:, 0:1] <= row0) & (row0 < cols[:, 1:2])
        chunk_e = jnp.sum(jnp.where(own, lr.astype(F32), 0.0), axis=0, keepdims=True)
        chunk_rel = row0 + jnp.sum(jnp.where(own, cols[:, 2:3], 0.0), axis=0, keepdims=True)
        n_chunk = jnp.sum(cnt_pad, axis=-1, keepdims=True) * (1.0 / RUN)
        rows = [zrow] * SUBLANES
        rows[TILE_NCHUNK] = jnp.broadcast_to(n_chunk, (1, LANES))
        rows[TILE_CHUNK_E], rows[TILE_CHUNK_REL] = chunk_e, chunk_rel
        tile_ref[rows_8, :] = jnp.concatenate(rows, axis=0).astype(I32)
        carry = carry + cnt_pad

    total = carry
    carry_ref[...] = total

    @pl.when(i == pl.num_programs(0) - 1)
    def _():
        is_e = (lane >= ROUTER_LANE0) & (lane < ROUTER_LANE0 + N_EXPERTS)
        ptiles = jnp.floor((total + (TM_MOE - 1.0)) * (1.0 / TM_MOE))
        upto = jnp.where(lr <= lc, 1.0, 0.0).astype(BF16)
        pt8 = jnp.concatenate([ptiles, jnp.zeros((SUBLANES - 1, LANES), F32)], axis=0).astype(BF16)
        tend = jnp.dot(pt8, upto, preferred_element_type=F32)[0:1, :]
        n_used = jnp.max(tend, axis=-1, keepdims=True)
        e_last = jnp.max(jnp.where(ptiles > 0.0, lane - ROUTER_LANE0, -1.0), axis=-1, keepdims=True)
        tend_col = jnp.transpose(jnp.broadcast_to(tend, (LANES, LANES)))

        def tile_expert(first_tile):
            hit = is_e_col & (tend_col <= lane + first_tile)
            return jnp.minimum(jnp.sum(jnp.where(hit, 1.0, 0.0), axis=0, keepdims=True), e_last)

        rows = [zrow] * SUBLANES
        rows[GLOB_START] = (tend - ptiles) * TM_MOE
        rows[GLOB_ZERO] = jnp.where(is_e & (total > 0.0), tend - 1.0, -1.0)
        rows[GLOB_TE] = tile_expert(0.0)
        rows[GLOB_TE_HI] = tile_expert(float(LANES))
        rows[GLOB_NUSED] = jnp.broadcast_to(n_used, (1, LANES))
        rows[GLOB_PTILES] = ptiles
        glob_ref[...] = jnp.concatenate(rows, axis=0).astype(I32)


def _merge_out(oa_src, oc_src, os_src, x_src, gn, wo, layer, fg, wr_hi, wr_lo, br):
    specs, starts, n_blk = [], [], None
    for src, width in ((oa_src, D_ATTN), (oc_src, D_CONV), (os_src, D_SSM), (x_src, D_MODEL)):
        sp, st, nb = _source_specs(src, MERGE_SUB * TM_OUT, width)
        assert n_blk in (None, nb)
        n_blk = nb
        specs += sp
        starts.append(st)
    n = n_blk * MERGE_SUB * TM_OUT
    row = lambda width: pl.BlockSpec((MERGE_SUB * TM_OUT, width), lambda i: (i, 0))
    tbl = lambda width: pl.BlockSpec((MERGE_SUB * SUBLANES, width), lambda i: (i, 0))
    return pl.pallas_call(
        functools.partial(_merge_out_kernel, starts=tuple(starts)),
        grid=(n_blk,),
        in_specs=specs + [_full((1, D_MODEL)), _layer_block((D_MODEL, D_MODEL), layer), _full((1, D_MODEL)),
                          _full((D_MODEL, LANES)), _full((D_MODEL, LANES)), _full((1, LANES))],
        out_specs=[row(D_MODEL), row(D_PACK), row(4), tbl(TM_OUT), tbl(LANES), _full((SUBLANES, LANES))],
        out_shape=[jax.ShapeDtypeStruct((n, D_MODEL), F32), jax.ShapeDtypeStruct((n, D_PACK), U32),
                   jax.ShapeDtypeStruct((n, 4), F32),
                   jax.ShapeDtypeStruct((n_blk * MERGE_SUB * SUBLANES, TM_OUT), F32),
                   jax.ShapeDtypeStruct((n_blk * MERGE_SUB * SUBLANES, LANES), I32),
                   jax.ShapeDtypeStruct((SUBLANES, LANES), I32)],
        scratch_shapes=[pltpu.VMEM((1, LANES), F32), pltpu.VMEM((D_MODEL, D_MODEL), BF16)],
        compiler_params=_params(("arbitrary",)),
        name="merge_out",
    )(*oa_src, *oc_src, *os_src, *x_src, gn.reshape(1, D_MODEL), wo, fg.reshape(1, D_MODEL), wr_hi, wr_lo, br)


def _chunk_groups(tile_ref):
    n_chunk = tile_ref[TILE_NCHUNK, 0]
    return lax.shift_right_logical(n_chunk + (CHUNK_GROUP - 1), CHUNK_GROUP.bit_length() - 1)


def _for_each_chunk(tile_ref, glob_ref, fn):
    n_chunk = tile_ref[TILE_NCHUNK, 0]
    n_group = _chunk_groups(tile_ref)

    def group(g, carry):
        for k in range(CHUNK_GROUP):
            c = g * CHUNK_GROUP + k
            seg = glob_ref[GLOB_START * LANES + tile_ref[TILE_CHUNK_E, c]]
            fn(pl.multiple_of(c * RUN, RUN), c < n_chunk, seg + tile_ref[TILE_CHUNK_REL, c], k)
        return carry

    lax.fori_loop(0, n_group, group, 0)
    return n_group * CHUNK_GROUP


def _wait_chunks(n, wait_one):
    def group(g, carry):
        for _ in range(CHUNK_GROUP):
            wait_one()
        return carry

    lax.fori_loop(0, lax.shift_right_logical(n, CHUNK_GROUP.bit_length() - 1), group, 0)


def _dispatch_kernel(tile_ref, glob_ref, tokl_ref, xn_ref, xs_ref, sbuf_ref, zbuf_ref, inflight_ref, sem_z, sem_r):
    i = pl.program_id(0)
    last = pl.num_programs(0) - 1
    slot = lax.rem(i, 2)
    trash0 = xs_ref.shape[0] - TRASH_ROWS

    def zero_copy(t):
        return pltpu.make_async_copy(zbuf_ref, xs_ref.at[pl.ds(pl.multiple_of(t * TM_MOE, TM_MOE), TM_MOE)], sem_z)

    def for_zero_tiles(fn):
        def seg_last(k, c):
            t = glob_ref[GLOB_ZERO * LANES + k]

            @pl.when(t >= 0)
            def _():
                fn(t)
            return c

        def unused(t, c):
            fn(t)
            return c

        lax.fori_loop(0, LANES, seg_last, 0)
        lax.fori_loop(glob_ref[GLOB_NUSED * LANES], trash0 // TM_MOE, unused, 0)

    @pl.when(i == 0)
    def _():
        zbuf_ref[...] = jnp.zeros(zbuf_ref.shape, U32)
        trash = pltpu.make_async_copy(zbuf_ref.at[pl.ds(0, TRASH_ROWS)], xs_ref.at[pl.ds(trash0, TRASH_ROWS)], sem_z)
        trash.start()
        for_zero_tiles(lambda t: zero_copy(t).start())
        for_zero_tiles(lambda t: zero_copy(t).wait())
        trash.wait()
        inflight_ref[0] = 0

    q = lax.broadcasted_iota(I32, (N_LOCAL, 1), 0).astype(F32)
    hit = (q == tokl_ref[TOK_P1:TOK_P1 + 1, :]) | (q == tokl_ref[TOK_P2:TOK_P2 + 1, :])
    sel = jnp.where(hit, 1.0, 0.0).astype(BF16)
    a, b = _unpack_pair(xn_ref[...])
    sa = jnp.dot(sel, a.astype(BF16), preferred_element_type=F32)
    sb = jnp.dot(sel, b.astype(BF16), preferred_element_type=F32)
    sbuf_ref[slot] = _pack_pair(sa, sb)

    def start_chunk(local_row, real, sorted_row, k):
        dst = pl.multiple_of(jnp.where(real, sorted_row, trash0 + k * RUN), RUN)
        pltpu.make_async_copy(sbuf_ref.at[slot, pl.ds(local_row, RUN)], xs_ref.at[pl.ds(dst, RUN)],
                              sem_r).start(priority=k % 2)

    def drain(n):
        _wait_chunks(n, lambda: pltpu.make_async_copy(sbuf_ref.at[0, pl.ds(0, RUN)], xs_ref.at[pl.ds(0, RUN)],
                                                      sem_r).wait())

    drain(inflight_ref[0])
    n_issued = _for_each_chunk(tile_ref, glob_ref, start_chunk)
    inflight_ref[0] = n_issued

    @pl.when(i == last)
    def _():
        drain(n_issued)


def _dispatch(tile_tbl, glob_flat, tokl, xn, n_rows_sorted):
    n = xn.shape[0]
    return pl.pallas_call(
        _dispatch_kernel,
        grid=(n // TM_OUT,),
        in_specs=[pl.BlockSpec((SUBLANES, LANES), lambda i: (i, 0), memory_space=pltpu.SMEM),
                  pl.BlockSpec(memory_space=pltpu.SMEM),
                  pl.BlockSpec((SUBLANES, TM_OUT), lambda i: (i, 0)),
                  pl.BlockSpec((TM_OUT, D_PACK), lambda i: (i, 0))],
        out_specs=pl.BlockSpec(memory_space=pl.ANY),
        out_shape=jax.ShapeDtypeStruct((n_rows_sorted, D_PACK), U32),
        scratch_shapes=[pltpu.VMEM((2, N_LOCAL, D_PACK), U32), pltpu.VMEM((TM_MOE, D_PACK), U32),
                        pltpu.SMEM((1,), I32), pltpu.SemaphoreType.DMA(()), pltpu.SemaphoreType.DMA(())],
        compiler_params=_params(("arbitrary",)),
        name="moe_dispatch",
    )(tile_tbl, glob_flat, tokl, xn)


def _tile_expert(glob_ref, i):
    return glob_ref[GLOB_TE * LANES + i]


def _moe_kernel(glob_ref, xs_ref, wg_hbm, wu_hbm, wd_hbm, ys_ref, wgb_ref, wub_ref, wdb_ref,
                sg_ref, su_ref, sd_ref, ord_ref, sem, *, layer):
    i = pl.program_id(0)
    used = i < glob_ref[GLOB_NUSED * LANES]
    expert = _tile_expert(glob_ref, i)
    new_expert = (i == 0) | (expert != _tile_expert(glob_ref, jnp.maximum(i - 1, 0)))

    def weight_copies(e, slot):
        return [pltpu.make_async_copy(src.at[layer, e], dst.at[slot], sem.at[slot])
                for src, dst in ((wg_hbm, sg_ref), (wu_hbm, su_ref), (wd_hbm, sd_ref))]

    @pl.when(i == 0)
    def _():
        ord_ref[0] = 0
        for cp in weight_copies(expert, 0):
            cp.start()

    @pl.when(used & new_expert)
    def _():
        slot = lax.rem(ord_ref[0], 2)
        for cp in weight_copies(expert, slot):
            cp.wait()
        nxt = lax.while_loop(
            lambda k: (k < N_EXPERTS) & (glob_ref[GLOB_PTILES * LANES + ROUTER_LANE0 + jnp.minimum(k, N_EXPERTS - 1)] == 0),
            lambda k: k + 1, expert + 1)

        @pl.when(nxt < N_EXPERTS)
        def _():
            for cp in weight_copies(nxt, 1 - slot):
                cp.start()

        wgb_ref[...] = sg_ref[slot].astype(BF16)
        wub_ref[...] = su_ref[slot].astype(BF16)
        wdb_ref[...] = sd_ref[slot].astype(BF16)
        ord_ref[0] = ord_ref[0] + 1

    @pl.when(used)
    def _():
        a, b = _unpack_pair(xs_ref[...])
        x = jnp.concatenate([a, b], axis=1).astype(BF16)
        gate = jnp.dot(x, wgb_ref[...], preferred_element_type=F32)
        up = jnp.dot(x, wub_ref[...], preferred_element_type=F32)
        h = (gate * jax.nn.sigmoid(gate) * up).astype(BF16)
        y = jnp.dot(h, wdb_ref[...], preferred_element_type=F32)
        ys_ref[...] = _pack_pair(y[:, :D_PACK], y[:, D_PACK:])

    @pl.when(jnp.logical_not(used))
    def _():
        zero = jnp.zeros(ys_ref.shape, F32)
        ys_ref[...] = _pack_pair(zero, zero)


def _moe(glob_flat, xs, wg, wu, wd, layer):
    n_tiles = (xs.shape[0] - TRASH_ROWS) // TM_MOE
    assert n_tiles <= 2 * LANES
    hbm = pl.BlockSpec(memory_space=pl.ANY)
    grid_spec = pltpu.PrefetchScalarGridSpec(
        num_scalar_prefetch=1,
        grid=(n_tiles,),
        in_specs=[pl.BlockSpec((TM_MOE, D_PACK), lambda i, g: (jnp.minimum(i, g[GLOB_NUSED * LANES] - 1), 0)),
                  hbm, hbm, hbm],
        out_specs=pl.BlockSpec((TM_MOE, D_PACK), lambda i, g: (i, 0)),
        scratch_shapes=[pltpu.VMEM((D_MODEL, D_EXPERT), BF16), pltpu.VMEM((D_MODEL, D_EXPERT), BF16),
                        pltpu.VMEM((D_EXPERT, D_MODEL), BF16),
                        pltpu.VMEM((2, D_MODEL, D_EXPERT), F32), pltpu.VMEM((2, D_MODEL, D_EXPERT), F32),
                        pltpu.VMEM((2, D_EXPERT, D_MODEL), F32), pltpu.SMEM((1,), I32),
                        pltpu.SemaphoreType.DMA((2,))],
    )
    return pl.pallas_call(
        functools.partial(_moe_kernel, layer=layer),
        grid_spec=grid_spec,
        out_shape=jax.ShapeDtypeStruct((n_tiles * TM_MOE, D_PACK), U32),
        compiler_params=_params(("arbitrary",)),
        name="moe_experts",
    )(glob_flat, xs, wg, wu, wd)


def _combine_kernel(tile_ref, next_ref, glob_ref, tokc_ref, x1_ref, ys_ref, fg_ref, *rest, n_first, final):
    n_out = 2 if final else 1
    out_refs = rest[:n_out]
    lbuf_ref, inflight_ref, sem = rest[n_out:]
    i = pl.program_id(0)
    last = pl.num_programs(0) - 1
    slot = lax.rem(i, 2)

    def gather(tbl_ref, dst_slot):
        def start(local_row, real, sorted_row, k):
            src = pl.multiple_of(jnp.where(real, sorted_row, 0), RUN)
            pltpu.make_async_copy(ys_ref.at[pl.ds(src, RUN)], lbuf_ref.at[dst_slot, pl.ds(local_row, RUN)],
                                  sem.at[dst_slot]).start(priority=k % 2)

        inflight_ref[dst_slot] = _for_each_chunk(tbl_ref, glob_ref, start)

    @pl.when(i == 0)
    def _():
        lbuf_ref[...] = jnp.zeros(lbuf_ref.shape, U32)
        gather(tile_ref, 0)

    @pl.when(i < last)
    def _():
        gather(next_ref, 1 - slot)

    _wait_chunks(inflight_ref[slot], lambda: pltpu.make_async_copy(
        ys_ref.at[pl.ds(0, RUN)], lbuf_ref.at[slot, pl.ds(0, RUN)], sem.at[slot]).wait())

    tokc = tokc_ref[...]
    w1 = tokc[:, TOK_W1:TOK_W1 + 1]
    w2 = tokc[:, TOK_W2:TOK_W2 + 1]
    col = lax.broadcasted_iota(I32, (1, N_LOCAL), 1).astype(F32)
    sel1 = jnp.where(col == tokc[:, TOK_P1:TOK_P1 + 1], 1.0, 0.0).astype(BF16)
    sel2 = jnp.where(col == tokc[:, TOK_P2:TOK_P2 + 1], 1.0, 0.0).astype(BF16)
    halves = []
    for part in _unpack_pair(lbuf_ref[slot]):
        rows = part.astype(BF16)
        halves.append(w1 * jnp.dot(sel1, rows, preferred_element_type=F32)
                      + w2 * jnp.dot(sel2, rows, preferred_element_type=F32))
    x2 = x1_ref[...] + jnp.concatenate(halves, axis=1)
    if not final:
        out_refs[0][...] = x2
    else:
        y = _rms(x2, fg_ref[...])

        @pl.when(i < n_first)
        def _():
            out_refs[0][...] = y

        @pl.when(i >= n_first)
        def _():
            out_refs[1][...] = y


def _combine(tile_tbl, glob_flat, tokc, x1, ys, fg, n_first_rows, final):
    n = x1.shape[0]
    n_blk = n // TM_OUT
    n_first = n_first_rows // TM_OUT
    assert n_first_rows % TM_OUT == 0
    row = lambda width: pl.BlockSpec((TM_OUT, width), lambda i: (i, 0))
    if final:
        out_specs = [pl.BlockSpec((TM_OUT, D_MODEL), lambda i: (jnp.minimum(i, n_first - 1), 0)),
                     pl.BlockSpec((TM_OUT, D_MODEL), lambda i: (jnp.maximum(i - n_first, 0), 0))]
        out_shape = [jax.ShapeDtypeStruct((n_first_rows, D_MODEL), F32),
                     jax.ShapeDtypeStruct((n - n_first_rows, D_MODEL), F32)]
    else:
        out_specs = [row(D_MODEL)]
        out_shape = [jax.ShapeDtypeStruct((n, D_MODEL), F32)]
    return pl.pallas_call(
        functools.partial(_combine_kernel, n_first=n_first, final=final),
        grid=(n_blk,),
        in_specs=[pl.BlockSpec((SUBLANES, LANES), lambda i: (i, 0), memory_space=pltpu.SMEM),
                  pl.BlockSpec((SUBLANES, LANES), lambda i: (jnp.minimum(i + 1, n_blk - 1), 0), memory_space=pltpu.SMEM),
                  pl.BlockSpec(memory_space=pltpu.SMEM),
                  row(4), row(D_MODEL), pl.BlockSpec(memory_space=pl.ANY), _full((1, D_MODEL))],
        out_specs=out_specs,
        out_shape=out_shape,
        scratch_shapes=[pltpu.VMEM((2, N_LOCAL, D_PACK), U32), pltpu.SMEM((2,), I32),
                        pltpu.SemaphoreType.DMA((2,))],
        compiler_params=_params(("arbitrary",)),
        name="moe_combine",
    )(tile_tbl, tile_tbl, glob_flat, tokc, x1, ys, fg.reshape(1, D_MODEL))


def _n_moe_tiles(n_tokens):
    n_runs = (n_tokens // TM_OUT) * N_EXPERTS
    return (2 * n_tokens + n_runs * (RUN - 1) + N_EXPERTS * (TM_MOE - 1)) // TM_MOE + 1


def kernel(x_prompt, x_sample, cache_k, cache_v, state_conv, state_ssm_re, state_ssm_im, attn_norm_g, w_in, attn_sinks, conv_w, conv_b, conv_ln_g, conv_ln_b, ssm_a_re, ssm_a_im, ssm_log_dt, ssm_b_re, ssm_b_im, ssm_c_re, ssm_c_im, ssm_d, ssm_glu_w, ssm_glu_b, grp_norm_g, w_out, ffn_norm_g, w_group_router, b_group_router, w_expert_router, b_expert_router, w_gate, w_up, w_down, final_norm_g):
    p = dict(ssm_a_re=ssm_a_re, ssm_a_im=ssm_a_im, ssm_log_dt=ssm_log_dt, ssm_b_re=ssm_b_re, ssm_b_im=ssm_b_im,
             ssm_c_re=ssm_c_re, ssm_c_im=ssm_c_im, ssm_d=ssm_d, ssm_glu_w=ssm_glu_w, ssm_glu_b=ssm_glu_b)
    depth = w_in.shape[0]
    n_seq, t, _ = x_prompt.shape
    n_dec, s_new, _ = x_sample.shape
    win = cache_k.shape[2]
    n_p = n_seq * t
    n_s = n_dec * s_new
    n = n_p + n_s
    assert n_seq == 2 and t % SCAN_T == 0 and t % CONV_T == 0 and n_p % TM_IN == 0 and n_s % TM_IN == 0
    hist = CONV_WIDTH - 1
    n_tiles = _n_moe_tiles(n)

    x_src = [x_prompt.reshape(n_p, D_MODEL), x_sample.reshape(n_s, D_MODEL)]
    outs = {k: [] for k in ("kp", "vp", "cp", "rp", "ip", "kv_new", "u_new", "rs", "is")}
    ck_all = cache_k.reshape(depth * n_dec, win, D_KV)
    cv_all = cache_v.reshape(depth * n_dec, win, D_KV)
    conv_all = state_conv.reshape(depth * n_dec, hist * D_CONV)

    def last_rows(z, k):
        return jnp.stack([z[(s + 1) * t - k:(s + 1) * t] for s in range(n_seq)])

    for l in range(depth):
        q, kv, u, su = _in_proj(x_src, attn_norm_g[l], w_in, l)

        oa_p = _attn_prompt(q, kv, attn_sinks[l], n_seq, t)
        oa_s = _attn_sample(q, kv, ck_all, cv_all, attn_sinks[l], s_new, n_p, n_dec, l * n_dec)
        kv_p = last_rows(kv, win).astype(F32)
        outs["kp"].append(kv_p[..., :D_KV].reshape(n_seq, win, N_KV_HEADS, HEAD_DIM))
        outs["vp"].append(kv_p[..., D_KV:].reshape(n_seq, win, N_KV_HEADS, HEAD_DIM))
        outs["kv_new"].append(kv[n_p:].reshape(n_dec, s_new, 2 * D_KV))

        oc_p = _conv_prompt(u, conv_w[l], conv_b[l], conv_ln_g[l], conv_ln_b[l], n_seq, t)
        u_s = u[n_p:].reshape(n_dec, s_new, D_CONV)
        oc_s = _conv_sample(conv_all, u_s.reshape(n_dec, s_new * D_CONV),
                            conv_w[l], conv_b[l], conv_ln_g[l], conv_ln_b[l], s_new, l)
        outs["cp"].append(last_rows(u, hist))
        outs["u_new"].append(u_s)

        sp = _s5_params(p, l)
        os0, os1, hre, him = _s5_prompt(su, sp, t)
        su_tb = su[n_p:].reshape(n_dec, s_new, D_SSM).transpose(1, 0, 2).reshape(n_s, D_SSM)
        os_tb, hr_s, hi_s = _s5_sample(su_tb, state_ssm_re[l].reshape(n_dec, N_STATE),
                                       state_ssm_im[l].reshape(n_dec, N_STATE), sp, s_new)
        os_s = os_tb.reshape(s_new, n_dec, D_SSM).transpose(1, 0, 2).reshape(n_s, D_SSM)
        outs["rp"].append(hre.reshape(n_seq, SSM_GROUPS, SSM_STATE))
        outs["ip"].append(him.reshape(n_seq, SSM_GROUPS, SSM_STATE))
        outs["rs"].append(hr_s.reshape(n_dec, SSM_GROUPS, SSM_STATE))
        outs["is"].append(hi_s.reshape(n_dec, SSM_GROUPS, SSM_STATE))

        unused = LANES - N_EXPERT_GROUPS - N_EXPERTS
        wr = jnp.pad(jnp.concatenate([w_group_router[l], w_expert_router[l]], axis=1), ((0, 0), (0, unused)))
        wr_hi = wr.astype(BF16)
        wr_lo = (wr - wr_hi.astype(F32)).astype(BF16)
        br = jnp.pad(jnp.concatenate([b_group_router[l], b_expert_router[l]]), (0, unused)).reshape(1, LANES)
        x1, xn, tokc, tokl, tile_tbl, glob = _merge_out(
            [oa_p, oa_s], [oc_p, oc_s.reshape(n_s, D_CONV)], [os0, os1, os_s], x_src,
            grp_norm_g[l], w_out, l, ffn_norm_g[l], wr_hi, wr_lo, br)

        glob_flat = glob.reshape(SUBLANES * LANES)
        xs = _dispatch(tile_tbl, glob_flat, tokl, xn, n_tiles * TM_MOE + TRASH_ROWS)
        ys = _moe(glob_flat, xs, w_gate, w_up, w_down, l)
        res = _combine(tile_tbl, glob_flat, tokc, x1, ys, final_norm_g, n_p, final=(l == depth - 1))
        x_src = [res[0]]

    y_p = res[0].reshape(n_seq, t, D_MODEL)
    y_s = res[1].reshape(n_dec, s_new, D_MODEL)
    st = lambda k: jnp.stack(outs[k])
    kv_new = st("kv_new").astype(F32)
    heads = lambda z: z.reshape(depth, n_dec, s_new, N_KV_HEADS, HEAD_DIM)
    k_s = jnp.concatenate([cache_k[:, :, s_new:], heads(kv_new[..., :D_KV])], axis=2)
    v_s = jnp.concatenate([cache_v[:, :, s_new:], heads(kv_new[..., D_KV:])], axis=2)
    conv_s = jnp.concatenate([state_conv[:, :, s_new:], st("u_new")], axis=2)
    return (y_p, y_s, st("kp"), st("vp"), st("cp"), st("rp"), st("ip"), k_s, v_s, conv_s, st("rs"), st("is"))
```
